```python
import math
import jax, jax.numpy as jnp
from jax import lax
import numpy as np

D_MODEL = 1024
BATCH = 16
SEQ = 4096
DEPTH = 1
DEC_BATCH = 2
DEC_SEQ = 16384
PAST_LEN = 128

ATT_HEADS = 4
ATT_QK_DIM = 64
ATT_V_DIM = 128
ATT_WIDTH = ATT_HEADS * ATT_V_DIM
Q_BLOCK = 128
HG_HEADS = 4
HG_KEY_DIM = 128
HG_VAL_DIM = 128
HG_WIDTH = HG_HEADS * HG_KEY_DIM
HG_OUT_WIDTH = HG_HEADS * HG_VAL_DIM
HG_CHUNK = 64
MIX_WIDTH = ATT_WIDTH + HG_OUT_WIDTH
PROJ_SIZES = (ATT_HEADS * 2 * ATT_QK_DIM, ATT_HEADS * 2 * ATT_QK_DIM, ATT_WIDTH,
              HG_WIDTH, HG_WIDTH, HG_WIDTH, HG_OUT_WIDTH, HG_OUT_WIDTH)
PROJ_WIDTH = 4096
N_EXPERTS = 256
TOP_K = 8
N_GROUPS = 8
TOPK_GROUPS = 4
EXPERT_DIM = 256
SHARED_DIM = 256
ROUTE_SCALE = 2.5
DISPATCH_BLOCK = 128
EPS = 1e-6

kernel_name = 'hybrid_diffattn_hgrn2_moe_adaln_encoder'


def rms_norm(x, g):
    xf = x.astype(jnp.float32)
    y = xf * lax.rsqrt(jnp.mean(xf * xf, axis=-1, keepdims=True) + EPS)
    return (y * g.astype(jnp.float32)).astype(x.dtype)


def alibi_slopes(n):
    return jnp.exp2(-8.0 * (jnp.arange(n, dtype=jnp.float32) + 1.0) / n)


def diff_attention(q, k, v, lam):
    B, H, _, S, DK = q.shape
    DV = v.shape[-1]
    slopes = alibi_slopes(H)
    kpos = jnp.arange(S)
    scale = DK ** -0.5

    def block(i):
        start = i * Q_BLOCK
        qb = lax.dynamic_slice_in_dim(q, start, Q_BLOCK, axis=3)
        qpos = start + jnp.arange(Q_BLOCK)
        dist = jnp.abs(qpos[:, None] - kpos[None, :]).astype(jnp.float32)
        bias = -slopes[:, None, None, None] * dist
        s = jnp.einsum('bhmqd,bhmkd->bhmqk', qb, k).astype(jnp.float32) * scale + bias
        p = jax.nn.softmax(s, axis=-1)
        w = p[:, :, 0] - lam * p[:, :, 1]
        return jnp.einsum('bhqk,bhkd->bhqd', w.astype(v.dtype), v)

    o = lax.map(block, jnp.arange(S // Q_BLOCK))
    return o.transpose(1, 2, 0, 3, 4).reshape(B, H, S, DV)


def hgrn2_scan(q, k, v, g):
    B, H, S, DK = q.shape
    DV = v.shape[-1]
    n = S // HG_CHUNK

    def to_chunks(t):
        return t.reshape(B, H, n, HG_CHUNK, t.shape[-1]).transpose(2, 0, 1, 3, 4)

    mask = jnp.tril(jnp.ones((HG_CHUNK, HG_CHUNK), dtype=bool))[:, :, None]

    def step(state, inp):
        qb, kb, vb, gb = inp
        b = jnp.cumsum(gb, axis=2)
        o_inter = jnp.einsum('bhtk,bhkv->bhtv', qb * jnp.exp(b), state)
        diff = b[:, :, :, None, :] - b[:, :, None, :, :]
        decay = jnp.exp(jnp.where(mask, diff, -jnp.inf))
        a = jnp.einsum('bhtk,bhtsk,bhsk->bhts', qb, decay, kb)
        o_intra = jnp.einsum('bhts,bhsv->bhtv', a, vb)
        b_last = b[:, :, -1:, :]
        new_state = jnp.exp(b_last[:, :, 0, :])[..., None] * state + jnp.einsum(
            'bhsk,bhsv->bhkv', kb * jnp.exp(b_last - b), vb)
        return new_state, o_inter + o_intra

    s0 = jnp.zeros((B, H, DK, DV), jnp.float32)
    _, o = lax.scan(step, s0, (to_chunks(q), to_chunks(k), to_chunks(v), to_chunks(g)))
    return o.transpose(1, 2, 0, 3, 4).reshape(B, H, S, DV)


def token_mixer(h, w_in, w_out, lam_q1, lam_k1, lam_q2, lam_k2, lam_init, subln_g, lb, hg_norm_g):
    B, S, _ = h.shape
    proj = h @ w_in
    parts = []
    off = 0
    for size in PROJ_SIZES:
        parts.append(proj[..., off:off + size])
        off += size
    q_a, k_a, v_a, q_h, f_fw, f_bw, i_h, g_h = parts

    q = q_a.reshape(B, S, ATT_HEADS, 2, ATT_QK_DIM).transpose(0, 2, 3, 1, 4)
    k = k_a.reshape(B, S, ATT_HEADS, 2, ATT_QK_DIM).transpose(0, 2, 3, 1, 4)
    v = v_a.reshape(B, S, ATT_HEADS, ATT_V_DIM).transpose(0, 2, 1, 3)
    lam = (jnp.exp(jnp.sum(lam_q1.astype(jnp.float32) * lam_k1.astype(jnp.float32)))
           - jnp.exp(jnp.sum(lam_q2.astype(jnp.float32) * lam_k2.astype(jnp.float32)))
           + lam_init)
    o_att = diff_attention(q, k, v, lam)
    o_att = rms_norm(o_att, subln_g) * (1.0 - lam_init)
    o_att = o_att.transpose(0, 2, 1, 3).reshape(B, S, ATT_WIDTH)

    def heads(t, d):
        return t.astype(jnp.float32).reshape(B, S, HG_HEADS, d).transpose(0, 2, 1, 3)

    qh = jax.nn.silu(heads(q_h, HG_KEY_DIM))
    ih = heads(i_h, HG_VAL_DIM)
    lbf = lb.astype(jnp.float32).reshape(2, 1, HG_HEADS, 1, HG_KEY_DIM)
    f_f = lbf[0] + (1.0 - lbf[0]) * jax.nn.sigmoid(heads(f_fw, HG_KEY_DIM))
    f_b = lbf[1] + (1.0 - lbf[1]) * jax.nn.sigmoid(heads(f_bw, HG_KEY_DIM))
    o_fw = hgrn2_scan(qh, 1.0 - f_f, ih, jnp.log(f_f))
    flip = lambda t: jnp.flip(t, axis=2)
    o_bw = flip(hgrn2_scan(flip(qh), flip(1.0 - f_b), flip(ih), flip(jnp.log(f_b))))
    o_hg = rms_norm(o_fw + o_bw, hg_norm_g) * jax.nn.silu(heads(g_h, HG_VAL_DIM))
    o_hg = o_hg.transpose(0, 2, 1, 3).reshape(B, S, HG_OUT_WIDTH).astype(h.dtype)

    return jnp.concatenate([o_att.astype(h.dtype), o_hg], axis=-1) @ w_out


def moe(h, w_router, b_router, wg, wu, wd, swg, swu, swd):
    T, D = h.shape
    scores = jax.nn.sigmoid(h.astype(jnp.float32) @ w_router.astype(jnp.float32))
    biased = scores + b_router.astype(jnp.float32)
    grp_scores = lax.top_k(biased.reshape(T, N_GROUPS, N_EXPERTS // N_GROUPS), 2)[0].sum(-1)
    _, gidx = lax.top_k(grp_scores, TOPK_GROUPS)
    gmask = jnp.any(gidx[:, :, None] == jnp.arange(N_GROUPS)[None, None, :], axis=1)
    emask = jnp.repeat(gmask, N_EXPERTS // N_GROUPS, axis=1)
    _, eidx = lax.top_k(jnp.where(emask, biased, -jnp.inf), TOP_K)
    wts = jnp.take_along_axis(scores, eidx, axis=1)
    wts = wts / (jnp.sum(wts, axis=-1, keepdims=True) + 1e-20) * ROUTE_SCALE

    A = T * TOP_K
    e_flat = eidx.reshape(A)
    tok_flat = jnp.repeat(jnp.arange(T, dtype=jnp.int32), TOP_K)
    w_flat = wts.reshape(A)
    order = jnp.argsort(e_flat, stable=True)
    e_s, tok_s, w_s = e_flat[order], tok_flat[order], w_flat[order]
    counts = jnp.bincount(e_flat, length=N_EXPERTS)
    starts = jnp.cumsum(counts) - counts
    padded = (counts + DISPATCH_BLOCK - 1) // DISPATCH_BLOCK * DISPATCH_BLOCK
    pends = jnp.cumsum(padded)
    pstarts = pends - padded
    dest = pstarts[e_s] + jnp.arange(A) - starts[e_s]
    P = A + N_EXPERTS * DISPATCH_BLOCK
    nb = P // DISPATCH_BLOCK
    tok_buf = jnp.full((P,), T, jnp.int32).at[dest].set(tok_s)
    w_buf = jnp.zeros((P,), jnp.float32).at[dest].set(w_s)
    blk_e = jnp.minimum(jnp.searchsorted(pends, jnp.arange(nb) * DISPATCH_BLOCK, side='right'),
                        N_EXPERTS - 1)
    h_pad = jnp.concatenate([h, jnp.zeros((1, D), h.dtype)], axis=0)

    def step(y, inp):
        tok, wt, e = inp
        xb = h_pad[tok]
        a = jax.nn.silu(xb @ wg[e]) * (xb @ wu[e])
        out = ((a @ wd[e]).astype(jnp.float32) * wt[:, None]).astype(y.dtype)
        return y.at[tok].add(out), None

    y, _ = lax.scan(step, jnp.zeros((T + 1, D), h.dtype),
                    (tok_buf.reshape(nb, DISPATCH_BLOCK), w_buf.reshape(nb, DISPATCH_BLOCK), blk_e))
    shared = (jax.nn.silu(h @ swg) * (h @ swu)) @ swd
    return y[:T] + shared


def trunk(x, c, w_ada, b_ada, norm1_g, w_in, lam_q1, lam_k1, lam_q2, lam_k2, subln_g,
          hg_lb_logits, hg_norm_g, w_out, norm2_g, w_router, b_router, w_exp_gate, w_exp_up,
          w_exp_down, w_sh_gate, w_sh_up, w_sh_down, w_ada_f, b_ada_f, normf_g):
    B, S, D = x.shape
    lbs = jnp.cumsum(jax.nn.softmax(hg_lb_logits.astype(jnp.float32), axis=0), axis=0)
    cs = jax.nn.silu(c)
    for l in range(DEPTH):
        lam_init = 0.8 - 0.6 * math.exp(-0.3 * l)
        mod = (cs @ w_ada[l] + b_ada[l])[:, None, :]
        sh1, sc1, g1, sh2, sc2, g2 = jnp.split(mod, 6, axis=-1)
        h = rms_norm(x, norm1_g[l]) * (1.0 + sc1) + sh1
        x = x + g1 * token_mixer(h, w_in[l], w_out[l], lam_q1[l], lam_k1[l], lam_q2[l], lam_k2[l],
                                 lam_init, subln_g[l], lbs[l], hg_norm_g[l])
        h = rms_norm(x, norm2_g[l]) * (1.0 + sc2) + sh2
        y = moe(h.reshape(B * S, D), w_router[l], b_router[l], w_exp_gate[l], w_exp_up[l],
                w_exp_down[l], w_sh_gate[l], w_sh_up[l], w_sh_down[l]).reshape(B, S, D)
        x = x + g2 * y
    modf = (cs @ w_ada_f + b_ada_f)[:, None, :]
    shf, scf = jnp.split(modf, 2, axis=-1)
    return rms_norm(x, normf_g) * (1.0 + scf) + shf


def setup_inputs(seed: int = 0) -> dict:
    key = jax.random.key(seed)
    ks = jax.random.split(key, 32)
    D = D_MODEL

    def nrm(k, shape, scale):
        return jax.random.normal(k, shape, jnp.float32) * scale

    return {
        'x_prompt': nrm(ks[0], (BATCH, SEQ, D), 1.0),
        'x_sample': nrm(ks[1], (DEC_BATCH, DEC_SEQ, D), 1.0),
        'c_prompt': nrm(ks[2], (BATCH, D), 1.0),
        'c_sample': nrm(ks[3], (DEC_BATCH, D), 1.0),
        'w_ada': nrm(ks[4], (DEPTH, D, 6 * D), 0.5 * D ** -0.5),
        'b_ada': nrm(ks[5], (DEPTH, 6 * D), 0.02),
        'norm1_g': 1.0 + nrm(ks[6], (DEPTH, D), 0.02),
        'w_in': nrm(ks[7], (DEPTH, D, PROJ_WIDTH), D ** -0.5),
        'lam_q1': nrm(ks[8], (DEPTH, ATT_QK_DIM), 0.1),
        'lam_k1': nrm(ks[9], (DEPTH, ATT_QK_DIM), 0.1),
        'lam_q2': nrm(ks[10], (DEPTH, ATT_QK_DIM), 0.1),
        'lam_k2': nrm(ks[11], (DEPTH, ATT_QK_DIM), 0.1),
        'subln_g': 1.0 + nrm(ks[12], (DEPTH, ATT_V_DIM), 0.02),
        'hg_lb_logits': nrm(ks[13], (DEPTH + 1, 2, HG_WIDTH), 0.5),
        'hg_norm_g': 1.0 + nrm(ks[14], (DEPTH, HG_VAL_DIM), 0.02),
        'w_out': nrm(ks[15], (DEPTH, MIX_WIDTH, D), MIX_WIDTH ** -0.5),
        'norm2_g': 1.0 + nrm(ks[16], (DEPTH, D), 0.02),
        'w_router': nrm(ks[17], (DEPTH, D, N_EXPERTS), D ** -0.5),
        'b_router': nrm(ks[18], (DEPTH, N_EXPERTS), 0.01),
        'w_exp_gate': nrm(ks[19], (DEPTH, N_EXPERTS, D, EXPERT_DIM), D ** -0.5),
        'w_exp_up': nrm(ks[20], (DEPTH, N_EXPERTS, D, EXPERT_DIM), D ** -0.5),
        'w_exp_down': nrm(ks[21], (DEPTH, N_EXPERTS, EXPERT_DIM, D), EXPERT_DIM ** -0.5),
        'w_sh_gate': nrm(ks[22], (DEPTH, D, SHARED_DIM), D ** -0.5),
        'w_sh_up': nrm(ks[23], (DEPTH, D, SHARED_DIM), D ** -0.5),
        'w_sh_down': nrm(ks[24], (DEPTH, SHARED_DIM, D), SHARED_DIM ** -0.5),
        'w_ada_f': nrm(ks[25], (D, 2 * D), 0.5 * D ** -0.5),
        'b_ada_f': nrm(ks[26], (2 * D,), 0.02),
        'normf_g': 1.0 + nrm(ks[27], (D,), 0.02),
    }


def reference(x_prompt, x_sample, c_prompt, c_sample, w_ada, b_ada, norm1_g, w_in, lam_q1, lam_k1,
              lam_q2, lam_k2, subln_g, hg_lb_logits, hg_norm_g, w_out, norm2_g, w_router, b_router,
              w_exp_gate, w_exp_up, w_exp_down, w_sh_gate, w_sh_up, w_sh_down, w_ada_f, b_ada_f,
              normf_g):
    y_prompt = trunk(x_prompt, c_prompt, w_ada, b_ada, norm1_g, w_in, lam_q1, lam_k1, lam_q2, lam_k2,
                     subln_g, hg_lb_logits, hg_norm_g, w_out, norm2_g, w_router, b_router, w_exp_gate,
                     w_exp_up, w_exp_down, w_sh_gate, w_sh_up, w_sh_down, w_ada_f, b_ada_f, normf_g)
    y_sample = trunk(x_sample, c_sample, w_ada, b_ada, norm1_g, w_in, lam_q1, lam_k1, lam_q2, lam_k2,
                     subln_g, hg_lb_logits, hg_norm_g, w_out, norm2_g, w_router, b_router, w_exp_gate,
                     w_exp_up, w_exp_down, w_sh_gate, w_sh_up, w_sh_down, w_ada_f, b_ada_f, normf_g)
    return (y_prompt, y_sample)
```

```python
import functools
import math

import jax
import jax.numpy as jnp
from jax import lax
from jax.experimental import pallas as pl
from jax.experimental.pallas import tpu as pltpu
from jax.experimental.pallas import tpu_sc as plsc

F32 = jnp.float32
BF16 = jnp.bfloat16
I32 = jnp.int32

EPS = 1e-6
LOG2E = 1.4426950408889634

ATT_HEADS = 4
ATT_QK_DIM = 64
HEAD_W = 128
ATT_WIDTH = ATT_HEADS * HEAD_W
HG_HEADS = 4
HG_WIDTH = HG_HEADS * HEAD_W
N_EXPERTS = 256
TOP_K = 8
N_GROUPS = 8
TOPK_GROUPS = 4
GROUP_SIZE = N_EXPERTS // N_GROUPS
EXPERT_DIM = 256
ROUTE_SCALE = 2.5
LAM_INIT = 0.8 - 0.6 * math.exp(-0.3 * 0)

HG_CHUNK = 128
HG_SUB = 16
EXPERT_BLOCK = 256
SC_CORES = 2
SC_SUBCORES = 16
SC_WORKERS = SC_CORES * SC_SUBCORES
SC_WINDOW = 32
VMEM_LIMIT = 48 * 1024 * 1024

NEG_INF = float("-inf")


def _cparams(sem):
    return pltpu.CompilerParams(dimension_semantics=sem, vmem_limit_bytes=VMEM_LIMIT)


def _silu(x):
    return x * jax.nn.sigmoid(x)


def _ada_kernel(c_ref, w_ref, b_ref, o_ref):
    cs = _silu(c_ref[...])
    o_ref[...] = jnp.dot(cs, w_ref[...], preferred_element_type=F32,
                         precision=lax.Precision.HIGHEST) + b_ref[...]


def _ada(c, w, b):
    R, D = c.shape
    N = w.shape[1]
    tn = 1024
    return pl.pallas_call(
        _ada_kernel,
        out_shape=jax.ShapeDtypeStruct((R, N), F32),
        grid=(N // tn,),
        in_specs=[pl.BlockSpec((R, D), lambda j: (0, 0)),
                  pl.BlockSpec((D, tn), lambda j: (0, j)),
                  pl.BlockSpec((1, tn), lambda j: (0, j))],
        out_specs=pl.BlockSpec((R, tn), lambda j: (0, j)),
        compiler_params=_cparams(("arbitrary",)),
        name="ada_mod",
    )(c, w, b)


def _rms(x):
    return x * lax.rsqrt(jnp.mean(x * x, axis=-1, keepdims=True) + EPS)


def _proj_kernel(x_ref, mod_ref, g_ref, w_ref, o_ref, *, qscale):
    x = x_ref[...]
    mod = mod_ref[0]
    h = _rms(x) * g_ref[...] * (1.0 + mod[1:2]) + mod[0:1]
    acc = jnp.dot(h.astype(BF16), w_ref[...], preferred_element_type=F32)
    qw = ATT_WIDTH
    o_ref[:, :qw] = (acc[:, :qw] * qscale).astype(BF16)
    o_ref[:, qw:] = acc[:, qw:].astype(BF16)


def _proj(xt, mod, g, w, S, tm):
    T, D = xt.shape
    N = w.shape[1]
    qscale = ATT_QK_DIM ** -0.5 * LOG2E
    return pl.pallas_call(
        functools.partial(_proj_kernel, qscale=qscale),
        out_shape=jax.ShapeDtypeStruct((T, N), BF16),
        grid=(T // tm,),
        in_specs=[pl.BlockSpec((tm, D), lambda i: (i, 0)),
                  pl.BlockSpec((1, 6, D), lambda i: ((i * tm) // S, 0, 0)),
                  pl.BlockSpec((1, D), lambda i: (0, 0)),
                  pl.BlockSpec((D, N), lambda i: (0, 0))],
        out_specs=pl.BlockSpec((tm, N), lambda i: (i, 0)),
        compiler_params=_cparams(("parallel",)),
        name="norm1_proj",
    )(xt, mod, g, w)


def _attn_kernel(slope_ref, lam_ref, q_ref, k_ref, v_ref, g_ref, o_ref,
                 qm_ref, m_ref, l_ref, acc_ref, *, tq, tk, nk):
    h = pl.program_id(1)
    qi = pl.program_id(2)
    ki = pl.program_id(3)

    @pl.when(ki == 0)
    def _init():
        q = q_ref[...]
        lane = lax.broadcasted_iota(I32, q.shape, 1)
        zero = jnp.zeros_like(q)
        qm_ref[0] = jnp.where(lane < ATT_QK_DIM, q, zero)
        qm_ref[1] = jnp.where(lane >= ATT_QK_DIM, q, zero)
        m_ref[...] = jnp.full(m_ref.shape, NEG_INF, F32)
        l_ref[...] = jnp.zeros(l_ref.shape, F32)
        acc_ref[...] = jnp.zeros(acc_ref.shape, F32)

    slope = slope_ref[h]
    k = k_ref[...]
    v = v_ref[...]
    row = lax.broadcasted_iota(I32, (tk, tq), 0)
    col = lax.broadcasted_iota(I32, (tk, tq), 1)
    dist = (col - row + (qi * tq - ki * tk)).astype(F32)
    bias = jnp.abs(dist) * (-slope)
    for m in range(2):
        s = lax.dot_general(k, qm_ref[m], (((1,), (1,)), ((), ())),
                            preferred_element_type=F32) + bias
        m_old = m_ref[m]
        m_new = jnp.maximum(m_old, jnp.max(s, axis=0, keepdims=True))
        alpha = jnp.exp2(m_old - m_new)
        p = jnp.exp2(s - m_new)
        l_ref[m] = alpha * l_ref[m] + jnp.sum(p, axis=0, keepdims=True)
        pv = lax.dot_general(v, p.astype(BF16), (((0,), (0,)), ((), ())),
                             preferred_element_type=F32)
        acc_ref[m] = alpha * acc_ref[m] + pv
        m_ref[m] = m_new

    @pl.when(ki == nk - 1)
    def _fin():
        lp = lam_ref[...]
        lam = (jnp.exp(jnp.sum(lp[0:1] * lp[1:2], axis=-1, keepdims=True))
               - jnp.exp(jnp.sum(lp[2:3] * lp[3:4], axis=-1, keepdims=True))
               + LAM_INIT)
        o = acc_ref[0] / l_ref[0] - lam * (acc_ref[1] / l_ref[1])
        ms = jnp.mean(o * o, axis=0, keepdims=True)
        y = o * lax.rsqrt(ms + EPS) * g_ref[...] * (1.0 - LAM_INIT)
        o_ref[...] = y.T.astype(o_ref.dtype)


def _attention(proj, slopes, lam_p, subln_g, B, S, tq, tk):
    T = B * S
    nq, nk = S // tq, S // tk
    kcol = ATT_WIDTH // HEAD_W
    vcol = 2 * ATT_WIDTH // HEAD_W
    grid_spec = pltpu.PrefetchScalarGridSpec(
        num_scalar_prefetch=1,
        grid=(B, ATT_HEADS, nq, nk),
        in_specs=[
            pl.BlockSpec((4, ATT_QK_DIM), lambda b, h, qi, ki, s: (0, 0)),
            pl.BlockSpec((tq, HEAD_W), lambda b, h, qi, ki, s: (b * nq + qi, h)),
            pl.BlockSpec((tk, HEAD_W), lambda b, h, qi, ki, s: (b * nk + ki, kcol + h)),
            pl.BlockSpec((tk, HEAD_W), lambda b, h, qi, ki, s: (b * nk + ki, vcol + h)),
            pl.BlockSpec((HEAD_W, 1), lambda b, h, qi, ki, s: (0, 0)),
        ],
        out_specs=pl.BlockSpec((tq, HEAD_W), lambda b, h, qi, ki, s: (b * nq + qi, h)),
        scratch_shapes=[pltpu.VMEM((2, tq, HEAD_W), BF16),
                        pltpu.VMEM((2, 1, tq), F32),
                        pltpu.VMEM((2, 1, tq), F32),
                        pltpu.VMEM((2, HEAD_W, tq), F32)],
    )
    return pl.pallas_call(
        functools.partial(_attn_kernel, tq=tq, tk=tk, nk=nk),
        out_shape=jax.ShapeDtypeStruct((T, ATT_WIDTH), BF16),
        grid_spec=grid_spec,
        compiler_params=_cparams(("parallel", "parallel", "parallel", "arbitrary")),
        name="diff_attention",
    )(slopes, lam_p, proj, proj, proj, subln_g)


def _hg_chunk(q, kk, v, g, st, reverse):
    C = q.shape[0]
    nsub = C // HG_SUB
    row = lax.broadcasted_iota(I32, (C, HEAD_W), 0)
    b = g
    d = 1
    while d < C:
        if not reverse:
            b = b + jnp.where(row >= d, pltpu.roll(b, d, axis=0), 0.0)
        else:
            b = b + jnp.where(row < C - d, pltpu.roll(b, C - d, axis=0), 0.0)
        d *= 2
    b_end = b[C - 1:C] if not reverse else b[0:1]

    qd = (q * jnp.exp(b)).astype(BF16)
    o_inter = lax.dot_general(qd, st.astype(BF16), (((1,), (1,)), ((), ())),
                              preferred_element_type=F32)

    ones = jnp.ones((HEAD_W, HEAD_W), BF16)
    srow = lax.broadcasted_iota(I32, (HG_SUB, HEAD_W), 0)
    vb = v.astype(BF16)
    outs = []
    for blk in range(nsub):
        lo, hi = blk * HG_SUB, (blk + 1) * HG_SUB
        qI, kI, vI, bI = q[lo:hi], kk[lo:hi], v[lo:hi], b[lo:hi]
        pieces = []
        for t in range(HG_SUB):
            dl = bI[t:t + 1] - bI
            keep = (srow <= t) if not reverse else (srow >= t)
            e = jnp.exp(jnp.where(keep, dl, NEG_INF))
            pieces.append(qI[t:t + 1] * kI * e)
        wst = jnp.concatenate(pieces, axis=0).astype(BF16)
        rsum = jnp.dot(wst, ones, preferred_element_type=F32)
        o_blk = jnp.sum(rsum.reshape(HG_SUB, HG_SUB, HEAD_W) * vI[None], axis=1)
        has_off = (blk > 0) if not reverse else (blk < nsub - 1)
        if has_off:
            if not reverse:
                r = b[lo - 1:lo]
                kmask = row < lo
            else:
                r = b[hi:hi + 1]
                kmask = row >= hi
            qs = (qI * jnp.exp(bI - r)).astype(BF16)
            ks = (kk * jnp.exp(jnp.where(kmask, r - b, NEG_INF))).astype(BF16)
            a = lax.dot_general(qs, ks, (((1,), (1,)), ((), ())),
                                preferred_element_type=F32)
            o_blk = o_blk + jnp.dot(a.astype(BF16), vb, preferred_element_type=F32)
        outs.append(o_blk)
    o = o_inter + jnp.concatenate(outs, axis=0)

    kd = (kk * jnp.exp(b_end - b)).astype(BF16)
    upd = lax.dot_general(vb, kd, (((0,), (0,)), ((), ())),
                          preferred_element_type=F32)
    st_new = st * jnp.exp(b_end) + upd
    return o, st_new


def _hg_kernel(qf_ref, ff_ref, if_ref, qb_ref, fb_ref, ib_ref, lb_ref,
               of_ref, ob_ref, sf_ref, sb_ref):
    j = pl.program_id(2)

    @pl.when(j == 0)
    def _init():
        sf_ref[...] = jnp.zeros(sf_ref.shape, F32)
        sb_ref[...] = jnp.zeros(sb_ref.shape, F32)

    lb = lb_ref[0]

    def prep(q_ref, f_ref, i_ref, lbd):
        q = _silu(q_ref[...].astype(F32))
        f = lbd + (1.0 - lbd) * jax.nn.sigmoid(f_ref[...].astype(F32))
        return q, 1.0 - f, i_ref[...].astype(F32), jnp.log(f)

    q, kk, v, g = prep(qf_ref, ff_ref, if_ref, lb[0:1])
    o, st = _hg_chunk(q, kk, v, g, sf_ref[...], reverse=False)
    of_ref[...] = o
    sf_ref[...] = st

    q, kk, v, g = prep(qb_ref, fb_ref, ib_ref, lb[1:2])
    o, st = _hg_chunk(q, kk, v, g, sb_ref[...], reverse=True)
    ob_ref[...] = o
    sb_ref[...] = st


def _hgrn2(proj, lbs, B, S):
    T = B * S
    C = HG_CHUNK
    n = S // C
    base = 3 * ATT_WIDTH // HEAD_W
    nh = HG_HEADS
    qc, ffc, fbc, ic = base, base + nh, base + 2 * nh, base + 3 * nh

    def fw(col):
        return pl.BlockSpec((C, HEAD_W), lambda b, h, j: (b * n + j, col + h))

    def bw(col):
        return pl.BlockSpec((C, HEAD_W), lambda b, h, j: (b * n + n - 1 - j, col + h))

    out_f = pl.BlockSpec((C, HEAD_W), lambda b, h, j: (b * n + j, h))
    out_b = pl.BlockSpec((C, HEAD_W), lambda b, h, j: (b * n + n - 1 - j, h))
    return pl.pallas_call(
        _hg_kernel,
        out_shape=(jax.ShapeDtypeStruct((T, HG_WIDTH), F32),
                   jax.ShapeDtypeStruct((T, HG_WIDTH), F32)),
        grid=(B, nh, n),
        in_specs=[fw(qc), fw(ffc), fw(ic), bw(qc), bw(fbc), bw(ic),
                  pl.BlockSpec((1, 2, HEAD_W), lambda b, h, j: (h, 0, 0))],
        out_specs=(out_f, out_b),
        scratch_shapes=[pltpu.VMEM((HEAD_W, HEAD_W), F32),
                        pltpu.VMEM((HEAD_W, HEAD_W), F32)],
        compiler_params=_cparams(("parallel", "parallel", "arbitrary")),
        name="hgrn2_scan",
    )(proj, proj, proj, proj, proj, proj, lbs)


def _mixout_kernel(x_ref, oa_ref, of_ref, ob_ref, gate_ref, mod_ref, hgg_ref, wo_ref,
                   n2_ref, wr_ref, x1_ref, h2_ref, lg_ref):
    mod = mod_ref[0]
    o = of_ref[...] + ob_ref[...]
    gate = _silu(gate_ref[...].astype(F32))
    hg = jnp.concatenate(
        [_rms(o[:, h * HEAD_W:(h + 1) * HEAD_W]) * hgg_ref[...] for h in range(HG_HEADS)],
        axis=-1) * gate
    mix = (jnp.dot(oa_ref[...], wo_ref[:ATT_WIDTH, :], preferred_element_type=F32)
           + jnp.dot(hg.astype(BF16), wo_ref[ATT_WIDTH:, :], preferred_element_type=F32))
    x1 = x_ref[...] + mod[2:3] * mix
    x1_ref[...] = x1
    h2 = _rms(x1) * n2_ref[...] * (1.0 + mod[4:5]) + mod[3:4]
    h2_ref[...] = h2
    logits = jnp.dot(h2, wr_ref[...], preferred_element_type=F32,
                     precision=lax.Precision.HIGHEST)
    lg_ref[...] = logits.T


def _mixout(xt, o_att, o_fw, o_bw, proj, mod, hg_g, w_out, n2g, w_router, S, tm):
    T, D = xt.shape
    gate_col = proj.shape[1] // HG_WIDTH - 1
    row = lambda i: (i, 0)
    const = lambda i: (0, 0)
    return pl.pallas_call(
        _mixout_kernel,
        out_shape=(jax.ShapeDtypeStruct((T, D), F32),
                   jax.ShapeDtypeStruct((T, D), F32),
                   jax.ShapeDtypeStruct((N_EXPERTS, T), F32)),
        grid=(T // tm,),
        in_specs=[pl.BlockSpec((tm, D), row),
                  pl.BlockSpec((tm, ATT_WIDTH), row),
                  pl.BlockSpec((tm, HG_WIDTH), row),
                  pl.BlockSpec((tm, HG_WIDTH), row),
                  pl.BlockSpec((tm, HG_WIDTH), lambda i: (i, gate_col)),
                  pl.BlockSpec((1, 6, D), lambda i: ((i * tm) // S, 0, 0)),
                  pl.BlockSpec((1, HEAD_W), const),
                  pl.BlockSpec(w_out.shape, const),
                  pl.BlockSpec((1, D), const),
                  pl.BlockSpec(w_router.shape, const)],
        out_specs=(pl.BlockSpec((tm, D), row),
                   pl.BlockSpec((tm, D), row),
                   pl.BlockSpec((N_EXPERTS, tm), lambda i: (0, i))),
        compiler_params=_cparams(("parallel",)),
        name="mixout_norm2_router",
    )(xt, o_att, o_fw, o_bw, proj, mod, hg_g, w_out, n2g, w_router)


def _first_argmax(x, iota, size):
    mx = jnp.max(x, axis=0, keepdims=True)
    idx = jnp.min(jnp.where(x == mx, iota, size), axis=0, keepdims=True)
    return mx, idx


def _route_kernel(lg_ref, br_ref, tri_ref, e_ref, w_ref, r_ref, cnt_ref, carry_ref):
    i = pl.program_id(0)

    @pl.when(i == 0)
    def _init():
        carry_ref[...] = jnp.zeros(carry_ref.shape, F32)

    scores = jax.nn.sigmoid(lg_ref[...])
    biased = scores + br_ref[...]
    tm = scores.shape[1]
    giota = lax.broadcasted_iota(I32, (GROUP_SIZE, tm), 0)
    gs = []
    for g in range(N_GROUPS):
        blk = biased[g * GROUP_SIZE:(g + 1) * GROUP_SIZE]
        m1, i1 = _first_argmax(blk, giota, GROUP_SIZE)
        m2 = jnp.max(jnp.where(giota == i1, NEG_INF, blk), axis=0, keepdims=True)
        gs.append(m1 + m2)
    gsc = jnp.concatenate(gs, axis=0)
    gi = lax.broadcasted_iota(I32, (N_GROUPS, tm), 0)
    gsel = jnp.zeros((N_GROUPS, tm), jnp.bool_)
    for _ in range(TOPK_GROUPS):
        _, idx = _first_argmax(gsc, gi, N_GROUPS)
        hit = gi == idx
        gsel = jnp.logical_or(gsel, hit)
        gsc = jnp.where(hit, NEG_INF, gsc)
    masked = jnp.concatenate(
        [jnp.where(gsel[g:g + 1], biased[g * GROUP_SIZE:(g + 1) * GROUP_SIZE], NEG_INF)
         for g in range(N_GROUPS)], axis=0)
    ei = lax.broadcasted_iota(I32, (N_EXPERTS, tm), 0)
    eidx, wts = [], []
    onehot = jnp.zeros((N_EXPERTS, tm), F32)
    for _ in range(TOP_K):
        _, idx = _first_argmax(masked, ei, N_EXPERTS)
        hit = ei == idx
        eidx.append(idx)
        wts.append(jnp.sum(jnp.where(hit, scores, 0.0), axis=0, keepdims=True))
        onehot = jnp.where(hit, 1.0, onehot)
        masked = jnp.where(hit, NEG_INF, masked)
    w = jnp.concatenate(wts, axis=0)
    w = w / (jnp.sum(w, axis=0, keepdims=True) + 1e-20) * ROUTE_SCALE
    e_ref[...] = jnp.concatenate(eidx, axis=0)
    w_ref[...] = w
    before = jnp.dot(onehot.astype(BF16), tri_ref[...], preferred_element_type=F32)
    before = before + carry_ref[...]
    ranks = [jnp.sum(jnp.where(ei == idx, before, 0.0), axis=0, keepdims=True) for idx in eidx]
    r_ref[...] = jnp.concatenate(ranks, axis=0).astype(I32)
    carry = carry_ref[...] + jnp.sum(onehot, axis=1, keepdims=True)
    carry_ref[...] = carry
    cnt_ref[...] = carry.astype(I32)


def _route(logits_t, b_router, tm):
    E, T = logits_t.shape
    tri = (jnp.arange(tm)[:, None] < jnp.arange(tm)[None, :]).astype(BF16)
    tok = lambda i: (0, i)
    const = lambda i: (0, 0)
    return pl.pallas_call(
        _route_kernel,
        out_shape=(jax.ShapeDtypeStruct((TOP_K, T), I32),
                   jax.ShapeDtypeStruct((TOP_K, T), F32),
                   jax.ShapeDtypeStruct((TOP_K, T), I32),
                   jax.ShapeDtypeStruct((E, 1), I32)),
        grid=(T // tm,),
        in_specs=[pl.BlockSpec((E, tm), tok),
                  pl.BlockSpec((E, 1), const),
                  pl.BlockSpec((tm, tm), const)],
        out_specs=(pl.BlockSpec((TOP_K, tm), tok),
                   pl.BlockSpec((TOP_K, tm), tok),
                   pl.BlockSpec((TOP_K, tm), tok),
                   pl.BlockSpec((E, 1), const)),
        scratch_shapes=[pltpu.VMEM((E, 1), F32)],
        compiler_params=_cparams(("arbitrary",)),
        name="route_topk",
    )(logits_t, b_router, tri)


def _pos_kernel(e_ref, r_ref, ps_ref, o_ref):
    e = e_ref[...]
    tm = e.shape[1]
    ei = lax.broadcasted_iota(I32, (N_EXPERTS, tm), 0)
    ps = ps_ref[...]
    rows = [jnp.sum(jnp.where(ei == e[k:k + 1], ps, 0), axis=0, keepdims=True)
            for k in range(TOP_K)]
    o_ref[...] = jnp.concatenate(rows, axis=0) + r_ref[...]


def _positions(eidx, rank, pstart, tm):
    K, T = eidx.shape
    tok = lambda i: (0, i)
    return pl.pallas_call(
        _pos_kernel,
        out_shape=jax.ShapeDtypeStruct((K, T), I32),
        grid=(T // tm,),
        in_specs=[pl.BlockSpec((K, tm), tok), pl.BlockSpec((K, tm), tok),
                  pl.BlockSpec((N_EXPERTS, 1), lambda i: (0, 0))],
        out_specs=pl.BlockSpec((K, tm), tok),
        compiler_params=_cparams(("parallel",)),
        name="dispatch_positions",
    )(eidx, rank, pstart)


def _sc_mesh():
    return plsc.VectorSubcoreMesh(core_axis_name="c", subcore_axis_name="s")


def _worker_id():
    return lax.axis_index("s") * SC_CORES + lax.axis_index("c")


def _window_positions(pos):
    K, T = pos.shape
    nwin = T // (SC_WORKERS * SC_WINDOW)
    assert nwin * SC_WORKERS * SC_WINDOW == T, T
    return pos.reshape(K, SC_WORKERS, nwin, SC_WINDOW).transpose(1, 2, 0, 3)


def _sc_dispatch(h, pos, P):
    T, D = h.shape
    pos4 = _window_positions(pos)
    NW, nwin, K, W = pos4.shape

    @functools.partial(
        pl.kernel, mesh=_sc_mesh(),
        out_type=jax.ShapeDtypeStruct((P, D), h.dtype),
        scratch_types=[pltpu.VMEM((K, W), I32), pltpu.VMEM((W, D), h.dtype),
                       pltpu.SemaphoreType.DMA],
        name="sc_dispatch")
    def k(h_hbm, pos_hbm, xs_hbm, idx_v, rows_v, sem):
        wid = _worker_id()

        @pl.loop(0, nwin)
        def _(j):
            base = (wid * nwin + j) * W
            pltpu.sync_copy(pos_hbm.at[wid, j], idx_v)
            pltpu.sync_copy(h_hbm.at[pl.ds(base, W)], rows_v)
            copies = [pltpu.async_copy(rows_v, xs_hbm.at[idx_v.at[kk]], sem)
                      for kk in range(K)]
            for c in copies:
                c.wait()

    return k(h, pos4)


def _sc_gather(ys, pos):
    P, D = ys.shape
    T = pos.shape[1]
    pos4 = _window_positions(pos)
    NW, nwin, K, W = pos4.shape

    @functools.partial(
        pl.kernel, mesh=_sc_mesh(),
        out_type=jax.ShapeDtypeStruct((K, T, D), ys.dtype),
        scratch_types=[pltpu.VMEM((K, W), I32), pltpu.VMEM((W, D), ys.dtype),
                       pltpu.SemaphoreType.DMA],
        name="sc_gather")
    def k(ys_hbm, pos_hbm, yg_hbm, idx_v, rows_v, sem):
        wid = _worker_id()

        @pl.loop(0, nwin)
        def _(j):
            base = (wid * nwin + j) * W
            pltpu.sync_copy(pos_hbm.at[wid, j], idx_v)
            for kk in range(K):
                pltpu.async_copy(ys_hbm.at[idx_v.at[kk]], rows_v, sem).wait()
                pltpu.sync_copy(rows_v, yg_hbm.at[kk, pl.ds(base, W)])

    return k(ys, pos4)


def _expert_kernel(be_ref, nv_ref, x_ref, wg_ref, wu_ref, wd_ref, o_ref):
    i = pl.program_id(0)

    @pl.when(i < nv_ref[0])
    def _():
        x = x_ref[...].astype(BF16)
        a = (_silu(jnp.dot(x, wg_ref[0], preferred_element_type=F32))
             * jnp.dot(x, wu_ref[0], preferred_element_type=F32))
        o_ref[...] = jnp.dot(a.astype(BF16), wd_ref[0], preferred_element_type=F32)


def _experts(xs, blk_e, nvalid, wg, wu, wd):
    P, D = xs.shape
    bm = EXPERT_BLOCK
    nb = P // bm
    rowmap = lambda i, be, nv: (jnp.minimum(i, nv[0] - 1), 0)
    wmap = lambda i, be, nv: (be[i], 0, 0)
    grid_spec = pltpu.PrefetchScalarGridSpec(
        num_scalar_prefetch=2,
        grid=(nb,),
        in_specs=[pl.BlockSpec((bm, D), rowmap),
                  pl.BlockSpec((1, D, EXPERT_DIM), wmap),
                  pl.BlockSpec((1, D, EXPERT_DIM), wmap),
                  pl.BlockSpec((1, EXPERT_DIM, D), wmap)],
        out_specs=pl.BlockSpec((bm, D), rowmap),
    )
    return pl.pallas_call(
        _expert_kernel,
        out_shape=jax.ShapeDtypeStruct((P, D), F32),
        grid_spec=grid_spec,
        compiler_params=_cparams(("arbitrary",)),
        name="expert_ffn",
    )(blk_e, nvalid, xs, wg, wu, wd)


def _final_kernel(yg_ref, w_ref, h2_ref, x1_ref, mod_ref, modf_ref, nf_ref,
                  sg_ref, su_ref, sd_ref, o_ref):
    w = w_ref[...]
    y = yg_ref[0] * w[:, 0:1]
    for k in range(1, TOP_K):
        y = y + yg_ref[k] * w[:, k:k + 1]
    hb = h2_ref[...].astype(BF16)
    a = (_silu(jnp.dot(hb, sg_ref[...], preferred_element_type=F32))
         * jnp.dot(hb, su_ref[...], preferred_element_type=F32))
    shared = jnp.dot(a.astype(BF16), sd_ref[...], preferred_element_type=F32)
    mod = mod_ref[0]
    modf = modf_ref[0]
    x2 = x1_ref[...] + mod[5:6] * (y + shared)
    o_ref[...] = _rms(x2) * nf_ref[...] * (1.0 + modf[1:2]) + modf[0:1]


def _final(yg, wts_t, h2, x1, mod, modf, nfg, sg, su, sd, S, tm):
    T, D = x1.shape
    row = lambda i: (i, 0)
    const = lambda i: (0, 0)
    bat = lambda i: ((i * tm) // S, 0, 0)
    return pl.pallas_call(
        _final_kernel,
        out_shape=jax.ShapeDtypeStruct((T, D), F32),
        grid=(T // tm,),
        in_specs=[pl.BlockSpec((TOP_K, tm, D), lambda i: (0, i, 0)),
                  pl.BlockSpec((tm, TOP_K), row),
                  pl.BlockSpec((tm, D), row),
                  pl.BlockSpec((tm, D), row),
                  pl.BlockSpec((1, 6, D), bat),
                  pl.BlockSpec((1, 2, D), bat),
                  pl.BlockSpec((1, D), const),
                  pl.BlockSpec(sg.shape, const),
                  pl.BlockSpec(su.shape, const),
                  pl.BlockSpec(sd.shape, const)],
        out_specs=pl.BlockSpec((tm, D), row),
        compiler_params=_cparams(("parallel",)),
        name="combine_shared_final",
    )(yg, wts_t, h2, x1, mod, modf, nfg, sg, su, sd)


def _tile(n, pref):
    t = min(n, pref)
    assert n % t == 0, (n, pref)
    return t


def _moe(h2, logits_t, x1, mod, modf, p, S):
    T, D = h2.shape
    tm_r = _tile(T, 512)
    eidx, wts, rank, counts = _route(logits_t, p["b_router"], tm_r)
    bm = EXPERT_BLOCK
    counts = counts[:, 0]
    padded = (counts + bm - 1) // bm * bm
    pends = jnp.cumsum(padded)
    pstart = (pends - padded).astype(I32)
    P = T * TOP_K + N_EXPERTS * bm
    nb = P // bm
    pos = _positions(eidx, rank, pstart[:, None], tm_r)
    blk_e = jnp.minimum(jnp.searchsorted(pends, jnp.arange(nb) * bm, side="right"),
                        N_EXPERTS - 1).astype(I32)
    nvalid = (pends[-1] // bm).astype(I32)
    blk_e = jnp.where(jnp.arange(nb) < nvalid, blk_e, blk_e[jnp.maximum(nvalid - 1, 0)])
    xs = _sc_dispatch(h2, pos, P)
    ys = _experts(xs, blk_e, nvalid[None], p["wg"], p["wu"], p["wd"])
    yg = _sc_gather(ys, pos)
    tm_f = _tile(T, 128)
    return _final(yg, wts.T, h2, x1, mod, modf, p["normf_g"], p["sg"], p["su"], p["sd"], S, tm_f)


def _trunk(x, mod, modf, p):
    B, S, D = x.shape
    T = B * S
    xt = x.reshape(T, D)
    tm = _tile(S, 256)
    proj = _proj(xt, mod, p["norm1_g"], p["w_in"], S, tm)
    tq = _tile(S, 256)
    tk = _tile(S, 512)
    o_att = _attention(proj, p["slopes"], p["lam_p"], p["subln_g"], B, S, tq, tk)
    o_fw, o_bw = _hgrn2(proj, p["lbs"], B, S)
    x1, h2, logits_t = _mixout(xt, o_att, o_fw, o_bw, proj, mod, p["hg_norm_g"], p["w_out"],
                               p["norm2_g"], p["w_router"], S, tm)
    out = _moe(h2, logits_t, x1, mod, modf, p, S)
    return out.reshape(B, S, D)


def kernel(x_prompt, x_sample, c_prompt, c_sample, w_ada, b_ada, norm1_g, w_in, lam_q1, lam_k1, lam_q2, lam_k2, subln_g, hg_lb_logits, hg_norm_g, w_out, norm2_g, w_router, b_router, w_exp_gate, w_exp_up, w_exp_down, w_sh_gate, w_sh_up, w_sh_down, w_ada_f, b_ada_f, normf_g):
    D = x_prompt.shape[-1]
    Bp, Bs = c_prompt.shape[0], c_sample.shape[0]
    c_all = jnp.concatenate([c_prompt, c_sample], axis=0)
    R = -(-c_all.shape[0] // 8) * 8
    c_all = jnp.pad(c_all, ((0, R - c_all.shape[0]), (0, 0)))
    w_all = jnp.concatenate([w_ada[0], w_ada_f], axis=1)
    b_all = jnp.concatenate([b_ada[0], b_ada_f], axis=0)[None]
    mod_all = _ada(c_all, w_all, b_all)
    mod6 = mod_all[:, :6 * D].reshape(R, 6, D)
    mod2 = mod_all[:, 6 * D:].reshape(R, 2, D)

    lbs = jax.nn.softmax(hg_lb_logits.astype(F32), axis=0)[0]
    lbs = lbs.reshape(2, HG_HEADS, HEAD_W).transpose(1, 0, 2)
    slopes = (jnp.exp2(-8.0 * (jnp.arange(ATT_HEADS, dtype=F32) + 1.0) / ATT_HEADS) * LOG2E)
    p = dict(
        norm1_g=norm1_g[0][None], w_in=w_in[0].astype(BF16),
        slopes=slopes.astype(F32),
        lam_p=jnp.stack([lam_q1[0], lam_k1[0], lam_q2[0], lam_k2[0]]).astype(F32),
        subln_g=subln_g[0][:, None], lbs=lbs, hg_norm_g=hg_norm_g[0][None],
        w_out=w_out[0].astype(BF16), norm2_g=norm2_g[0][None],
        w_router=w_router[0], b_router=b_router[0][:, None],
        wg=w_exp_gate[0].astype(BF16), wu=w_exp_up[0].astype(BF16),
        wd=w_exp_down[0].astype(BF16),
        sg=w_sh_gate[0].astype(BF16), su=w_sh_up[0].astype(BF16), sd=w_sh_down[0].astype(BF16),
        normf_g=normf_g[None],
    )
    y_prompt = _trunk(x_prompt, mod6[:Bp], mod2[:Bp], p)
    y_sample = _trunk(x_sample, mod6[Bp:Bp + Bs], mod2[Bp:Bp + Bs], p)
    return (y_prompt, y_sample)
```

```python
import functools
import math

import jax
import jax.numpy as jnp
from jax import lax
from jax.experimental import pallas as pl
from jax.experimental.pallas import tpu as pltpu
from jax.experimental.pallas import tpu_sc as plsc

F32 = jnp.float32
BF16 = jnp.bfloat16
I32 = jnp.int32

EPS = 1e-6
LOG2E = 1.4426950408889634

ATT_HEADS = 4
ATT_QK_DIM = 64
HEAD_W = 128
ATT_WIDTH = ATT_HEADS * HEAD_W
HG_HEADS = 4
HG_WIDTH = HG_HEADS * HEAD_W
N_EXPERTS = 256
TOP_K = 8
N_GROUPS = 8
TOPK_GROUPS = 4
GROUP_SIZE = N_EXPERTS // N_GROUPS
EXPERT_DIM = 256
ROUTE_SCALE = 2.5
LAM_INIT = 0.8 - 0.6 * math.exp(-0.3 * 0)

HG_CHUNK = 128
HG_SUB = 16
HG_MILD_LOG_GATE = 3.75
EXPERT_BLOCK = 256
SC_CORES = 2
SC_SUBCORES = 16
SC_WORKERS = SC_CORES * SC_SUBCORES
SC_WINDOW = 32
VMEM_LIMIT = 48 * 1024 * 1024

NEG_INF = float("-inf")


def _cparams(sem):
    return pltpu.CompilerParams(dimension_semantics=sem, vmem_limit_bytes=VMEM_LIMIT)


def _silu(x):
    return x * jax.nn.sigmoid(x)


def _pack_rows(x):
    half = x.shape[1] // 2
    bits = lax.bitcast_convert_type(x.astype(BF16).astype(F32), jnp.uint32)
    return lax.bitcast_convert_type((bits[:, :half] >> 16) | bits[:, half:], I32)


def _unpack_rows(w):
    u = lax.bitcast_convert_type(w, jnp.uint32)
    lo = lax.bitcast_convert_type(u << 16, F32)
    hi = lax.bitcast_convert_type(u & jnp.uint32(0xFFFF0000), F32)
    return lo, hi


def _ada_kernel(c_ref, w_ref, b_ref, o_ref):
    cs = _silu(c_ref[...])
    o_ref[...] = jnp.dot(cs, w_ref[...], preferred_element_type=F32,
                         precision=lax.Precision.HIGHEST) + b_ref[...]


def _ada(c, w, b):
    R, D = c.shape
    N = w.shape[1]
    tn = 1024
    return pl.pallas_call(
        _ada_kernel,
        out_shape=jax.ShapeDtypeStruct((R, N), F32),
        grid=(N // tn,),
        in_specs=[pl.BlockSpec((R, D), lambda j: (0, 0)),
                  pl.BlockSpec((D, tn), lambda j: (0, j)),
                  pl.BlockSpec((1, tn), lambda j: (0, j))],
        out_specs=pl.BlockSpec((R, tn), lambda j: (0, j)),
        compiler_params=_cparams(("arbitrary",)),
        name="ada_mod",
    )(c, w, b)


def _rms(x):
    return x * lax.rsqrt(jnp.mean(x * x, axis=-1, keepdims=True) + EPS)


def _proj_kernel(x_ref, mod_ref, g_ref, w_ref, posk_ref, grp_ref, o_ref, ka_ref, nrm_ref, *, qscale):
    x = x_ref[...]
    mod = mod_ref[0]
    h = _rms(x) * g_ref[...] * (1.0 + mod[1:2]) + mod[0:1]
    acc = jnp.dot(h.astype(BF16), w_ref[...], preferred_element_type=F32)
    qw = ATT_WIDTH
    qb = (acc[:, :qw] * qscale).astype(BF16)
    kb = acc[:, qw:2 * qw].astype(BF16)
    o_ref[:, :qw] = qb
    o_ref[:, qw:2 * qw] = kb
    o_ref[:, 2 * qw:] = acc[:, 2 * qw:].astype(BF16)
    for hh in range(ATT_HEADS):
        ka_ref[:, 2 * hh * HEAD_W:(2 * hh + 1) * HEAD_W] = kb[:, hh * HEAD_W:(hh + 1) * HEAD_W]
        ka_ref[:, (2 * hh + 1) * HEAD_W:(2 * hh + 2) * HEAD_W] = posk_ref[:, hh * HEAD_W:(hh + 1) * HEAD_W]
    for idx, t in enumerate((qb, kb)):
        tf = t.astype(F32)
        n2 = jnp.dot((tf * tf).astype(BF16), grp_ref[...], preferred_element_type=F32)
        nrm_ref[0, idx] = jnp.broadcast_to(jnp.max(n2, axis=0, keepdims=True), (8, HEAD_W))


def _proj(xt, mod, g, w, posk, S, tm):
    T, D = xt.shape
    N = w.shape[1]
    qscale = ATT_QK_DIM ** -0.5 * LOG2E
    nblk = T // tm
    spb = S // tm
    grp = (jnp.arange(ATT_WIDTH)[:, None] // ATT_QK_DIM == jnp.arange(HEAD_W)[None, :]).astype(BF16)
    return pl.pallas_call(
        functools.partial(_proj_kernel, qscale=qscale),
        out_shape=(jax.ShapeDtypeStruct((T, N), BF16),
                   jax.ShapeDtypeStruct((T, 2 * ATT_WIDTH), BF16),
                   jax.ShapeDtypeStruct((nblk, 2, 8, HEAD_W), F32)),
        grid=(nblk,),
        in_specs=[pl.BlockSpec((tm, D), lambda i: (i, 0)),
                  pl.BlockSpec((1, 6, D), lambda i: ((i * tm) // S, 0, 0)),
                  pl.BlockSpec((1, D), lambda i: (0, 0)),
                  pl.BlockSpec((D, N), lambda i: (0, 0)),
                  pl.BlockSpec((tm, ATT_WIDTH), lambda i: (i % spb, 0)),
                  pl.BlockSpec((ATT_WIDTH, HEAD_W), lambda i: (0, 0))],
        out_specs=(pl.BlockSpec((tm, N), lambda i: (i, 0)),
                   pl.BlockSpec((tm, 2 * ATT_WIDTH), lambda i: (i, 0)),
                   pl.BlockSpec((1, 2, 8, HEAD_W), lambda i: (i, 0, 0, 0))),
        compiler_params=_cparams(("parallel",)),
        name="norm1_proj",
    )(xt, mod, g, w, posk, grp)


def _split3(x):
    hi = x.astype(BF16)
    r = x - hi.astype(F32)
    mid = r.astype(BF16)
    lo = (r - mid.astype(F32)).astype(BF16)
    return hi, mid, lo


def _alibi_tables(slopes2, S):
    H = slopes2.shape[0]
    a = slopes2[:, None] * jnp.arange(S, dtype=F32)[None, :]
    ah, am, al = _split3(a)
    one = jnp.ones((H, S), BF16)
    zero = jnp.zeros((H, S), BF16)
    pad = jnp.zeros((H, S, HEAD_W - 9), BF16)
    posk = jnp.concatenate([jnp.stack([one] * 6 + [ah, am, al], axis=-1), pad], axis=-1)
    posq = jnp.concatenate([jnp.stack([zero] * 3 + [-ah, -am, -al] + [one] * 3, axis=-1), pad], axis=-1)
    return posq, posk.transpose(1, 0, 2).reshape(S, H * HEAD_W)


def _attn_kernel(slope_ref, lam_ref, q_ref, k_ref, v_ref, g_ref, o_ref,
                 qm_ref, m_ref, l_ref, acc_ref, *, tq, tk, nk):
    h = pl.program_id(1)
    qi = pl.program_id(2)
    ki = pl.program_id(3)

    @pl.when(ki == 0)
    def _init():
        q = q_ref[...]
        lane = lax.broadcasted_iota(I32, q.shape, 1)
        zero = jnp.zeros_like(q)
        qm_ref[0] = jnp.where(lane < ATT_QK_DIM, q, zero)
        qm_ref[1] = jnp.where(lane >= ATT_QK_DIM, q, zero)
        m_ref[...] = jnp.full(m_ref.shape, NEG_INF, F32)
        l_ref[...] = jnp.zeros(l_ref.shape, F32)
        acc_ref[...] = jnp.zeros(acc_ref.shape, F32)

    slope = slope_ref[h]
    k = k_ref[...]
    v = v_ref[...]
    row = lax.broadcasted_iota(I32, (tk, tq), 0)
    col = lax.broadcasted_iota(I32, (tk, tq), 1)
    dist = (col - row + (qi * tq - ki * tk)).astype(F32)
    bias = jnp.abs(dist) * (-slope)
    for m in range(2):
        s = lax.dot_general(k, qm_ref[m], (((1,), (1,)), ((), ())),
                            preferred_element_type=F32) + bias
        m_old = m_ref[m]
        m_new = jnp.maximum(m_old, jnp.max(s, axis=0, keepdims=True))
        alpha = jnp.exp2(m_old - m_new)
        p = jnp.exp2(s - m_new)
        l_ref[m] = alpha * l_ref[m] + jnp.sum(p, axis=0, keepdims=True)
        pv = lax.dot_general(v, p.astype(BF16), (((0,), (0,)), ((), ())),
                             preferred_element_type=F32)
        acc_ref[m] = alpha * acc_ref[m] + pv
        m_ref[m] = m_new

    @pl.when(ki == nk - 1)
    def _fin():
        lp = lam_ref[...]
        lam = (jnp.exp(jnp.sum(lp[0:1] * lp[1:2], axis=-1, keepdims=True))
               - jnp.exp(jnp.sum(lp[2:3] * lp[3:4], axis=-1, keepdims=True))
               + LAM_INIT)
        o = acc_ref[0] / l_ref[0] - lam * (acc_ref[1] / l_ref[1])
        ms = jnp.mean(o * o, axis=0, keepdims=True)
        y = o * lax.rsqrt(ms + EPS) * g_ref[...] * (1.0 - LAM_INIT)
        o_ref[...] = y.T.astype(o_ref.dtype)


def _attention(proj, slopes, lam_p, subln_g, B, S, tq, tk):
    T = B * S
    nq, nk = S // tq, S // tk
    kcol = ATT_WIDTH // HEAD_W
    vcol = 2 * ATT_WIDTH // HEAD_W
    grid_spec = pltpu.PrefetchScalarGridSpec(
        num_scalar_prefetch=1,
        grid=(B, ATT_HEADS, nq, nk),
        in_specs=[
            pl.BlockSpec((4, ATT_QK_DIM), lambda b, h, qi, ki, s: (0, 0)),
            pl.BlockSpec((tq, HEAD_W), lambda b, h, qi, ki, s: (b * nq + qi, h)),
            pl.BlockSpec((tk, HEAD_W), lambda b, h, qi, ki, s: (b * nk + ki, kcol + h)),
            pl.BlockSpec((tk, HEAD_W), lambda b, h, qi, ki, s: (b * nk + ki, vcol + h)),
            pl.BlockSpec((HEAD_W, 1), lambda b, h, qi, ki, s: (0, 0)),
        ],
        out_specs=pl.BlockSpec((tq, HEAD_W), lambda b, h, qi, ki, s: (b * nq + qi, h)),
        scratch_shapes=[pltpu.VMEM((2, tq, HEAD_W), BF16),
                        pltpu.VMEM((2, 1, tq), F32),
                        pltpu.VMEM((2, 1, tq), F32),
                        pltpu.VMEM((2, HEAD_W, tq), F32)],
    )
    return pl.pallas_call(
        functools.partial(_attn_kernel, tq=tq, tk=tk, nk=nk),
        out_shape=jax.ShapeDtypeStruct((T, ATT_WIDTH), BF16),
        grid_spec=grid_spec,
        compiler_params=_cparams(("parallel", "parallel", "parallel", "arbitrary")),
        name="diff_attention",
    )(slopes, lam_p, proj, proj, proj, subln_g)


def _attn_finish(lam_ref, g_ref, o_ref, l_ref, acc_ref):
    lp = lam_ref[...]
    lam = (jnp.exp(jnp.sum(lp[0:1] * lp[1:2], axis=-1, keepdims=True))
           - jnp.exp(jnp.sum(lp[2:3] * lp[3:4], axis=-1, keepdims=True))
           + LAM_INIT)
    o = acc_ref[0] / l_ref[0] - lam * (acc_ref[1] / l_ref[1])
    ms = jnp.mean(o * o, axis=0, keepdims=True)
    y = o * lax.rsqrt(ms + EPS) * g_ref[...] * (1.0 - LAM_INIT)
    o_ref[...] = y.T.astype(o_ref.dtype)


def _attn_ref_kernel(sc_ref, lam_ref, q_ref, ka_ref, v_ref, g_ref, posq_ref, o_ref,
                     qa_ref, l_ref, acc_ref, *, tq, tk, nk):
    h = pl.program_id(1)
    qi = pl.program_id(2)
    ki = pl.program_id(3)

    @pl.when(ki == 0)
    def _init():
        q = q_ref[...]
        lane = lax.broadcasted_iota(I32, q.shape, 1)
        zero = jnp.zeros_like(q)
        kmax = sc_ref[ATT_HEADS]
        pa = posq_ref[0]
        for m in range(2):
            sel = (lane < ATT_QK_DIM) if m == 0 else (lane >= ATT_QK_DIM)
            qm = jnp.where(sel, q, zero)
            qf = qm.astype(F32)
            ub = jnp.sqrt(jnp.sum(qf * qf, axis=1, keepdims=True)) * kmax
            uh, um, ul = (t.astype(F32) for t in _split3(jnp.broadcast_to(ub, q.shape)))
            ubp = jnp.where(lane == 0, -uh, jnp.where(lane == 1, -um, jnp.where(lane == 2, -ul, 0.0)))
            ubp = ubp.astype(BF16)
            for var, pp in enumerate((pa + ubp, ubp - pa, ubp)):
                qa_ref[2 * var + m] = jnp.concatenate([qm, pp], axis=1)
        l_ref[...] = jnp.zeros(l_ref.shape, F32)
        acc_ref[...] = jnp.zeros(acc_ref.shape, F32)

    rel = qi * tq - ki * tk
    near = jnp.logical_and(rel > -tq, rel < tk)

    def body(with_bias):
        ka = ka_ref[...]
        if with_bias:
            row = lax.broadcasted_iota(I32, (tk, tq), 0)
            col = lax.broadcasted_iota(I32, (tk, tq), 1)
            bias = jnp.abs((col - row + rel).astype(F32)) * sc_ref[h]
            var = 2
        else:
            var = jnp.where(rel > 0, 0, 1)
        ss = [lax.dot_general(ka, qa_ref[2 * var + m], (((1,), (1,)), ((), ())),
                              preferred_element_type=F32) for m in range(2)]
        v = v_ref[...]
        for m in range(2):
            p = jnp.exp2(ss[m] + bias if with_bias else ss[m])
            l_ref[m] += jnp.sum(p, axis=0, keepdims=True)
            acc_ref[m] += lax.dot_general(v, p.astype(BF16), (((0,), (0,)), ((), ())),
                                          preferred_element_type=F32)

    @pl.when(near)
    def _():
        body(True)

    @pl.when(jnp.logical_not(near))
    def _():
        body(False)

    @pl.when(ki == nk - 1)
    def _fin():
        _attn_finish(lam_ref, g_ref, o_ref, l_ref, acc_ref)


def _attention_ref(proj, kaug, scal, lam_p, subln_g, posq, B, S, tq, tk):
    T = B * S
    nq, nk = S // tq, S // tk
    vcol = 2 * ATT_WIDTH // HEAD_W
    grid_spec = pltpu.PrefetchScalarGridSpec(
        num_scalar_prefetch=1,
        grid=(B, ATT_HEADS, nq, nk),
        in_specs=[
            pl.BlockSpec((4, ATT_QK_DIM), lambda b, h, qi, ki, s: (0, 0)),
            pl.BlockSpec((tq, HEAD_W), lambda b, h, qi, ki, s: (b * nq + qi, h)),
            pl.BlockSpec((tk, 2 * HEAD_W), lambda b, h, qi, ki, s: (b * nk + ki, h)),
            pl.BlockSpec((tk, HEAD_W), lambda b, h, qi, ki, s: (b * nk + ki, vcol + h)),
            pl.BlockSpec((HEAD_W, 1), lambda b, h, qi, ki, s: (0, 0)),
            pl.BlockSpec((1, tq, HEAD_W), lambda b, h, qi, ki, s: (h, qi, 0)),
        ],
        out_specs=pl.BlockSpec((tq, HEAD_W), lambda b, h, qi, ki, s: (b * nq + qi, h)),
        scratch_shapes=[pltpu.VMEM((6, tq, 2 * HEAD_W), BF16),
                        pltpu.VMEM((2, 1, tq), F32),
                        pltpu.VMEM((2, HEAD_W, tq), F32)],
    )
    return pl.pallas_call(
        functools.partial(_attn_ref_kernel, tq=tq, tk=tk, nk=nk),
        out_shape=jax.ShapeDtypeStruct((T, ATT_WIDTH), BF16),
        grid_spec=grid_spec,
        compiler_params=_cparams(("parallel", "parallel", "parallel", "arbitrary")),
        name="diff_attention_ref",
    )(scal, lam_p, proj, kaug, proj, subln_g, posq)


def _hg_chunk(q, kk, v, g, st, reverse):
    C = q.shape[0]
    nsub = C // HG_SUB
    row = lax.broadcasted_iota(I32, (C, HEAD_W), 0)
    b = _hg_cumsum(g, reverse)
    b_end = b[C - 1:C] if not reverse else b[0:1]

    qd = (q * jnp.exp(b)).astype(BF16)
    o_inter = lax.dot_general(qd, st.astype(BF16), (((1,), (1,)), ((), ())),
                              preferred_element_type=F32)

    ones = jnp.ones((HEAD_W, HEAD_W), BF16)
    srow = lax.broadcasted_iota(I32, (HG_SUB, HEAD_W), 0)
    vb = v.astype(BF16)
    outs = []
    for blk in range(nsub):
        lo, hi = blk * HG_SUB, (blk + 1) * HG_SUB
        qI, kI, vI, bI = q[lo:hi], kk[lo:hi], v[lo:hi], b[lo:hi]
        pieces = []
        for t in range(HG_SUB):
            dl = bI[t:t + 1] - bI
            keep = (srow <= t) if not reverse else (srow >= t)
            e = jnp.exp(jnp.where(keep, dl, NEG_INF))
            pieces.append(qI[t:t + 1] * kI * e)
        wst = jnp.concatenate(pieces, axis=0).astype(BF16)
        rsum = jnp.dot(wst, ones, preferred_element_type=F32)
        o_blk = jnp.sum(rsum.reshape(HG_SUB, HG_SUB, HEAD_W) * vI[None], axis=1)
        has_off = (blk > 0) if not reverse else (blk < nsub - 1)
        if has_off:
            if not reverse:
                r = b[lo - 1:lo]
                kmask = row < lo
            else:
                r = b[hi:hi + 1]
                kmask = row >= hi
            qs = (qI * jnp.exp(bI - r)).astype(BF16)
            ks = (kk * jnp.exp(jnp.where(kmask, r - b, NEG_INF))).astype(BF16)
            a = lax.dot_general(qs, ks, (((1,), (1,)), ((), ())),
                                preferred_element_type=F32)
            o_blk = o_blk + jnp.dot(a.astype(BF16), vb, preferred_element_type=F32)
        outs.append(o_blk)
    o = o_inter + jnp.concatenate(outs, axis=0)

    kd = (kk * jnp.exp(b_end - b)).astype(BF16)
    upd = lax.dot_general(vb, kd, (((0,), (0,)), ((), ())),
                          preferred_element_type=F32)
    st_new = st * jnp.exp(b_end) + upd
    return o, st_new


def _hg_cumsum(g, reverse):
    C = g.shape[0]
    row = lax.broadcasted_iota(I32, g.shape, 0)
    b = g
    d = 1
    while d < C:
        if not reverse:
            b = b + jnp.where(row >= d, pltpu.roll(b, d, axis=0), 0.0)
        else:
            b = b + jnp.where(row < C - d, pltpu.roll(b, C - d, axis=0), 0.0)
        d *= 2
    return b


def _hg_chunk_mild(q, kk, v, g, st, reverse):
    C = q.shape[0]
    nsub = C // HG_SUB
    row = lax.broadcasted_iota(I32, (C, HEAD_W), 0)
    b = _hg_cumsum(g, reverse)
    b_end = b[C - 1:C] if not reverse else b[0:1]
    qd = (q * jnp.exp(b)).astype(BF16)
    o_inter = lax.dot_general(qd, st.astype(BF16), (((1,), (1,)), ((), ())),
                              preferred_element_type=F32)
    vb = v.astype(BF16)
    qrow = lax.broadcasted_iota(I32, (HG_SUB, C), 0)
    kcol = lax.broadcasted_iota(I32, (HG_SUB, C), 1)
    zero_ref = jnp.zeros((1, HEAD_W), F32)
    a_rows = []
    for blk in range(nsub):
        lo, hi = blk * HG_SUB, (blk + 1) * HG_SUB
        if not reverse:
            r = b[lo - 1:lo] if blk > 0 else zero_ref
            kmask = row < hi
            amask = kcol <= qrow + lo
        else:
            r = b[hi:hi + 1] if blk < nsub - 1 else zero_ref
            kmask = row >= lo
            amask = kcol >= qrow + lo
        qs = (q[lo:hi] * jnp.exp(b[lo:hi] - r)).astype(BF16)
        ks = (kk * jnp.exp(jnp.where(kmask, r - b, NEG_INF))).astype(BF16)
        a = lax.dot_general(qs, ks, (((1,), (1,)), ((), ())), preferred_element_type=F32)
        a_rows.append(jnp.where(amask, a, 0.0))
    a_full = jnp.concatenate(a_rows, axis=0).astype(BF16)
    o = o_inter + jnp.dot(a_full, vb, preferred_element_type=F32)
    kd = (kk * jnp.exp(b_end - b)).astype(BF16)
    upd = lax.dot_general(vb, kd, (((0,), (0,)), ((), ())), preferred_element_type=F32)
    return o, st * jnp.exp(b_end) + upd


def _hg_kernel(qf_ref, ff_ref, if_ref, qb_ref, fb_ref, ib_ref, lb_ref,
               of_ref, ob_ref, sf_ref, sb_ref):
    j = pl.program_id(2)

    @pl.when(j == 0)
    def _init():
        sf_ref[...] = jnp.zeros(sf_ref.shape, F32)
        sb_ref[...] = jnp.zeros(sb_ref.shape, F32)

    lb = lb_ref[0]

    def prep(q_ref, f_ref, i_ref, lbd):
        q = _silu(q_ref[...].astype(F32))
        f = lbd + (1.0 - lbd) * jax.nn.sigmoid(f_ref[...].astype(F32))
        return q, 1.0 - f, i_ref[...].astype(F32), jnp.log(f)

    fwd = prep(qf_ref, ff_ref, if_ref, lb[0:1])
    bwd = prep(qb_ref, fb_ref, ib_ref, lb[1:2])
    mild = jnp.minimum(jnp.min(fwd[3]), jnp.min(bwd[3])) >= -HG_MILD_LOG_GATE

    def run(chunk_fn):
        o, st = chunk_fn(*fwd, sf_ref[...], reverse=False)
        of_ref[...] = o
        sf_ref[...] = st
        o, st = chunk_fn(*bwd, sb_ref[...], reverse=True)
        ob_ref[...] = o
        sb_ref[...] = st

    @pl.when(mild)
    def _():
        run(_hg_chunk_mild)

    @pl.when(jnp.logical_not(mild))
    def _():
        run(_hg_chunk)


def _hgrn2(proj, lbs, B, S):
    T = B * S
    C = HG_CHUNK
    n = S // C
    base = 3 * ATT_WIDTH // HEAD_W
    nh = HG_HEADS
    qc, ffc, fbc, ic = base, base + nh, base + 2 * nh, base + 3 * nh

    def fw(col):
        return pl.BlockSpec((C, HEAD_W), lambda b, h, j: (b * n + j, col + h))

    def bw(col):
        return pl.BlockSpec((C, HEAD_W), lambda b, h, j: (b * n + n - 1 - j, col + h))

    out_f = pl.BlockSpec((C, HEAD_W), lambda b, h, j: (b * n + j, h))
    out_b = pl.BlockSpec((C, HEAD_W), lambda b, h, j: (b * n + n - 1 - j, h))
    return pl.pallas_call(
        _hg_kernel,
        out_shape=(jax.ShapeDtypeStruct((T, HG_WIDTH), F32),
                   jax.ShapeDtypeStruct((T, HG_WIDTH), F32)),
        grid=(B, nh, n),
        in_specs=[fw(qc), fw(ffc), fw(ic), bw(qc), bw(fbc), bw(ic),
                  pl.BlockSpec((1, 2, HEAD_W), lambda b, h, j: (h, 0, 0))],
        out_specs=(out_f, out_b),
        scratch_shapes=[pltpu.VMEM((HEAD_W, HEAD_W), F32),
                        pltpu.VMEM((HEAD_W, HEAD_W), F32)],
        compiler_params=_cparams(("parallel", "parallel", "arbitrary")),
        name="hgrn2_scan",
    )(proj, proj, proj, proj, proj, proj, lbs)


def _mixout_kernel(x_ref, oa_ref, of_ref, ob_ref, gate_ref, mod_ref, hgg_ref, wo_ref,
                   n2_ref, wr_ref, x1_ref, h2_ref, lg_ref):
    mod = mod_ref[0]
    o = of_ref[...] + ob_ref[...]
    gate = _silu(gate_ref[...].astype(F32))
    hg = jnp.concatenate(
        [_rms(o[:, h * HEAD_W:(h + 1) * HEAD_W]) * hgg_ref[...] for h in range(HG_HEADS)],
        axis=-1) * gate
    mix = (jnp.dot(oa_ref[...], wo_ref[:ATT_WIDTH, :], preferred_element_type=F32)
           + jnp.dot(hg.astype(BF16), wo_ref[ATT_WIDTH:, :], preferred_element_type=F32))
    x1 = x_ref[...] + mod[2:3] * mix
    x1_ref[...] = x1
    h2 = _rms(x1) * n2_ref[...] * (1.0 + mod[4:5]) + mod[3:4]
    h2_ref[...] = _pack_rows(h2)
    logits = jnp.dot(h2, wr_ref[...], preferred_element_type=F32,
                     precision=lax.Precision.HIGHEST)
    lg_ref[...] = logits.T


def _mixout(xt, o_att, o_fw, o_bw, proj, mod, hg_g, w_out, n2g, w_router, S, tm):
    T, D = xt.shape
    gate_col = proj.shape[1] // HG_WIDTH - 1
    row = lambda i: (i, 0)
    const = lambda i: (0, 0)
    return pl.pallas_call(
        _mixout_kernel,
        out_shape=(jax.ShapeDtypeStruct((T, D), F32),
                   jax.ShapeDtypeStruct((T, D // 2), I32),
                   jax.ShapeDtypeStruct((N_EXPERTS, T), F32)),
        grid=(T // tm,),
        in_specs=[pl.BlockSpec((tm, D), row),
                  pl.BlockSpec((tm, ATT_WIDTH), row),
                  pl.BlockSpec((tm, HG_WIDTH), row),
                  pl.BlockSpec((tm, HG_WIDTH), row),
                  pl.BlockSpec((tm, HG_WIDTH), lambda i: (i, gate_col)),
                  pl.BlockSpec((1, 6, D), lambda i: ((i * tm) // S, 0, 0)),
                  pl.BlockSpec((1, HEAD_W), const),
                  pl.BlockSpec(w_out.shape, const),
                  pl.BlockSpec((1, D), const),
                  pl.BlockSpec(w_router.shape, const)],
        out_specs=(pl.BlockSpec((tm, D), row),
                   pl.BlockSpec((tm, D // 2), row),
                   pl.BlockSpec((N_EXPERTS, tm), lambda i: (0, i))),
        compiler_params=_cparams(("parallel",)),
        name="mixout_norm2_router",
    )(xt, o_att, o_fw, o_bw, proj, mod, hg_g, w_out, n2g, w_router)


def _first_argmax(x, iota, size):
    mx = jnp.max(x, axis=0, keepdims=True)
    idx = jnp.min(jnp.where(x == mx, iota, size), axis=0, keepdims=True)
    return mx, idx


def _route_kernel(lg_ref, br_ref, tri_ref, e_ref, w_ref, r_ref, cnt_ref, carry_ref):
    i = pl.program_id(0)

    @pl.when(i == 0)
    def _init():
        carry_ref[...] = jnp.zeros(carry_ref.shape, F32)

    scores = jax.nn.sigmoid(lg_ref[...])
    biased = scores + br_ref[...]
    tm = scores.shape[1]
    giota = lax.broadcasted_iota(I32, (GROUP_SIZE, tm), 0)
    gs = []
    for g in range(N_GROUPS):
        blk = biased[g * GROUP_SIZE:(g + 1) * GROUP_SIZE]
        m1, i1 = _first_argmax(blk, giota, GROUP_SIZE)
        m2 = jnp.max(jnp.where(giota == i1, NEG_INF, blk), axis=0, keepdims=True)
        gs.append(m1 + m2)
    gsc = jnp.concatenate(gs, axis=0)
    gi = lax.broadcasted_iota(I32, (N_GROUPS, tm), 0)
    gsel = jnp.zeros((N_GROUPS, tm), jnp.bool_)
    for _ in range(TOPK_GROUPS):
        _, idx = _first_argmax(gsc, gi, N_GROUPS)
        hit = gi == idx
        gsel = jnp.logical_or(gsel, hit)
        gsc = jnp.where(hit, NEG_INF, gsc)
    masked = jnp.concatenate(
        [jnp.where(gsel[g:g + 1], biased[g * GROUP_SIZE:(g + 1) * GROUP_SIZE], NEG_INF)
         for g in range(N_GROUPS)], axis=0)
    ei = lax.broadcasted_iota(I32, (N_EXPERTS, tm), 0)
    eidx, wts = [], []
    onehot = jnp.zeros((N_EXPERTS, tm), F32)
    for _ in range(TOP_K):
        _, idx = _first_argmax(masked, ei, N_EXPERTS)
        hit = ei == idx
        eidx.append(idx)
        wts.append(jnp.sum(jnp.where(hit, scores, 0.0), axis=0, keepdims=True))
        onehot = jnp.where(hit, 1.0, onehot)
        masked = jnp.where(hit, NEG_INF, masked)
    w = jnp.concatenate(wts, axis=0)
    w = w / (jnp.sum(w, axis=0, keepdims=True) + 1e-20) * ROUTE_SCALE
    e_ref[...] = jnp.concatenate(eidx, axis=0)
    w_ref[...] = w
    before = jnp.dot(onehot.astype(BF16), tri_ref[...], preferred_element_type=F32)
    before = before + carry_ref[...]
    ranks = [jnp.sum(jnp.where(ei == idx, before, 0.0), axis=0, keepdims=True) for idx in eidx]
    r_ref[...] = jnp.concatenate(ranks, axis=0).astype(I32)
    carry = carry_ref[...] + jnp.sum(onehot, axis=1, keepdims=True)
    carry_ref[...] = carry
    cnt_ref[...] = carry.astype(I32)


def _route(logits_t, b_router, tm):
    E, T = logits_t.shape
    tri = (jnp.arange(tm)[:, None] < jnp.arange(tm)[None, :]).astype(BF16)
    tok = lambda i: (0, i)
    const = lambda i: (0, 0)
    return pl.pallas_call(
        _route_kernel,
        out_shape=(jax.ShapeDtypeStruct((TOP_K, T), I32),
                   jax.ShapeDtypeStruct((TOP_K, T), F32),
                   jax.ShapeDtypeStruct((TOP_K, T), I32),
                   jax.ShapeDtypeStruct((E, 1), I32)),
        grid=(T // tm,),
        in_specs=[pl.BlockSpec((E, tm), tok),
                  pl.BlockSpec((E, 1), const),
                  pl.BlockSpec((tm, tm), const)],
        out_specs=(pl.BlockSpec((TOP_K, tm), tok),
                   pl.BlockSpec((TOP_K, tm), tok),
                   pl.BlockSpec((TOP_K, tm), tok),
                   pl.BlockSpec((E, 1), const)),
        scratch_shapes=[pltpu.VMEM((E, 1), F32)],
        compiler_params=_cparams(("arbitrary",)),
        name="route_topk",
    )(logits_t, b_router, tri)


def _pos_kernel(e_ref, r_ref, ps_ref, o_ref):
    e = e_ref[...]
    tm = e.shape[1]
    ei = lax.broadcasted_iota(I32, (N_EXPERTS, tm), 0)
    ps = ps_ref[...]
    rows = [jnp.sum(jnp.where(ei == e[k:k + 1], ps, 0), axis=0, keepdims=True)
            for k in range(TOP_K)]
    o_ref[...] = jnp.concatenate(rows, axis=0) + r_ref[...]


def _positions(eidx, rank, pstart, tm):
    K, T = eidx.shape
    tok = lambda i: (0, i)
    return pl.pallas_call(
        _pos_kernel,
        out_shape=jax.ShapeDtypeStruct((K, T), I32),
        grid=(T // tm,),
        in_specs=[pl.BlockSpec((K, tm), tok), pl.BlockSpec((K, tm), tok),
                  pl.BlockSpec((N_EXPERTS, 1), lambda i: (0, 0))],
        out_specs=pl.BlockSpec((K, tm), tok),
        compiler_params=_cparams(("parallel",)),
        name="dispatch_positions",
    )(eidx, rank, pstart)


def _sc_mesh():
    return plsc.VectorSubcoreMesh(core_axis_name="c", subcore_axis_name="s")


def _worker_id():
    return lax.axis_index("s") * SC_CORES + lax.axis_index("c")


def _window_positions(pos):
    K, T = pos.shape
    nwin = T // (SC_WORKERS * SC_WINDOW)
    assert nwin * SC_WORKERS * SC_WINDOW == T, T
    return pos.reshape(K, SC_WORKERS, nwin, SC_WINDOW).transpose(1, 2, 0, 3)


def _sc_dispatch(h, pos, P):
    T, D = h.shape
    pos4 = _window_positions(pos)
    NW, nwin, K, W = pos4.shape

    @functools.partial(
        pl.kernel, mesh=_sc_mesh(),
        out_type=jax.ShapeDtypeStruct((P, D), h.dtype),
        scratch_types=[pltpu.VMEM((K, W), I32), pltpu.VMEM((W, D), h.dtype),
                       pltpu.SemaphoreType.DMA],
        name="sc_dispatch")
    def k(h_hbm, pos_hbm, xs_hbm, idx_v, rows_v, sem):
        wid = _worker_id()

        @pl.loop(0, nwin)
        def _(j):
            base = (wid * nwin + j) * W
            pltpu.sync_copy(pos_hbm.at[wid, j], idx_v)
            pltpu.sync_copy(h_hbm.at[pl.ds(base, W)], rows_v)
            copies = [pltpu.async_copy(rows_v, xs_hbm.at[idx_v.at[kk]], sem)
                      for kk in range(K)]
            for c in copies:
                c.wait()

    return k(h, pos4)


def _sc_gather(ys, pos):
    P, D = ys.shape
    T = pos.shape[1]
    pos4 = _window_positions(pos)
    NW, nwin, K, W = pos4.shape

    @functools.partial(
        pl.kernel, mesh=_sc_mesh(),
        out_type=jax.ShapeDtypeStruct((K, T, D), ys.dtype),
        scratch_types=[pltpu.VMEM((K, W), I32), pltpu.VMEM((W, D), ys.dtype),
                       pltpu.SemaphoreType.DMA],
        name="sc_gather")
    def k(ys_hbm, pos_hbm, yg_hbm, idx_v, rows_v, sem):
        wid = _worker_id()

        @pl.loop(0, nwin)
        def _(j):
            base = (wid * nwin + j) * W
            pltpu.sync_copy(pos_hbm.at[wid, j], idx_v)
            for kk in range(K):
                pltpu.async_copy(ys_hbm.at[idx_v.at[kk]], rows_v, sem).wait()
                pltpu.sync_copy(rows_v, yg_hbm.at[kk, pl.ds(base, W)])

    return k(ys, pos4)


def _expert_kernel(be_ref, nv_ref, x_ref, wg_ref, wu_ref, wd_ref, o_ref):
    i = pl.program_id(0)

    @pl.when(i < nv_ref[0])
    def _():
        lo, hi = _unpack_rows(x_ref[...])
        x = jnp.concatenate([lo, hi], axis=1).astype(BF16)
        a = (_silu(jnp.dot(x, wg_ref[0], preferred_element_type=F32))
             * jnp.dot(x, wu_ref[0], preferred_element_type=F32))
        o_ref[...] = _pack_rows(jnp.dot(a.astype(BF16), wd_ref[0], preferred_element_type=F32))


def _experts(xs, blk_e, nvalid, wg, wu, wd):
    P, Dh = xs.shape
    D = 2 * Dh
    bm = EXPERT_BLOCK
    nb = P // bm
    rowmap = lambda i, be, nv: (jnp.minimum(i, nv[0] - 1), 0)
    wmap = lambda i, be, nv: (be[i], 0, 0)
    grid_spec = pltpu.PrefetchScalarGridSpec(
        num_scalar_prefetch=2,
        grid=(nb,),
        in_specs=[pl.BlockSpec((bm, Dh), rowmap),
                  pl.BlockSpec((1, D, EXPERT_DIM), wmap),
                  pl.BlockSpec((1, D, EXPERT_DIM), wmap),
                  pl.BlockSpec((1, EXPERT_DIM, D), wmap)],
        out_specs=pl.BlockSpec((bm, Dh), rowmap),
    )
    return pl.pallas_call(
        _expert_kernel,
        out_shape=jax.ShapeDtypeStruct((P, Dh), I32),
        grid_spec=grid_spec,
        compiler_params=_cparams(("arbitrary",)),
        name="expert_ffn",
    )(blk_e, nvalid, xs, wg, wu, wd)


def _final_kernel(yg_ref, w_ref, h2_ref, x1_ref, mod_ref, modf_ref, nf_ref,
                  sg_ref, su_ref, sd_ref, o_ref):
    w = w_ref[...]
    y_lo = y_hi = None
    for k in range(TOP_K):
        lo, hi = _unpack_rows(yg_ref[k])
        wk = w[:, k:k + 1]
        y_lo = lo * wk if y_lo is None else y_lo + lo * wk
        y_hi = hi * wk if y_hi is None else y_hi + hi * wk
    y = jnp.concatenate([y_lo, y_hi], axis=1)
    hb = jnp.concatenate(_unpack_rows(h2_ref[...]), axis=1).astype(BF16)
    a = (_silu(jnp.dot(hb, sg_ref[...], preferred_element_type=F32))
         * jnp.dot(hb, su_ref[...], preferred_element_type=F32))
    shared = jnp.dot(a.astype(BF16), sd_ref[...], preferred_element_type=F32)
    mod = mod_ref[0]
    modf = modf_ref[0]
    x2 = x1_ref[...] + mod[5:6] * (y + shared)
    o_ref[...] = _rms(x2) * nf_ref[...] * (1.0 + modf[1:2]) + modf[0:1]


def _final(yg, wts_t, h2, x1, mod, modf, nfg, sg, su, sd, S, tm):
    T, D = x1.shape
    row = lambda i: (i, 0)
    const = lambda i: (0, 0)
    bat = lambda i: ((i * tm) // S, 0, 0)
    return pl.pallas_call(
        _final_kernel,
        out_shape=jax.ShapeDtypeStruct((T, D), F32),
        grid=(T // tm,),
        in_specs=[pl.BlockSpec((TOP_K, tm, D // 2), lambda i: (0, i, 0)),
                  pl.BlockSpec((tm, TOP_K), row),
                  pl.BlockSpec((tm, D // 2), row),
                  pl.BlockSpec((tm, D), row),
                  pl.BlockSpec((1, 6, D), bat),
                  pl.BlockSpec((1, 2, D), bat),
                  pl.BlockSpec((1, D), const),
                  pl.BlockSpec(sg.shape, const),
                  pl.BlockSpec(su.shape, const),
                  pl.BlockSpec(sd.shape, const)],
        out_specs=pl.BlockSpec((tm, D), row),
        compiler_params=_cparams(("parallel",)),
        name="combine_shared_final",
    )(yg, wts_t, h2, x1, mod, modf, nfg, sg, su, sd)


def _tile(n, pref):
    t = min(n, pref)
    assert n % t == 0, (n, pref)
    return t


def _moe(h2, logits_t, x1, mod, modf, p, S):
    T = h2.shape[0]
    tm_r = _tile(T, 512)
    eidx, wts, rank, counts = _route(logits_t, p["b_router"], tm_r)
    bm = EXPERT_BLOCK
    counts = counts[:, 0]
    padded = (counts + bm - 1) // bm * bm
    pends = jnp.cumsum(padded)
    pstart = (pends - padded).astype(I32)
    P = T * TOP_K + N_EXPERTS * bm
    nb = P // bm
    pos = _positions(eidx, rank, pstart[:, None], tm_r)
    blk_e = jnp.minimum(jnp.searchsorted(pends, jnp.arange(nb) * bm, side="right"),
                        N_EXPERTS - 1).astype(I32)
    nvalid = (pends[-1] // bm).astype(I32)
    blk_e = jnp.where(jnp.arange(nb) < nvalid, blk_e, blk_e[jnp.maximum(nvalid - 1, 0)])
    xs = _sc_dispatch(h2, pos, P)
    ys = _experts(xs, blk_e, nvalid[None], p["wg"], p["wu"], p["wd"])
    yg = _sc_gather(ys, pos)
    tm_f = _tile(S, 256)
    return _final(yg, wts.T, h2, x1, mod, modf, p["normf_g"], p["sg"], p["su"], p["sd"], S, tm_f)


NORM_SLACK = 1.02
REF_GAP_LIMIT = 100.0


def _diff_attention(proj, kaug, nrm, posq, p, B, S):
    qmax = jnp.sqrt(jnp.max(nrm[:, 0, 0, :2 * ATT_HEADS])) * NORM_SLACK
    kmax = jnp.sqrt(jnp.max(nrm[:, 1, 0, :2 * ATT_HEADS])) * NORM_SLACK
    in_range = 2.0 * qmax * kmax <= REF_GAP_LIMIT
    scal = jnp.concatenate([-p["slopes"], kmax[None]]).astype(F32)

    def single_pass(_):
        return _attention_ref(proj, kaug, scal, p["lam_p"], p["subln_g"], posq, B, S,
                              _tile(S, 512), _tile(S, 1024))

    def online(_):
        return _attention(proj, p["slopes"], p["lam_p"], p["subln_g"], B, S,
                          _tile(S, 256), _tile(S, 512))

    return lax.cond(in_range, single_pass, online, None)


def _trunk(x, mod, modf, p):
    B, S, D = x.shape
    T = B * S
    xt = x.reshape(T, D)
    tm = _tile(S, 256)
    posq, posk = _alibi_tables(p["slopes"], S)
    proj, kaug, nrm = _proj(xt, mod, p["norm1_g"], p["w_in"], posk, S, tm)
    o_att = _diff_attention(proj, kaug, nrm, posq, p, B, S)
    o_fw, o_bw = _hgrn2(proj, p["lbs"], B, S)
    x1, h2, logits_t = _mixout(xt, o_att, o_fw, o_bw, proj, mod, p["hg_norm_g"], p["w_out"],
                               p["norm2_g"], p["w_router"], S, tm)
    out = _moe(h2, logits_t, x1, mod, modf, p, S)
    return out.reshape(B, S, D)


def kernel(x_prompt, x_sample, c_prompt, c_sample, w_ada, b_ada, norm1_g, w_in, lam_q1, lam_k1, lam_q2, lam_k2, subln_g, hg_lb_logits, hg_norm_g, w_out, norm2_g, w_router, b_router, w_exp_gate, w_exp_up, w_exp_down, w_sh_gate, w_sh_up, w_sh_down, w_ada_f, b_ada_f, normf_g):
    D = x_prompt.shape[-1]
    Bp, Bs = c_prompt.shape[0], c_sample.shape[0]
    c_all = jnp.concatenate([c_prompt, c_sample], axis=0)
    R = -(-c_all.shape[0] // 8) * 8
    c_all = jnp.pad(c_all, ((0, R - c_all.shape[0]), (0, 0)))
    w_all = jnp.concatenate([w_ada[0], w_ada_f], axis=1)
    b_all = jnp.concatenate([b_ada[0], b_ada_f], axis=0)[None]
    mod_all = _ada(c_all, w_all, b_all)
    mod6 = mod_all[:, :6 * D].reshape(R, 6, D)
    mod2 = mod_all[:, 6 * D:].reshape(R, 2, D)

    lbs = jax.nn.softmax(hg_lb_logits.astype(F32), axis=0)[0]
    lbs = lbs.reshape(2, HG_HEADS, HEAD_W).transpose(1, 0, 2)
    slopes = (jnp.exp2(-8.0 * (jnp.arange(ATT_HEADS, dtype=F32) + 1.0) / ATT_HEADS) * LOG2E)
    p = dict(
        norm1_g=norm1_g[0][None], w_in=w_in[0].astype(BF16),
        slopes=slopes.astype(F32),
        lam_p=jnp.stack([lam_q1[0], lam_k1[0], lam_q2[0], lam_k2[0]]).astype(F32),
        subln_g=subln_g[0][:, None], lbs=lbs, hg_norm_g=hg_norm_g[0][None],
        w_out=w_out[0].astype(BF16), norm2_g=norm2_g[0][None],
        w_router=w_router[0], b_router=b_router[0][:, None],
        wg=w_exp_gate[0].astype(BF16), wu=w_exp_up[0].astype(BF16),
        wd=w_exp_down[0].astype(BF16),
        sg=w_sh_gate[0].astype(BF16), su=w_sh_up[0].astype(BF16), sd=w_sh_down[0].astype(BF16),
        normf_g=normf_g[None],
    )
    y_prompt = _trunk(x_prompt, mod6[:Bp], mod2[:Bp], p)
    y_sample = _trunk(x_sample, mod6[Bp:Bp + Bs], mod2[Bp:Bp + Bs], p)
    return (y_prompt, y_sample)
```

```python
import functools
import math

import jax
import jax.numpy as jnp
from jax import lax
from jax.experimental import pallas as pl
from jax.experimental.pallas import tpu as pltpu
from jax.experimental.pallas import tpu_sc as plsc

F32 = jnp.float32
BF16 = jnp.bfloat16
I32 = jnp.int32

EPS = 1e-6
LOG2E = 1.4426950408889634

ATT_HEADS = 4
ATT_QK_DIM = 64
HEAD_W = 128
ATT_WIDTH = ATT_HEADS * HEAD_W
HG_HEADS = 4
HG_WIDTH = HG_HEADS * HEAD_W
N_EXPERTS = 256
TOP_K = 8
N_GROUPS = 8
TOPK_GROUPS = 4
GROUP_SIZE = N_EXPERTS // N_GROUPS
EXPERT_DIM = 256
ROUTE_SCALE = 2.5
LAM_INIT = 0.8 - 0.6 * math.exp(-0.3 * 0)

HG_CHUNK = 128
HG_SUB = 16
HG_MILD_LOG_GATE = 3.75
HG_STEP_CHUNKS = 4
EXPERT_BLOCK = 512
SC_CORES = 2
SC_SUBCORES = 16
SC_WORKERS = SC_CORES * SC_SUBCORES
SC_WINDOW = 32
VMEM_LIMIT = 56 * 1024 * 1024

NEG_INF = float("-inf")


def _cparams(sem):
    return pltpu.CompilerParams(dimension_semantics=sem, vmem_limit_bytes=VMEM_LIMIT)


def _silu(x):
    return x * jax.nn.sigmoid(x)


def _pack_rows(x):
    half = x.shape[1] // 2
    bits = lax.bitcast_convert_type(x.astype(BF16).astype(F32), jnp.uint32)
    return lax.bitcast_convert_type((bits[:, :half] >> 16) | bits[:, half:], I32)


def _unpack_rows(w):
    u = lax.bitcast_convert_type(w, jnp.uint32)
    lo = lax.bitcast_convert_type(u << 16, F32)
    hi = lax.bitcast_convert_type(u & jnp.uint32(0xFFFF0000), F32)
    return lo, hi


def _ada_kernel(c_ref, w_ref, b_ref, o_ref):
    cs = _silu(c_ref[...])
    o_ref[...] = jnp.dot(cs, w_ref[...], preferred_element_type=F32,
                         precision=lax.Precision.HIGHEST) + b_ref[...]


def _ada(c, w, b):
    R, D = c.shape
    N = w.shape[1]
    tn = 1024
    return pl.pallas_call(
        _ada_kernel,
        out_shape=jax.ShapeDtypeStruct((R, N), F32),
        grid=(N // tn,),
        in_specs=[pl.BlockSpec((R, D), lambda j: (0, 0)),
                  pl.BlockSpec((D, tn), lambda j: (0, j)),
                  pl.BlockSpec((1, tn), lambda j: (0, j))],
        out_specs=pl.BlockSpec((R, tn), lambda j: (0, j)),
        compiler_params=_cparams(("arbitrary",)),
        name="ada_mod",
    )(c, w, b)


def _rms(x):
    return x * lax.rsqrt(jnp.mean(x * x, axis=-1, keepdims=True) + EPS)


def _proj_kernel(x_ref, mod_ref, g_ref, w_ref, posk_ref, grp_ref,
                 o_ref, q_ref, ka_ref, v_ref, nrm_ref, *, qscale):
    x = x_ref[...]
    mod = mod_ref[0]
    h = _rms(x) * g_ref[...] * (1.0 + mod[1:2]) + mod[0:1]
    acc = jnp.dot(h.astype(BF16), w_ref[...], preferred_element_type=F32)
    qw = ATT_WIDTH
    qb = (acc[:, :qw] * qscale).astype(BF16)
    kb = acc[:, qw:2 * qw].astype(BF16)
    vb = acc[:, 2 * qw:3 * qw].astype(BF16)
    o_ref[...] = acc[:, 3 * qw:].astype(BF16)
    for hh in range(ATT_HEADS):
        cols = slice(hh * HEAD_W, (hh + 1) * HEAD_W)
        q_ref[hh] = qb[:, cols]
        v_ref[hh] = vb[:, cols]
        ka_ref[hh, :, :HEAD_W] = kb[:, cols]
        ka_ref[hh, :, HEAD_W:] = posk_ref[:, cols]
    for idx, t in enumerate((qb, kb)):
        tf = t.astype(F32)
        n2 = jnp.dot((tf * tf).astype(BF16), grp_ref[...], preferred_element_type=F32)
        nrm_ref[0, idx] = jnp.broadcast_to(jnp.max(n2, axis=0, keepdims=True), (8, HEAD_W))


def _proj(xt, mod, g, w, posk, S, tm):
    T, D = xt.shape
    N = w.shape[1]
    qscale = ATT_QK_DIM ** -0.5 * LOG2E
    nblk = T // tm
    spb = S // tm
    grp = (jnp.arange(ATT_WIDTH)[:, None] // ATT_QK_DIM == jnp.arange(HEAD_W)[None, :]).astype(BF16)
    return pl.pallas_call(
        functools.partial(_proj_kernel, qscale=qscale),
        out_shape=(jax.ShapeDtypeStruct((T, N - 3 * ATT_WIDTH), BF16),
                   jax.ShapeDtypeStruct((ATT_HEADS, T, HEAD_W), BF16),
                   jax.ShapeDtypeStruct((ATT_HEADS, T, 2 * HEAD_W), BF16),
                   jax.ShapeDtypeStruct((ATT_HEADS, T, HEAD_W), BF16),
                   jax.ShapeDtypeStruct((nblk, 2, 8, HEAD_W), F32)),
        grid=(nblk,),
        in_specs=[pl.BlockSpec((tm, D), lambda i: (i, 0)),
                  pl.BlockSpec((1, 6, D), lambda i: ((i * tm) // S, 0, 0)),
                  pl.BlockSpec((1, D), lambda i: (0, 0)),
                  pl.BlockSpec((D, N), lambda i: (0, 0)),
                  pl.BlockSpec((tm, ATT_WIDTH), lambda i: (i % spb, 0)),
                  pl.BlockSpec((ATT_WIDTH, HEAD_W), lambda i: (0, 0))],
        out_specs=(pl.BlockSpec((tm, N - 3 * ATT_WIDTH), lambda i: (i, 0)),
                   pl.BlockSpec((ATT_HEADS, tm, HEAD_W), lambda i: (0, i, 0)),
                   pl.BlockSpec((ATT_HEADS, tm, 2 * HEAD_W), lambda i: (0, i, 0)),
                   pl.BlockSpec((ATT_HEADS, tm, HEAD_W), lambda i: (0, i, 0)),
                   pl.BlockSpec((1, 2, 8, HEAD_W), lambda i: (i, 0, 0, 0))),
        compiler_params=_cparams(("parallel",)),
        name="norm1_proj",
    )(xt, mod, g, w, posk, grp)


def _split3(x):
    hi = x.astype(BF16)
    r = x - hi.astype(F32)
    mid = r.astype(BF16)
    lo = (r - mid.astype(F32)).astype(BF16)
    return hi, mid, lo


def _alibi_tables(slopes2, S):
    H = slopes2.shape[0]
    a = slopes2[:, None] * jnp.arange(S, dtype=F32)[None, :]
    ah, am, al = _split3(a)
    one = jnp.ones((H, S), BF16)
    zero = jnp.zeros((H, S), BF16)
    pad = jnp.zeros((H, S, HEAD_W - 9), BF16)
    posk = jnp.concatenate([jnp.stack([one] * 6 + [ah, am, al], axis=-1), pad], axis=-1)
    posq = jnp.concatenate([jnp.stack([zero] * 3 + [-ah, -am, -al] + [one] * 3, axis=-1), pad], axis=-1)
    return posq, posk.transpose(1, 0, 2).reshape(S, H * HEAD_W)


def _attn_kernel(slope_ref, lam_ref, q_ref, k_ref, v_ref, g_ref, o_ref,
                 qm_ref, m_ref, l_ref, acc_ref, *, tq, tk, nk):
    h = pl.program_id(1)
    qi = pl.program_id(2)
    ki = pl.program_id(3)

    @pl.when(ki == 0)
    def _init():
        q = q_ref[...]
        lane = lax.broadcasted_iota(I32, q.shape, 1)
        zero = jnp.zeros_like(q)
        qm_ref[0] = jnp.where(lane < ATT_QK_DIM, q, zero)
        qm_ref[1] = jnp.where(lane >= ATT_QK_DIM, q, zero)
        m_ref[...] = jnp.full(m_ref.shape, NEG_INF, F32)
        l_ref[...] = jnp.zeros(l_ref.shape, F32)
        acc_ref[...] = jnp.zeros(acc_ref.shape, F32)

    slope = slope_ref[h]
    k = k_ref[...]
    v = v_ref[...]
    row = lax.broadcasted_iota(I32, (tk, tq), 0)
    col = lax.broadcasted_iota(I32, (tk, tq), 1)
    dist = (col - row + (qi * tq - ki * tk)).astype(F32)
    bias = jnp.abs(dist) * (-slope)
    for m in range(2):
        s = lax.dot_general(k, qm_ref[m], (((1,), (1,)), ((), ())),
                            preferred_element_type=F32) + bias
        m_old = m_ref[m]
        m_new = jnp.maximum(m_old, jnp.max(s, axis=0, keepdims=True))
        alpha = jnp.exp2(m_old - m_new)
        p = jnp.exp2(s - m_new)
        l_ref[m] = alpha * l_ref[m] + jnp.sum(p, axis=0, keepdims=True)
        pv = lax.dot_general(v, p.astype(BF16), (((0,), (0,)), ((), ())),
                             preferred_element_type=F32)
        acc_ref[m] = alpha * acc_ref[m] + pv
        m_ref[m] = m_new

    @pl.when(ki == nk - 1)
    def _fin():
        lp = lam_ref[...]
        lam = (jnp.exp(jnp.sum(lp[0:1] * lp[1:2], axis=-1, keepdims=True))
               - jnp.exp(jnp.sum(lp[2:3] * lp[3:4], axis=-1, keepdims=True))
               + LAM_INIT)
        o = acc_ref[0] / l_ref[0] - lam * (acc_ref[1] / l_ref[1])
        ms = jnp.mean(o * o, axis=0, keepdims=True)
        y = o * lax.rsqrt(ms + EPS) * g_ref[...] * (1.0 - LAM_INIT)
        o_ref[...] = y.T.astype(o_ref.dtype)


def _attention(qh, kaug, vh, slopes, lam_p, subln_g, B, S, tq, tk):
    T = B * S
    nq, nk = S // tq, S // tk
    grid_spec = pltpu.PrefetchScalarGridSpec(
        num_scalar_prefetch=1,
        grid=(B, ATT_HEADS, nq, nk),
        in_specs=[
            pl.BlockSpec((4, ATT_QK_DIM), lambda b, h, qi, ki, s: (0, 0)),
            pl.BlockSpec((None, tq, HEAD_W), lambda b, h, qi, ki, s: (h, b * nq + qi, 0)),
            pl.BlockSpec((None, tk, HEAD_W), lambda b, h, qi, ki, s: (h, b * nk + ki, 0)),
            pl.BlockSpec((None, tk, HEAD_W), lambda b, h, qi, ki, s: (h, b * nk + ki, 0)),
            pl.BlockSpec((HEAD_W, 1), lambda b, h, qi, ki, s: (0, 0)),
        ],
        out_specs=pl.BlockSpec((tq, HEAD_W), lambda b, h, qi, ki, s: (b * nq + qi, h)),
        scratch_shapes=[pltpu.VMEM((2, tq, HEAD_W), BF16),
                        pltpu.VMEM((2, 1, tq), F32),
                        pltpu.VMEM((2, 1, tq), F32),
                        pltpu.VMEM((2, HEAD_W, tq), F32)],
    )
    return pl.pallas_call(
        functools.partial(_attn_kernel, tq=tq, tk=tk, nk=nk),
        out_shape=jax.ShapeDtypeStruct((T, ATT_WIDTH), BF16),
        grid_spec=grid_spec,
        compiler_params=_cparams(("parallel", "parallel", "parallel", "arbitrary")),
        name="diff_attention",
    )(slopes, lam_p, qh, kaug, vh, subln_g)


def _attn_finish(lam_ref, g_ref, o_ref, l_ref, acc_ref):
    lp = lam_ref[...]
    lam = (jnp.exp(jnp.sum(lp[0:1] * lp[1:2], axis=-1, keepdims=True))
           - jnp.exp(jnp.sum(lp[2:3] * lp[3:4], axis=-1, keepdims=True))
           + LAM_INIT)
    o = acc_ref[0] / l_ref[0] - lam * (acc_ref[1] / l_ref[1])
    ms = jnp.mean(o * o, axis=0, keepdims=True)
    y = o * lax.rsqrt(ms + EPS) * g_ref[...] * (1.0 - LAM_INIT)
    o_ref[...] = y.T.astype(o_ref.dtype)


def _attn_ref_kernel(sc_ref, lam_ref, q_ref, ka_ref, v_ref, g_ref, posq_ref, o_ref,
                     qa_ref, l_ref, acc_ref, *, tq, tk, nk):
    h = pl.program_id(1)
    qi = pl.program_id(2)
    ki = pl.program_id(3)

    @pl.when(ki == 0)
    def _init():
        q = q_ref[...]
        lane = lax.broadcasted_iota(I32, q.shape, 1)
        zero = jnp.zeros_like(q)
        kmax = sc_ref[ATT_HEADS]
        pa = posq_ref[...]
        for m in range(2):
            sel = (lane < ATT_QK_DIM) if m == 0 else (lane >= ATT_QK_DIM)
            qm = jnp.where(sel, q, zero)
            qf = qm.astype(F32)
            ub = jnp.sqrt(jnp.sum(qf * qf, axis=1, keepdims=True)) * kmax
            uh, um, ul = (t.astype(F32) for t in _split3(jnp.broadcast_to(ub, q.shape)))
            ubp = jnp.where(lane == 0, -uh, jnp.where(lane == 1, -um, jnp.where(lane == 2, -ul, 0.0)))
            ubp = ubp.astype(BF16)
            for var, pp in enumerate((pa + ubp, ubp - pa, ubp)):
                qa_ref[2 * var + m] = jnp.concatenate([qm, pp], axis=1)
        l_ref[...] = jnp.zeros(l_ref.shape, F32)
        acc_ref[...] = jnp.zeros(acc_ref.shape, F32)

    rel = qi * tq - ki * tk
    near = jnp.logical_and(rel > -tq, rel < tk)

    def body(with_bias):
        ka = ka_ref[...]
        if with_bias:
            row = lax.broadcasted_iota(I32, (tk, tq), 0)
            col = lax.broadcasted_iota(I32, (tk, tq), 1)
            bias = jnp.abs((col - row + rel).astype(F32)) * sc_ref[h]
            var = 2
        else:
            var = jnp.where(rel > 0, 0, 1)
        ss = [lax.dot_general(ka, qa_ref[2 * var + m], (((1,), (1,)), ((), ())),
                              preferred_element_type=F32) for m in range(2)]
        v = v_ref[...]
        for m in range(2):
            p = jnp.exp2(ss[m] + bias if with_bias else ss[m])
            l_ref[m] += jnp.sum(p, axis=0, keepdims=True)
            acc_ref[m] += lax.dot_general(v, p.astype(BF16), (((0,), (0,)), ((), ())),
                                          preferred_element_type=F32)

    @pl.when(near)
    def _():
        body(True)

    @pl.when(jnp.logical_not(near))
    def _():
        body(False)

    @pl.when(ki == nk - 1)
    def _fin():
        _attn_finish(lam_ref, g_ref, o_ref, l_ref, acc_ref)


def _attention_ref(qh, kaug, vh, scal, lam_p, subln_g, posq, B, S, tq, tk):
    T = B * S
    nq, nk = S // tq, S // tk
    grid_spec = pltpu.PrefetchScalarGridSpec(
        num_scalar_prefetch=1,
        grid=(B, ATT_HEADS, nq, nk),
        in_specs=[
            pl.BlockSpec((4, ATT_QK_DIM), lambda b, h, qi, ki, s: (0, 0)),
            pl.BlockSpec((None, tq, HEAD_W), lambda b, h, qi, ki, s: (h, b * nq + qi, 0)),
            pl.BlockSpec((None, tk, 2 * HEAD_W), lambda b, h, qi, ki, s: (h, b * nk + ki, 0)),
            pl.BlockSpec((None, tk, HEAD_W), lambda b, h, qi, ki, s: (h, b * nk + ki, 0)),
            pl.BlockSpec((HEAD_W, 1), lambda b, h, qi, ki, s: (0, 0)),
            pl.BlockSpec((None, tq, HEAD_W), lambda b, h, qi, ki, s: (h, qi, 0)),
        ],
        out_specs=pl.BlockSpec((tq, HEAD_W), lambda b, h, qi, ki, s: (b * nq + qi, h)),
        scratch_shapes=[pltpu.VMEM((6, tq, 2 * HEAD_W), BF16),
                        pltpu.VMEM((2, 1, tq), F32),
                        pltpu.VMEM((2, HEAD_W, tq), F32)],
    )
    return pl.pallas_call(
        functools.partial(_attn_ref_kernel, tq=tq, tk=tk, nk=nk),
        out_shape=jax.ShapeDtypeStruct((T, ATT_WIDTH), BF16),
        grid_spec=grid_spec,
        compiler_params=_cparams(("parallel", "parallel", "parallel", "arbitrary")),
        name="diff_attention_ref",
    )(scal, lam_p, qh, kaug, vh, subln_g, posq)


def _hg_chunk(q, kk, v, g, st, reverse):
    C = q.shape[0]
    nsub = C // HG_SUB
    row = lax.broadcasted_iota(I32, (C, HEAD_W), 0)
    b = _hg_cumsum(g, reverse)
    b_end = b[C - 1:C] if not reverse else b[0:1]

    qd = (q * jnp.exp(b)).astype(BF16)
    o_inter = lax.dot_general(qd, st.astype(BF16), (((1,), (1,)), ((), ())),
                              preferred_element_type=F32)

    ones = jnp.ones((HEAD_W, HEAD_W), BF16)
    srow = lax.broadcasted_iota(I32, (HG_SUB, HEAD_W), 0)
    vb = v.astype(BF16)
    outs = []
    for blk in range(nsub):
        lo, hi = blk * HG_SUB, (blk + 1) * HG_SUB
        qI, kI, vI, bI = q[lo:hi], kk[lo:hi], v[lo:hi], b[lo:hi]
        pieces = []
        for t in range(HG_SUB):
            dl = bI[t:t + 1] - bI
            keep = (srow <= t) if not reverse else (srow >= t)
            e = jnp.exp(jnp.where(keep, dl, NEG_INF))
            pieces.append(qI[t:t + 1] * kI * e)
        wst = jnp.concatenate(pieces, axis=0).astype(BF16)
        rsum = jnp.dot(wst, ones, preferred_element_type=F32)
        o_blk = jnp.sum(rsum.reshape(HG_SUB, HG_SUB, HEAD_W) * vI[None], axis=1)
        has_off = (blk > 0) if not reverse else (blk < nsub - 1)
        if has_off:
            if not reverse:
                r = b[lo - 1:lo]
                kmask = row < lo
            else:
                r = b[hi:hi + 1]
                kmask = row >= hi
            qs = (qI * jnp.exp(bI - r)).astype(BF16)
            ks = (kk * jnp.exp(jnp.where(kmask, r - b, NEG_INF))).astype(BF16)
            a = lax.dot_general(qs, ks, (((1,), (1,)), ((), ())),
                                preferred_element_type=F32)
            o_blk = o_blk + jnp.dot(a.astype(BF16), vb, preferred_element_type=F32)
        outs.append(o_blk)
    o = o_inter + jnp.concatenate(outs, axis=0)

    kd = (kk * jnp.exp(b_end - b)).astype(BF16)
    upd = lax.dot_general(vb, kd, (((0,), (0,)), ((), ())),
                          preferred_element_type=F32)
    st_new = st * jnp.exp(b_end) + upd
    return o, st_new


def _hg_cumsum(g, reverse):
    C = g.shape[0]
    row = lax.broadcasted_iota(I32, g.shape, 0)
    b = g
    d = 1
    while d < C:
        if not reverse:
            b = b + jnp.where(row >= d, pltpu.roll(b, d, axis=0), 0.0)
        else:
            b = b + jnp.where(row < C - d, pltpu.roll(b, C - d, axis=0), 0.0)
        d *= 2
    return b


def _hg_chunk_mild(q, kk, v, g, st, reverse):
    C = q.shape[0]
    nsub = C // HG_SUB
    row = lax.broadcasted_iota(I32, (C, HEAD_W), 0)
    b = _hg_cumsum(g, reverse)
    b_end = b[C - 1:C] if not reverse else b[0:1]
    qd = (q * jnp.exp(b)).astype(BF16)
    o_inter = lax.dot_general(qd, st.astype(BF16), (((1,), (1,)), ((), ())),
                              preferred_element_type=F32)
    vb = v.astype(BF16)
    qrow = lax.broadcasted_iota(I32, (HG_SUB, C), 0)
    kcol = lax.broadcasted_iota(I32, (HG_SUB, C), 1)
    zero_ref = jnp.zeros((1, HEAD_W), F32)
    a_rows = []
    for blk in range(nsub):
        lo, hi = blk * HG_SUB, (blk + 1) * HG_SUB
        if not reverse:
            r = b[lo - 1:lo] if blk > 0 else zero_ref
            kmask = row < hi
            amask = kcol <= qrow + lo
        else:
            r = b[hi:hi + 1] if blk < nsub - 1 else zero_ref
            kmask = row >= lo
            amask = kcol >= qrow + lo
        qs = (q[lo:hi] * jnp.exp(b[lo:hi] - r)).astype(BF16)
        ks = (kk * jnp.exp(jnp.where(kmask, r - b, NEG_INF))).astype(BF16)
        a = lax.dot_general(qs, ks, (((1,), (1,)), ((), ())), preferred_element_type=F32)
        a_rows.append(jnp.where(amask, a, 0.0))
    a_full = jnp.concatenate(a_rows, axis=0).astype(BF16)
    o = o_inter + jnp.dot(a_full, vb, preferred_element_type=F32)
    kd = (kk * jnp.exp(b_end - b)).astype(BF16)
    upd = lax.dot_general(vb, kd, (((0,), (0,)), ((), ())), preferred_element_type=F32)
    return o, st * jnp.exp(b_end) + upd


def _hg_kernel(qf_ref, ff_ref, if_ref, qb_ref, fb_ref, ib_ref, lb_ref,
               of_ref, ob_ref, sf_ref, sb_ref, *, nchunk):
    j = pl.program_id(2)

    @pl.when(j == 0)
    def _init():
        sf_ref[...] = jnp.zeros(sf_ref.shape, F32)
        sb_ref[...] = jnp.zeros(sb_ref.shape, F32)

    lb = lb_ref[0]
    C = HG_CHUNK

    def prep(q_ref, f_ref, i_ref, lbd, rows):
        q = _silu(q_ref[rows, :].astype(F32))
        f = lbd + (1.0 - lbd) * jax.nn.sigmoid(f_ref[rows, :].astype(F32))
        return q, 1.0 - f, i_ref[rows, :].astype(F32), jnp.log(f)

    def step(c, carry):
        rows_f = pl.ds(pl.multiple_of(c * C, C), C)
        rows_b = pl.ds(pl.multiple_of((nchunk - 1 - c) * C, C), C)
        fwd = prep(qf_ref, ff_ref, if_ref, lb[0:1], rows_f)
        bwd = prep(qb_ref, fb_ref, ib_ref, lb[1:2], rows_b)
        mild = jnp.minimum(jnp.min(fwd[3]), jnp.min(bwd[3])) >= -HG_MILD_LOG_GATE

        def run(chunk_fn):
            o, st = chunk_fn(*fwd, sf_ref[...], reverse=False)
            of_ref[rows_f, :] = o
            sf_ref[...] = st
            o, st = chunk_fn(*bwd, sb_ref[...], reverse=True)
            ob_ref[rows_b, :] = o
            sb_ref[...] = st

        @pl.when(mild)
        def _():
            run(_hg_chunk_mild)

        @pl.when(jnp.logical_not(mild))
        def _():
            run(_hg_chunk)

        return carry

    lax.fori_loop(0, nchunk, step, 0)


def _hgrn2(proj, lbs, B, S):
    T = B * S
    nchunk = min(HG_STEP_CHUNKS, S // HG_CHUNK)
    C = HG_CHUNK * nchunk
    n = S // C
    nh = HG_HEADS
    qc, ffc, fbc, ic = 0, nh, 2 * nh, 3 * nh

    def fw(col):
        return pl.BlockSpec((C, HEAD_W), lambda b, h, j: (b * n + j, col + h))

    def bw(col):
        return pl.BlockSpec((C, HEAD_W), lambda b, h, j: (b * n + n - 1 - j, col + h))

    out_f = pl.BlockSpec((C, HEAD_W), lambda b, h, j: (b * n + j, h))
    out_b = pl.BlockSpec((C, HEAD_W), lambda b, h, j: (b * n + n - 1 - j, h))
    return pl.pallas_call(
        functools.partial(_hg_kernel, nchunk=nchunk),
        out_shape=(jax.ShapeDtypeStruct((T, HG_WIDTH), F32),
                   jax.ShapeDtypeStruct((T, HG_WIDTH), F32)),
        grid=(B, nh, n),
        in_specs=[fw(qc), fw(ffc), fw(ic), bw(qc), bw(fbc), bw(ic),
                  pl.BlockSpec((1, 2, HEAD_W), lambda b, h, j: (h, 0, 0))],
        out_specs=(out_f, out_b),
        scratch_shapes=[pltpu.VMEM((HEAD_W, HEAD_W), F32),
                        pltpu.VMEM((HEAD_W, HEAD_W), F32)],
        compiler_params=_cparams(("parallel", "parallel", "arbitrary")),
        name="hgrn2_scan",
    )(proj, proj, proj, proj, proj, proj, lbs)


def _mixout_kernel(x_ref, oa_ref, of_ref, ob_ref, gate_ref, mod_ref, hgg_ref, wo_ref,
                   n2_ref, wr_ref, x1_ref, h2_ref, lg_ref):
    mod = mod_ref[0]
    o = of_ref[...] + ob_ref[...]
    gate = _silu(gate_ref[...].astype(F32))
    hg = jnp.concatenate(
        [_rms(o[:, h * HEAD_W:(h + 1) * HEAD_W]) * hgg_ref[...] for h in range(HG_HEADS)],
        axis=-1) * gate
    mix = (jnp.dot(oa_ref[...], wo_ref[:ATT_WIDTH, :], preferred_element_type=F32)
           + jnp.dot(hg.astype(BF16), wo_ref[ATT_WIDTH:, :], preferred_element_type=F32))
    x1 = x_ref[...] + mod[2:3] * mix
    x1_ref[...] = x1
    h2 = _rms(x1) * n2_ref[...] * (1.0 + mod[4:5]) + mod[3:4]
    h2_ref[...] = _pack_rows(h2)
    logits = jnp.dot(h2, wr_ref[...], preferred_element_type=F32,
                     precision=lax.Precision.HIGHEST)
    lg_ref[...] = logits.T


def _mixout(xt, o_att, o_fw, o_bw, proj, mod, hg_g, w_out, n2g, w_router, S, tm):
    T, D = xt.shape
    gate_col = proj.shape[1] // HG_WIDTH - 1
    row = lambda i: (i, 0)
    const = lambda i: (0, 0)
    return pl.pallas_call(
        _mixout_kernel,
        out_shape=(jax.ShapeDtypeStruct((T, D), F32),
                   jax.ShapeDtypeStruct((T, D // 2), I32),
                   jax.ShapeDtypeStruct((N_EXPERTS, T), F32)),
        grid=(T // tm,),
        in_specs=[pl.BlockSpec((tm, D), row),
                  pl.BlockSpec((tm, ATT_WIDTH), row),
                  pl.BlockSpec((tm, HG_WIDTH), row),
                  pl.BlockSpec((tm, HG_WIDTH), row),
                  pl.BlockSpec((tm, HG_WIDTH), lambda i: (i, gate_col)),
                  pl.BlockSpec((1, 6, D), lambda i: ((i * tm) // S, 0, 0)),
                  pl.BlockSpec((1, HEAD_W), const),
                  pl.BlockSpec(w_out.shape, const),
                  pl.BlockSpec((1, D), const),
                  pl.BlockSpec(w_router.shape, const)],
        out_specs=(pl.BlockSpec((tm, D), row),
                   pl.BlockSpec((tm, D // 2), row),
                   pl.BlockSpec((N_EXPERTS, tm), lambda i: (0, i))),
        compiler_params=_cparams(("parallel",)),
        name="mixout_norm2_router",
    )(xt, o_att, o_fw, o_bw, proj, mod, hg_g, w_out, n2g, w_router)


def _first_argmax(x, iota, size):
    mx = jnp.max(x, axis=0, keepdims=True)
    idx = jnp.min(jnp.where(x == mx, iota, size), axis=0, keepdims=True)
    return mx, idx


def _route_kernel(lg_ref, br_ref, tri_ref, e_ref, w_ref, r_ref, cnt_ref, carry_ref):
    i = pl.program_id(0)

    @pl.when(i == 0)
    def _init():
        carry_ref[...] = jnp.zeros(carry_ref.shape, F32)

    scores = jax.nn.sigmoid(lg_ref[...])
    biased = scores + br_ref[...]
    tm = scores.shape[1]
    giota = lax.broadcasted_iota(I32, (GROUP_SIZE, tm), 0)
    gs = []
    for g in range(N_GROUPS):
        blk = biased[g * GROUP_SIZE:(g + 1) * GROUP_SIZE]
        m1, i1 = _first_argmax(blk, giota, GROUP_SIZE)
        m2 = jnp.max(jnp.where(giota == i1, NEG_INF, blk), axis=0, keepdims=True)
        gs.append(m1 + m2)
    gsc = jnp.concatenate(gs, axis=0)
    gi = lax.broadcasted_iota(I32, (N_GROUPS, tm), 0)
    gsel = jnp.zeros((N_GROUPS, tm), jnp.bool_)
    for _ in range(TOPK_GROUPS):
        _, idx = _first_argmax(gsc, gi, N_GROUPS)
        hit = gi == idx
        gsel = jnp.logical_or(gsel, hit)
        gsc = jnp.where(hit, NEG_INF, gsc)
    masked = jnp.concatenate(
        [jnp.where(gsel[g:g + 1], biased[g * GROUP_SIZE:(g + 1) * GROUP_SIZE], NEG_INF)
         for g in range(N_GROUPS)], axis=0)
    ei = lax.broadcasted_iota(I32, (N_EXPERTS, tm), 0)
    eidx, wts = [], []
    onehot = jnp.zeros((N_EXPERTS, tm), F32)
    for _ in range(TOP_K):
        _, idx = _first_argmax(masked, ei, N_EXPERTS)
        hit = ei == idx
        eidx.append(idx)
        wts.append(jnp.sum(jnp.where(hit, scores, 0.0), axis=0, keepdims=True))
        onehot = jnp.where(hit, 1.0, onehot)
        masked = jnp.where(hit, NEG_INF, masked)
    w = jnp.concatenate(wts, axis=0)
    w = w / (jnp.sum(w, axis=0, keepdims=True) + 1e-20) * ROUTE_SCALE
    e_ref[...] = jnp.concatenate(eidx, axis=0)
    w_ref[...] = w
    before = jnp.dot(onehot.astype(BF16), tri_ref[...], preferred_element_type=F32)
    before = before + carry_ref[...]
    ranks = [jnp.sum(jnp.where(ei == idx, before, 0.0), axis=0, keepdims=True) for idx in eidx]
    r_ref[...] = jnp.concatenate(ranks, axis=0).astype(I32)
    carry = carry_ref[...] + jnp.sum(onehot, axis=1, keepdims=True)
    carry_ref[...] = carry
    cnt_ref[...] = carry.astype(I32)


def _route(logits_t, b_router, tm):
    E, T = logits_t.shape
    tri = (jnp.arange(tm)[:, None] < jnp.arange(tm)[None, :]).astype(BF16)
    tok = lambda i: (0, i)
    const = lambda i: (0, 0)
    return pl.pallas_call(
        _route_kernel,
        out_shape=(jax.ShapeDtypeStruct((TOP_K, T), I32),
                   jax.ShapeDtypeStruct((TOP_K, T), F32),
                   jax.ShapeDtypeStruct((TOP_K, T), I32),
                   jax.ShapeDtypeStruct((E, 1), I32)),
        grid=(T // tm,),
        in_specs=[pl.BlockSpec((E, tm), tok),
                  pl.BlockSpec((E, 1), const),
                  pl.BlockSpec((tm, tm), const)],
        out_specs=(pl.BlockSpec((TOP_K, tm), tok),
                   pl.BlockSpec((TOP_K, tm), tok),
                   pl.BlockSpec((TOP_K, tm), tok),
                   pl.BlockSpec((E, 1), const)),
        scratch_shapes=[pltpu.VMEM((E, 1), F32)],
        compiler_params=_cparams(("arbitrary",)),
        name="route_topk",
    )(logits_t, b_router, tri)


def _pos_kernel(e_ref, r_ref, ps_ref, o_ref):
    e = e_ref[...]
    tm = e.shape[1]
    ei = lax.broadcasted_iota(I32, (N_EXPERTS, tm), 0)
    ps = ps_ref[...]
    rows = [jnp.sum(jnp.where(ei == e[k:k + 1], ps, 0), axis=0, keepdims=True)
            for k in range(TOP_K)]
    o_ref[...] = jnp.concatenate(rows, axis=0) + r_ref[...]


def _positions(eidx, rank, pstart, tm):
    K, T = eidx.shape
    tok = lambda i: (0, i)
    return pl.pallas_call(
        _pos_kernel,
        out_shape=jax.ShapeDtypeStruct((K, T), I32),
        grid=(T // tm,),
        in_specs=[pl.BlockSpec((K, tm), tok), pl.BlockSpec((K, tm), tok),
                  pl.BlockSpec((N_EXPERTS, 1), lambda i: (0, 0))],
        out_specs=pl.BlockSpec((K, tm), tok),
        compiler_params=_cparams(("parallel",)),
        name="dispatch_positions",
    )(eidx, rank, pstart)


def _sc_mesh():
    return plsc.VectorSubcoreMesh(core_axis_name="c", subcore_axis_name="s")


def _worker_id():
    return lax.axis_index("s") * SC_CORES + lax.axis_index("c")


def _window_positions(pos):
    K, T = pos.shape
    nwin = T // (SC_WORKERS * SC_WINDOW)
    assert nwin * SC_WORKERS * SC_WINDOW == T, T
    return pos.reshape(K, SC_WORKERS, nwin, SC_WINDOW).transpose(1, 2, 0, 3)


def _sc_dispatch(h, pos, P):
    T, D = h.shape
    pos4 = _window_positions(pos)
    NW, nwin, K, W = pos4.shape

    @functools.partial(
        pl.kernel, mesh=_sc_mesh(),
        out_type=jax.ShapeDtypeStruct((P, D), h.dtype),
        scratch_types=[pltpu.VMEM((K, W), I32), pltpu.VMEM((W, D), h.dtype),
                       pltpu.SemaphoreType.DMA],
        name="sc_dispatch")
    def k(h_hbm, pos_hbm, xs_hbm, idx_v, rows_v, sem):
        wid = _worker_id()

        @pl.loop(0, nwin)
        def _(j):
            base = (wid * nwin + j) * W
            pltpu.sync_copy(pos_hbm.at[wid, j], idx_v)
            pltpu.sync_copy(h_hbm.at[pl.ds(base, W)], rows_v)
            copies = [pltpu.async_copy(rows_v, xs_hbm.at[idx_v.at[kk]], sem)
                      for kk in range(K)]
            for c in copies:
                c.wait()

    return k(h, pos4)


def _sc_gather(ys, pos):
    P, D = ys.shape
    T = pos.shape[1]
    pos4 = _window_positions(pos)
    NW, nwin, K, W = pos4.shape

    @functools.partial(
        pl.kernel, mesh=_sc_mesh(),
        out_type=jax.ShapeDtypeStruct((K, T, D), ys.dtype),
        scratch_types=[pltpu.VMEM((K, W), I32), pltpu.VMEM((W, D), ys.dtype),
                       pltpu.SemaphoreType.DMA],
        name="sc_gather")
    def k(ys_hbm, pos_hbm, yg_hbm, idx_v, rows_v, sem):
        wid = _worker_id()

        @pl.loop(0, nwin)
        def _(j):
            base = (wid * nwin + j) * W
            pltpu.sync_copy(pos_hbm.at[wid, j], idx_v)
            for kk in range(K):
                pltpu.async_copy(ys_hbm.at[idx_v.at[kk]], rows_v, sem).wait()
                pltpu.sync_copy(rows_v, yg_hbm.at[kk, pl.ds(base, W)])

    return k(ys, pos4)


def _expert_kernel(be_ref, nv_ref, x_ref, wg_ref, wu_ref, wd_ref, o_ref):
    i = pl.program_id(0)

    @pl.when(i < nv_ref[0])
    def _():
        lo, hi = _unpack_rows(x_ref[...])
        x = jnp.concatenate([lo, hi], axis=1).astype(BF16)
        a = (_silu(jnp.dot(x, wg_ref[0], preferred_element_type=F32))
             * jnp.dot(x, wu_ref[0], preferred_element_type=F32))
        o_ref[...] = _pack_rows(jnp.dot(a.astype(BF16), wd_ref[0], preferred_element_type=F32))


def _experts(xs, blk_e, nvalid, wg, wu, wd):
    P, Dh = xs.shape
    D = 2 * Dh
    bm = EXPERT_BLOCK
    nb = P // bm
    rowmap = lambda i, be, nv: (jnp.minimum(i, nv[0] - 1), 0)
    wmap = lambda i, be, nv: (be[i], 0, 0)
    grid_spec = pltpu.PrefetchScalarGridSpec(
        num_scalar_prefetch=2,
        grid=(nb,),
        in_specs=[pl.BlockSpec((bm, Dh), rowmap),
                  pl.BlockSpec((1, D, EXPERT_DIM), wmap),
                  pl.BlockSpec((1, D, EXPERT_DIM), wmap),
                  pl.BlockSpec((1, EXPERT_DIM, D), wmap)],
        out_specs=pl.BlockSpec((bm, Dh), rowmap),
    )
    return pl.pallas_call(
        _expert_kernel,
        out_shape=jax.ShapeDtypeStruct((P, Dh), I32),
        grid_spec=grid_spec,
        compiler_params=_cparams(("arbitrary",)),
        name="expert_ffn",
    )(blk_e, nvalid, xs, wg, wu, wd)


def _final_kernel(yg_ref, w_ref, h2_ref, x1_ref, mod_ref, modf_ref, nf_ref,
                  sg_ref, su_ref, sd_ref, o_ref):
    w = w_ref[...]
    y_lo = y_hi = None
    for k in range(TOP_K):
        lo, hi = _unpack_rows(yg_ref[k])
        wk = w[:, k:k + 1]
        y_lo = lo * wk if y_lo is None else y_lo + lo * wk
        y_hi = hi * wk if y_hi is None else y_hi + hi * wk
    y = jnp.concatenate([y_lo, y_hi], axis=1)
    hb = jnp.concatenate(_unpack_rows(h2_ref[...]), axis=1).astype(BF16)
    a = (_silu(jnp.dot(hb, sg_ref[...], preferred_element_type=F32))
         * jnp.dot(hb, su_ref[...], preferred_element_type=F32))
    shared = jnp.dot(a.astype(BF16), sd_ref[...], preferred_element_type=F32)
    mod = mod_ref[0]
    modf = modf_ref[0]
    x2 = x1_ref[...] + mod[5:6] * (y + shared)
    o_ref[...] = _rms(x2) * nf_ref[...] * (1.0 + modf[1:2]) + modf[0:1]


def _final(yg, wts_t, h2, x1, mod, modf, nfg, sg, su, sd, S, tm):
    T, D = x1.shape
    row = lambda i: (i, 0)
    const = lambda i: (0, 0)
    bat = lambda i: ((i * tm) // S, 0, 0)
    return pl.pallas_call(
        _final_kernel,
        out_shape=jax.ShapeDtypeStruct((T, D), F32),
        grid=(T // tm,),
        in_specs=[pl.BlockSpec((TOP_K, tm, D // 2), lambda i: (0, i, 0)),
                  pl.BlockSpec((tm, TOP_K), row),
                  pl.BlockSpec((tm, D // 2), row),
                  pl.BlockSpec((tm, D), row),
                  pl.BlockSpec((1, 6, D), bat),
                  pl.BlockSpec((1, 2, D), bat),
                  pl.BlockSpec((1, D), const),
                  pl.BlockSpec(sg.shape, const),
                  pl.BlockSpec(su.shape, const),
                  pl.BlockSpec(sd.shape, const)],
        out_specs=pl.BlockSpec((tm, D), row),
        compiler_params=_cparams(("parallel",)),
        name="combine_shared_final",
    )(yg, wts_t, h2, x1, mod, modf, nfg, sg, su, sd)


def _tile(n, pref):
    t = min(n, pref)
    assert n % t == 0, (n, pref)
    return t


def _moe(h2, logits_t, x1, mod, modf, p, S):
    T = h2.shape[0]
    tm_r = _tile(T, 512)
    eidx, wts, rank, counts = _route(logits_t, p["b_router"], tm_r)
    bm = EXPERT_BLOCK
    counts = counts[:, 0]
    padded = (counts + bm - 1) // bm * bm
    pends = jnp.cumsum(padded)
    pstart = (pends - padded).astype(I32)
    P = T * TOP_K + N_EXPERTS * bm
    nb = P // bm
    pos = _positions(eidx, rank, pstart[:, None], tm_r)
    blk_e = jnp.minimum(jnp.searchsorted(pends, jnp.arange(nb) * bm, side="right"),
                        N_EXPERTS - 1).astype(I32)
    nvalid = (pends[-1] // bm).astype(I32)
    blk_e = jnp.where(jnp.arange(nb) < nvalid, blk_e, blk_e[jnp.maximum(nvalid - 1, 0)])
    xs = _sc_dispatch(h2, pos, P)
    ys = _experts(xs, blk_e, nvalid[None], p["wg"], p["wu"], p["wd"])
    yg = _sc_gather(ys, pos)
    tm_f = _tile(S, 256)
    return _final(yg, wts.T, h2, x1, mod, modf, p["normf_g"], p["sg"], p["su"], p["sd"], S, tm_f)


NORM_SLACK = 1.02
REF_GAP_LIMIT = 100.0


def _diff_attention(qh, kaug, vh, nrm, posq, p, B, S):
    qmax = jnp.sqrt(jnp.max(nrm[:, 0, 0, :2 * ATT_HEADS])) * NORM_SLACK
    kmax = jnp.sqrt(jnp.max(nrm[:, 1, 0, :2 * ATT_HEADS])) * NORM_SLACK
    in_range = 2.0 * qmax * kmax <= REF_GAP_LIMIT
    scal = jnp.concatenate([-p["slopes"], kmax[None]]).astype(F32)
    tq = _tile(S, 1024)
    tk = 2 * tq if S >= 16 * tq else tq

    def single_pass(_):
        return _attention_ref(qh, kaug, vh, scal, p["lam_p"], p["subln_g"], posq, B, S, tq, tk)

    def online(_):
        return _attention(qh, kaug, vh, p["slopes"], p["lam_p"], p["subln_g"], B, S,
                          _tile(S, 256), _tile(S, 512))

    return lax.cond(in_range, single_pass, online, None)


def _trunk(x, mod, modf, p):
    B, S, D = x.shape
    T = B * S
    xt = x.reshape(T, D)
    tm = _tile(S, 256)
    posq, posk = _alibi_tables(p["slopes"], S)
    proj, qh, kaug, vh, nrm = _proj(xt, mod, p["norm1_g"], p["w_in"], posk, S, tm)
    o_att = _diff_attention(qh, kaug, vh, nrm, posq, p, B, S)
    o_fw, o_bw = _hgrn2(proj, p["lbs"], B, S)
    x1, h2, logits_t = _mixout(xt, o_att, o_fw, o_bw, proj, mod, p["hg_norm_g"], p["w_out"],
                               p["norm2_g"], p["w_router"], S, tm)
    out = _moe(h2, logits_t, x1, mod, modf, p, S)
    return out.reshape(B, S, D)


def kernel(x_prompt, x_sample, c_prompt, c_sample, w_ada, b_ada, norm1_g, w_in, lam_q1, lam_k1, lam_q2, lam_k2, subln_g, hg_lb_logits, hg_norm_g, w_out, norm2_g, w_router, b_router, w_exp_gate, w_exp_up, w_exp_down, w_sh_gate, w_sh_up, w_sh_down, w_ada_f, b_ada_f, normf_g):
    D = x_prompt.shape[-1]
    Bp, Bs = c_prompt.shape[0], c_sample.shape[0]
    c_all = jnp.concatenate([c_prompt, c_sample], axis=0)
    R = -(-c_all.shape[0] // 8) * 8
    c_all = jnp.pad(c_all, ((0, R - c_all.shape[0]), (0, 0)))
    w_all = jnp.concatenate([w_ada[0], w_ada_f], axis=1)
    b_all = jnp.concatenate([b_ada[0], b_ada_f], axis=0)[None]
    mod_all = _ada(c_all, w_all, b_all)
    mod6 = mod_all[:, :6 * D].reshape(R, 6, D)
    mod2 = mod_all[:, 6 * D:].reshape(R, 2, D)

    lbs = jax.nn.softmax(hg_lb_logits.astype(F32), axis=0)[0]
    lbs = lbs.reshape(2, HG_HEADS, HEAD_W).transpose(1, 0, 2)
    slopes = (jnp.exp2(-8.0 * (jnp.arange(ATT_HEADS, dtype=F32) + 1.0) / ATT_HEADS) * LOG2E)
    p = dict(
        norm1_g=norm1_g[0][None], w_in=w_in[0].astype(BF16),
        slopes=slopes.astype(F32),
        lam_p=jnp.stack([lam_q1[0], lam_k1[0], lam_q2[0], lam_k2[0]]).astype(F32),
        subln_g=subln_g[0][:, None], lbs=lbs, hg_norm_g=hg_norm_g[0][None],
        w_out=w_out[0].astype(BF16), norm2_g=norm2_g[0][None],
        w_router=w_router[0], b_router=b_router[0][:, None],
        wg=w_exp_gate[0].astype(BF16), wu=w_exp_up[0].astype(BF16),
        wd=w_exp_down[0].astype(BF16),
        sg=w_sh_gate[0].astype(BF16), su=w_sh_up[0].astype(BF16), sd=w_sh_down[0].astype(BF16),
        normf_g=normf_g[None],
    )
    y_prompt = _trunk(x_prompt, mod6[:Bp], mod2[:Bp], p)
    y_sample = _trunk(x_sample, mod6[Bp:Bp + Bs], mod2[Bp:Bp + Bs], p)
    return (y_prompt, y_sample)
```

```python
import functools
import math

import jax
import jax.numpy as jnp
from jax import lax
from jax.experimental import pallas as pl
from jax.experimental.pallas import tpu as pltpu
from jax.experimental.pallas import tpu_sc as plsc

F32 = jnp.float32
BF16 = jnp.bfloat16
I32 = jnp.int32

EPS = 1e-6
LOG2E = 1.4426950408889634

ATT_HEADS = 4
ATT_QK_DIM = 64
HEAD_W = 128
ATT_WIDTH = ATT_HEADS * HEAD_W
HG_HEADS = 4
HG_WIDTH = HG_HEADS * HEAD_W
N_EXPERTS = 256
TOP_K = 8
N_GROUPS = 8
TOPK_GROUPS = 4
GROUP_SIZE = N_EXPERTS // N_GROUPS
EXPERT_DIM = 256
ROUTE_SCALE = 2.5
LAM_INIT = 0.8 - 0.6 * math.exp(-0.3 * 0)

HG_CHUNK = 128
HG_SUB = 16
HG_MILD_LOG_GATE = 3.75
HG_STEP_CHUNKS = 4
EXPERT_BLOCK = 512
SC_CORES = 2
SC_SUBCORES = 16
SC_WORKERS = SC_CORES * SC_SUBCORES
SC_WINDOW = 32
VMEM_LIMIT = 56 * 1024 * 1024

NEG_INF = float("-inf")


def _cparams(sem):
    return pltpu.CompilerParams(dimension_semantics=sem, vmem_limit_bytes=VMEM_LIMIT)


def _silu(x):
    return x * jax.nn.sigmoid(x)


def _pack_rows(x):
    half = x.shape[1] // 2
    bits = lax.bitcast_convert_type(x.astype(BF16).astype(F32), jnp.uint32)
    return lax.bitcast_convert_type((bits[:, :half] >> 16) | bits[:, half:], I32)


def _unpack_rows(w):
    u = lax.bitcast_convert_type(w, jnp.uint32)
    lo = lax.bitcast_convert_type(u << 16, F32)
    hi = lax.bitcast_convert_type(u & jnp.uint32(0xFFFF0000), F32)
    return lo, hi


def _ada_kernel(c_ref, w_ref, b_ref, o_ref):
    cs = _silu(c_ref[...])
    o_ref[...] = jnp.dot(cs, w_ref[...], preferred_element_type=F32,
                         precision=lax.Precision.HIGHEST) + b_ref[...]


def _ada(c, w, b):
    R, D = c.shape
    N = w.shape[1]
    tn = 1024
    return pl.pallas_call(
        _ada_kernel,
        out_shape=jax.ShapeDtypeStruct((R, N), F32),
        grid=(N // tn,),
        in_specs=[pl.BlockSpec((R, D), lambda j: (0, 0)),
                  pl.BlockSpec((D, tn), lambda j: (0, j)),
                  pl.BlockSpec((1, tn), lambda j: (0, j))],
        out_specs=pl.BlockSpec((R, tn), lambda j: (0, j)),
        compiler_params=_cparams(("arbitrary",)),
        name="ada_mod",
    )(c, w, b)


def _rms(x):
    return x * lax.rsqrt(jnp.mean(x * x, axis=-1, keepdims=True) + EPS)


def _proj_kernel(x_ref, mod_ref, g_ref, w_ref, posk_ref, grp_ref,
                 o_ref, q_ref, ka_ref, v_ref, nrm_ref, *, qscale):
    x = x_ref[...]
    mod = mod_ref[0]
    h = _rms(x) * g_ref[...] * (1.0 + mod[1:2]) + mod[0:1]
    acc = jnp.dot(h.astype(BF16), w_ref[...], preferred_element_type=F32)
    qw = ATT_WIDTH
    qb = (acc[:, :qw] * qscale).astype(BF16)
    kb = acc[:, qw:2 * qw].astype(BF16)
    vb = acc[:, 2 * qw:3 * qw].astype(BF16)
    o_ref[...] = acc[:, 3 * qw:].astype(BF16)
    for hh in range(ATT_HEADS):
        cols = slice(hh * HEAD_W, (hh + 1) * HEAD_W)
        q_ref[hh] = qb[:, cols]
        v_ref[hh] = vb[:, cols]
        ka_ref[hh, :, :HEAD_W] = kb[:, cols]
        ka_ref[hh, :, HEAD_W:] = posk_ref[:, cols]
    for idx, t in enumerate((qb, kb)):
        tf = t.astype(F32)
        n2 = jnp.dot((tf * tf).astype(BF16), grp_ref[...], preferred_element_type=F32)
        nrm_ref[0, idx] = jnp.broadcast_to(jnp.max(n2, axis=0, keepdims=True), (8, HEAD_W))


def _proj(xt, mod, g, w, posk, S, tm):
    T, D = xt.shape
    N = w.shape[1]
    qscale = ATT_QK_DIM ** -0.5 * LOG2E
    nblk = T // tm
    spb = S // tm
    grp = (jnp.arange(ATT_WIDTH)[:, None] // ATT_QK_DIM == jnp.arange(HEAD_W)[None, :]).astype(BF16)
    return pl.pallas_call(
        functools.partial(_proj_kernel, qscale=qscale),
        out_shape=(jax.ShapeDtypeStruct((T, N - 3 * ATT_WIDTH), BF16),
                   jax.ShapeDtypeStruct((ATT_HEADS, T, HEAD_W), BF16),
                   jax.ShapeDtypeStruct((ATT_HEADS, T, 2 * HEAD_W), BF16),
                   jax.ShapeDtypeStruct((ATT_HEADS, T, HEAD_W), BF16),
                   jax.ShapeDtypeStruct((nblk, 2, 8, HEAD_W), F32)),
        grid=(nblk,),
        in_specs=[pl.BlockSpec((tm, D), lambda i: (i, 0)),
                  pl.BlockSpec((1, 6, D), lambda i: ((i * tm) // S, 0, 0)),
                  pl.BlockSpec((1, D), lambda i: (0, 0)),
                  pl.BlockSpec((D, N), lambda i: (0, 0)),
                  pl.BlockSpec((tm, ATT_WIDTH), lambda i: (i % spb, 0)),
                  pl.BlockSpec((ATT_WIDTH, HEAD_W), lambda i: (0, 0))],
        out_specs=(pl.BlockSpec((tm, N - 3 * ATT_WIDTH), lambda i: (i, 0)),
                   pl.BlockSpec((ATT_HEADS, tm, HEAD_W), lambda i: (0, i, 0)),
                   pl.BlockSpec((ATT_HEADS, tm, 2 * HEAD_W), lambda i: (0, i, 0)),
                   pl.BlockSpec((ATT_HEADS, tm, HEAD_W), lambda i: (0, i, 0)),
                   pl.BlockSpec((1, 2, 8, HEAD_W), lambda i: (i, 0, 0, 0))),
        compiler_params=_cparams(("parallel",)),
        name="norm1_proj",
    )(xt, mod, g, w, posk, grp)


def _split3(x):
    hi = x.astype(BF16)
    r = x - hi.astype(F32)
    mid = r.astype(BF16)
    lo = (r - mid.astype(F32)).astype(BF16)
    return hi, mid, lo


def _alibi_tables(slopes2, S):
    H = slopes2.shape[0]
    a = slopes2[:, None] * jnp.arange(S, dtype=F32)[None, :]
    ah, am, al = _split3(a)
    one = jnp.ones((H, S), BF16)
    zero = jnp.zeros((H, S), BF16)
    pad = jnp.zeros((H, S, HEAD_W - 9), BF16)
    posk = jnp.concatenate([jnp.stack([one] * 6 + [ah, am, al], axis=-1), pad], axis=-1)
    posq = jnp.concatenate([jnp.stack([zero] * 3 + [-ah, -am, -al] + [one] * 3, axis=-1), pad], axis=-1)
    return posq, posk.transpose(1, 0, 2).reshape(S, H * HEAD_W)


def _attn_kernel(slope_ref, lam_ref, q_ref, k_ref, v_ref, g_ref, o_ref,
                 qm_ref, m_ref, l_ref, acc_ref, *, tq, tk, nk):
    h = pl.program_id(1)
    qi = pl.program_id(2)
    ki = pl.program_id(3)

    @pl.when(ki == 0)
    def _init():
        q = q_ref[...]
        lane = lax.broadcasted_iota(I32, q.shape, 1)
        zero = jnp.zeros_like(q)
        qm_ref[0] = jnp.where(lane < ATT_QK_DIM, q, zero)
        qm_ref[1] = jnp.where(lane >= ATT_QK_DIM, q, zero)
        m_ref[...] = jnp.full(m_ref.shape, NEG_INF, F32)
        l_ref[...] = jnp.zeros(l_ref.shape, F32)
        acc_ref[...] = jnp.zeros(acc_ref.shape, F32)

    slope = slope_ref[h]
    k = k_ref[...]
    v = v_ref[...]
    row = lax.broadcasted_iota(I32, (tk, tq), 0)
    col = lax.broadcasted_iota(I32, (tk, tq), 1)
    dist = (col - row + (qi * tq - ki * tk)).astype(F32)
    bias = jnp.abs(dist) * (-slope)
    for m in range(2):
        s = lax.dot_general(k, qm_ref[m], (((1,), (1,)), ((), ())),
                            preferred_element_type=F32) + bias
        m_old = m_ref[m]
        m_new = jnp.maximum(m_old, jnp.max(s, axis=0, keepdims=True))
        alpha = jnp.exp2(m_old - m_new)
        p = jnp.exp2(s - m_new)
        l_ref[m] = alpha * l_ref[m] + jnp.sum(p, axis=0, keepdims=True)
        pv = lax.dot_general(v, p.astype(BF16), (((0,), (0,)), ((), ())),
                             preferred_element_type=F32)
        acc_ref[m] = alpha * acc_ref[m] + pv
        m_ref[m] = m_new

    @pl.when(ki == nk - 1)
    def _fin():
        lp = lam_ref[...]
        lam = (jnp.exp(jnp.sum(lp[0:1] * lp[1:2], axis=-1, keepdims=True))
               - jnp.exp(jnp.sum(lp[2:3] * lp[3:4], axis=-1, keepdims=True))
               + LAM_INIT)
        o = acc_ref[0] / l_ref[0] - lam * (acc_ref[1] / l_ref[1])
        ms = jnp.mean(o * o, axis=0, keepdims=True)
        y = o * lax.rsqrt(ms + EPS) * g_ref[...] * (1.0 - LAM_INIT)
        o_ref[...] = y.T.astype(o_ref.dtype)


def _attention(qh, kaug, vh, slopes, lam_p, subln_g, B, S, tq, tk):
    T = B * S
    nq, nk = S // tq, S // tk
    grid_spec = pltpu.PrefetchScalarGridSpec(
        num_scalar_prefetch=1,
        grid=(B, ATT_HEADS, nq, nk),
        in_specs=[
            pl.BlockSpec((4, ATT_QK_DIM), lambda b, h, qi, ki, s: (0, 0)),
            pl.BlockSpec((None, tq, HEAD_W), lambda b, h, qi, ki, s: (h, b * nq + qi, 0)),
            pl.BlockSpec((None, tk, HEAD_W), lambda b, h, qi, ki, s: (h, b * nk + ki, 0)),
            pl.BlockSpec((None, tk, HEAD_W), lambda b, h, qi, ki, s: (h, b * nk + ki, 0)),
            pl.BlockSpec((HEAD_W, 1), lambda b, h, qi, ki, s: (0, 0)),
        ],
        out_specs=pl.BlockSpec((tq, HEAD_W), lambda b, h, qi, ki, s: (b * nq + qi, h)),
        scratch_shapes=[pltpu.VMEM((2, tq, HEAD_W), BF16),
                        pltpu.VMEM((2, 1, tq), F32),
                        pltpu.VMEM((2, 1, tq), F32),
                        pltpu.VMEM((2, HEAD_W, tq), F32)],
    )
    return pl.pallas_call(
        functools.partial(_attn_kernel, tq=tq, tk=tk, nk=nk),
        out_shape=jax.ShapeDtypeStruct((T, ATT_WIDTH), BF16),
        grid_spec=grid_spec,
        compiler_params=_cparams(("parallel", "parallel", "parallel", "arbitrary")),
        name="diff_attention",
    )(slopes, lam_p, qh, kaug, vh, subln_g)


def _attn_finish(lam_ref, g_ref, o_ref, l_ref, acc_ref):
    lp = lam_ref[...]
    lam = (jnp.exp(jnp.sum(lp[0:1] * lp[1:2], axis=-1, keepdims=True))
           - jnp.exp(jnp.sum(lp[2:3] * lp[3:4], axis=-1, keepdims=True))
           + LAM_INIT)
    o = acc_ref[0] / l_ref[0] - lam * (acc_ref[1] / l_ref[1])
    ms = jnp.mean(o * o, axis=0, keepdims=True)
    y = o * lax.rsqrt(ms + EPS) * g_ref[...] * (1.0 - LAM_INIT)
    o_ref[...] = y.T.astype(o_ref.dtype)


def _attn_ref_kernel(sc_ref, lam_ref, q_ref, ka_ref, v_ref, g_ref, posq_ref, o_ref,
                     qa_ref, l_ref, acc_ref, *, tq, tk, nk):
    h = pl.program_id(1)
    qi = pl.program_id(2)
    ki = pl.program_id(3)

    @pl.when(ki == 0)
    def _init():
        q = q_ref[...]
        lane = lax.broadcasted_iota(I32, q.shape, 1)
        zero = jnp.zeros_like(q)
        kmax = sc_ref[ATT_HEADS]
        pa = posq_ref[...]
        for m in range(2):
            sel = (lane < ATT_QK_DIM) if m == 0 else (lane >= ATT_QK_DIM)
            qm = jnp.where(sel, q, zero)
            qf = qm.astype(F32)
            ub = jnp.sqrt(jnp.sum(qf * qf, axis=1, keepdims=True)) * kmax
            uh, um, ul = (t.astype(F32) for t in _split3(jnp.broadcast_to(ub, q.shape)))
            ubp = jnp.where(lane == 0, -uh, jnp.where(lane == 1, -um, jnp.where(lane == 2, -ul, 0.0)))
            ubp = ubp.astype(BF16)
            for var, pp in enumerate((pa + ubp, ubp - pa, ubp)):
                qa_ref[2 * var + m] = jnp.concatenate([qm, pp], axis=1)
        l_ref[...] = jnp.zeros(l_ref.shape, F32)
        acc_ref[...] = jnp.zeros(acc_ref.shape, F32)

    rel = qi * tq - ki * tk
    near = jnp.logical_and(rel > -tq, rel < tk)

    def body(with_bias):
        ka = ka_ref[...]
        if with_bias:
            row = lax.broadcasted_iota(I32, (tk, tq), 0)
            col = lax.broadcasted_iota(I32, (tk, tq), 1)
            bias = jnp.abs((col - row + rel).astype(F32)) * sc_ref[h]
            var = 2
        else:
            var = jnp.where(rel > 0, 0, 1)
        ss = [lax.dot_general(ka, qa_ref[2 * var + m], (((1,), (1,)), ((), ())),
                              preferred_element_type=F32) for m in range(2)]
        v = v_ref[...]
        for m in range(2):
            p = jnp.exp2(ss[m] + bias if with_bias else ss[m])
            l_ref[m] += jnp.sum(p, axis=0, keepdims=True)
            acc_ref[m] += lax.dot_general(v, p.astype(BF16), (((0,), (0,)), ((), ())),
                                          preferred_element_type=F32)

    @pl.when(near)
    def _():
        body(True)

    @pl.when(jnp.logical_not(near))
    def _():
        body(False)

    @pl.when(ki == nk - 1)
    def _fin():
        _attn_finish(lam_ref, g_ref, o_ref, l_ref, acc_ref)


def _attention_ref(qh, kaug, vh, scal, lam_p, subln_g, posq, B, S, tq, tk):
    T = B * S
    nq, nk = S // tq, S // tk
    grid_spec = pltpu.PrefetchScalarGridSpec(
        num_scalar_prefetch=1,
        grid=(B, ATT_HEADS, nq, nk),
        in_specs=[
            pl.BlockSpec((4, ATT_QK_DIM), lambda b, h, qi, ki, s: (0, 0)),
            pl.BlockSpec((None, tq, HEAD_W), lambda b, h, qi, ki, s: (h, b * nq + qi, 0)),
            pl.BlockSpec((None, tk, 2 * HEAD_W), lambda b, h, qi, ki, s: (h, b * nk + ki, 0)),
            pl.BlockSpec((None, tk, HEAD_W), lambda b, h, qi, ki, s: (h, b * nk + ki, 0)),
            pl.BlockSpec((HEAD_W, 1), lambda b, h, qi, ki, s: (0, 0)),
            pl.BlockSpec((None, tq, HEAD_W), lambda b, h, qi, ki, s: (h, qi, 0)),
        ],
        out_specs=pl.BlockSpec((tq, HEAD_W), lambda b, h, qi, ki, s: (b * nq + qi, h)),
        scratch_shapes=[pltpu.VMEM((6, tq, 2 * HEAD_W), BF16),
                        pltpu.VMEM((2, 1, tq), F32),
                        pltpu.VMEM((2, HEAD_W, tq), F32)],
    )
    return pl.pallas_call(
        functools.partial(_attn_ref_kernel, tq=tq, tk=tk, nk=nk),
        out_shape=jax.ShapeDtypeStruct((T, ATT_WIDTH), BF16),
        grid_spec=grid_spec,
        compiler_params=_cparams(("parallel", "parallel", "parallel", "arbitrary")),
        name="diff_attention_ref",
    )(scal, lam_p, qh, kaug, vh, subln_g, posq)


def _hg_chunk(q, kk, v, g, st, reverse):
    C = q.shape[0]
    nsub = C // HG_SUB
    row = lax.broadcasted_iota(I32, (C, HEAD_W), 0)
    b = _hg_cumsum(g, reverse)
    b_end = b[C - 1:C] if not reverse else b[0:1]

    qd = (q * jnp.exp(b)).astype(BF16)
    o_inter = lax.dot_general(qd, st.astype(BF16), (((1,), (1,)), ((), ())),
                              preferred_element_type=F32)

    ones = jnp.ones((HEAD_W, HEAD_W), BF16)
    srow = lax.broadcasted_iota(I32, (HG_SUB, HEAD_W), 0)
    vb = v.astype(BF16)
    outs = []
    for blk in range(nsub):
        lo, hi = blk * HG_SUB, (blk + 1) * HG_SUB
        qI, kI, vI, bI = q[lo:hi], kk[lo:hi], v[lo:hi], b[lo:hi]
        pieces = []
        for t in range(HG_SUB):
            dl = bI[t:t + 1] - bI
            keep = (srow <= t) if not reverse else (srow >= t)
            e = jnp.exp(jnp.where(keep, dl, NEG_INF))
            pieces.append(qI[t:t + 1] * kI * e)
        wst = jnp.concatenate(pieces, axis=0).astype(BF16)
        rsum = jnp.dot(wst, ones, preferred_element_type=F32)
        o_blk = jnp.sum(rsum.reshape(HG_SUB, HG_SUB, HEAD_W) * vI[None], axis=1)
        has_off = (blk > 0) if not reverse else (blk < nsub - 1)
        if has_off:
            if not reverse:
                r = b[lo - 1:lo]
                kmask = row < lo
            else:
                r = b[hi:hi + 1]
                kmask = row >= hi
            qs = (qI * jnp.exp(bI - r)).astype(BF16)
            ks = (kk * jnp.exp(jnp.where(kmask, r - b, NEG_INF))).astype(BF16)
            a = lax.dot_general(qs, ks, (((1,), (1,)), ((), ())),
                                preferred_element_type=F32)
            o_blk = o_blk + jnp.dot(a.astype(BF16), vb, preferred_element_type=F32)
        outs.append(o_blk)
    o = o_inter + jnp.concatenate(outs, axis=0)

    kd = (kk * jnp.exp(b_end - b)).astype(BF16)
    upd = lax.dot_general(vb, kd, (((0,), (0,)), ((), ())),
                          preferred_element_type=F32)
    st_new = st * jnp.exp(b_end) + upd
    return o, st_new


def _hg_cumsum(g, reverse):
    C = g.shape[0]
    row = lax.broadcasted_iota(I32, g.shape, 0)
    b = g
    d = 1
    while d < C:
        if not reverse:
            b = b + jnp.where(row >= d, pltpu.roll(b, d, axis=0), 0.0)
        else:
            b = b + jnp.where(row < C - d, pltpu.roll(b, C - d, axis=0), 0.0)
        d *= 2
    return b


def _hg_chunk_mild(q, kk, v, g, st, reverse):
    C = q.shape[0]
    nsub = C // HG_SUB
    row = lax.broadcasted_iota(I32, (C, HEAD_W), 0)
    b = _hg_cumsum(g, reverse)
    b_end = b[C - 1:C] if not reverse else b[0:1]
    qd = (q * jnp.exp(b)).astype(BF16)
    o_inter = lax.dot_general(qd, st.astype(BF16), (((1,), (1,)), ((), ())),
                              preferred_element_type=F32)
    vb = v.astype(BF16)
    qrow = lax.broadcasted_iota(I32, (HG_SUB, C), 0)
    kcol = lax.broadcasted_iota(I32, (HG_SUB, C), 1)
    zero_ref = jnp.zeros((1, HEAD_W), F32)
    a_rows = []
    for blk in range(nsub):
        lo, hi = blk * HG_SUB, (blk + 1) * HG_SUB
        if not reverse:
            r = b[lo - 1:lo] if blk > 0 else zero_ref
            kmask = row < hi
            amask = kcol <= qrow + lo
        else:
            r = b[hi:hi + 1] if blk < nsub - 1 else zero_ref
            kmask = row >= lo
            amask = kcol >= qrow + lo
        qs = (q[lo:hi] * jnp.exp(b[lo:hi] - r)).astype(BF16)
        ks = (kk * jnp.exp(jnp.where(kmask, r - b, NEG_INF))).astype(BF16)
        a = lax.dot_general(qs, ks, (((1,), (1,)), ((), ())), preferred_element_type=F32)
        a_rows.append(jnp.where(amask, a, 0.0))
    a_full = jnp.concatenate(a_rows, axis=0).astype(BF16)
    o = o_inter + jnp.dot(a_full, vb, preferred_element_type=F32)
    kd = (kk * jnp.exp(b_end - b)).astype(BF16)
    upd = lax.dot_general(vb, kd, (((0,), (0,)), ((), ())), preferred_element_type=F32)
    return o, st * jnp.exp(b_end) + upd


def _hg_kernel(qf_ref, ff_ref, if_ref, qb_ref, fb_ref, ib_ref, lb_ref, thr_ref,
               of_ref, ob_ref, sf_ref, sb_ref, *, nchunk):
    j = pl.program_id(2)

    @pl.when(j == 0)
    def _init():
        sf_ref[...] = jnp.zeros(sf_ref.shape, F32)
        sb_ref[...] = jnp.zeros(sb_ref.shape, F32)

    lb = lb_ref[0]
    C = HG_CHUNK

    def prep(q_ref, f_ref, i_ref, lbd, rows):
        q = _silu(q_ref[rows, :].astype(F32))
        f = lbd + (1.0 - lbd) * jax.nn.sigmoid(f_ref[rows, :].astype(F32))
        return q, 1.0 - f, i_ref[rows, :].astype(F32), jnp.log(f)

    def step(chunk_fn, c, carry):
        rows_f = pl.ds(pl.multiple_of(c * C, C), C)
        rows_b = pl.ds(pl.multiple_of((nchunk - 1 - c) * C, C), C)
        o, st = chunk_fn(*prep(qf_ref, ff_ref, if_ref, lb[0:1], rows_f), sf_ref[...], reverse=False)
        of_ref[rows_f, :] = o
        sf_ref[...] = st
        o, st = chunk_fn(*prep(qb_ref, fb_ref, ib_ref, lb[1:2], rows_b), sb_ref[...], reverse=True)
        ob_ref[rows_b, :] = o
        sb_ref[...] = st
        return carry

    thr = thr_ref[0]
    above = jnp.minimum(
        jnp.min(jnp.where(ff_ref[...].astype(F32) >= thr[0:1], 1.0, 0.0)),
        jnp.min(jnp.where(fb_ref[...].astype(F32) >= thr[1:2], 1.0, 0.0)))
    mild = above > 0.5

    @pl.when(mild)
    def _():
        lax.fori_loop(0, nchunk, functools.partial(step, _hg_chunk_mild), 0)

    @pl.when(jnp.logical_not(mild))
    def _():
        lax.fori_loop(0, nchunk, functools.partial(step, _hg_chunk), 0)


def _hgrn2(proj, lbs, B, S):
    T = B * S
    nchunk = min(HG_STEP_CHUNKS, S // HG_CHUNK)
    C = HG_CHUNK * nchunk
    n = S // C
    nh = HG_HEADS
    qc, ffc, fbc, ic = 0, nh, 2 * nh, 3 * nh

    def fw(col):
        return pl.BlockSpec((C, HEAD_W), lambda b, h, j: (b * n + j, col + h))

    def bw(col):
        return pl.BlockSpec((C, HEAD_W), lambda b, h, j: (b * n + n - 1 - j, col + h))

    out_f = pl.BlockSpec((C, HEAD_W), lambda b, h, j: (b * n + j, h))
    out_b = pl.BlockSpec((C, HEAD_W), lambda b, h, j: (b * n + n - 1 - j, h))
    need = (math.exp(-HG_MILD_LOG_GATE) - lbs) / (1.0 - lbs)
    thr = jnp.where(need > 0.0, jnp.log(jnp.maximum(need, 1e-30) / (1.0 - need)), NEG_INF).astype(F32)
    per_head = pl.BlockSpec((1, 2, HEAD_W), lambda b, h, j: (h, 0, 0))
    return pl.pallas_call(
        functools.partial(_hg_kernel, nchunk=nchunk),
        out_shape=(jax.ShapeDtypeStruct((T, HG_WIDTH), F32),
                   jax.ShapeDtypeStruct((T, HG_WIDTH), F32)),
        grid=(B, nh, n),
        in_specs=[fw(qc), fw(ffc), fw(ic), bw(qc), bw(fbc), bw(ic), per_head, per_head],
        out_specs=(out_f, out_b),
        scratch_shapes=[pltpu.VMEM((HEAD_W, HEAD_W), F32),
                        pltpu.VMEM((HEAD_W, HEAD_W), F32)],
        compiler_params=_cparams(("parallel", "parallel", "arbitrary")),
        name="hgrn2_scan",
    )(proj, proj, proj, proj, proj, proj, lbs, thr)


def _mixout_kernel(x_ref, oa_ref, of_ref, ob_ref, gate_ref, mod_ref, hgg_ref, wo_ref,
                   n2_ref, wr_ref, x1_ref, h2_ref, lg_ref):
    mod = mod_ref[0]
    o = of_ref[...] + ob_ref[...]
    gate = _silu(gate_ref[...].astype(F32))
    hg = jnp.concatenate(
        [_rms(o[:, h * HEAD_W:(h + 1) * HEAD_W]) * hgg_ref[...] for h in range(HG_HEADS)],
        axis=-1) * gate
    mix = (jnp.dot(oa_ref[...], wo_ref[:ATT_WIDTH, :], preferred_element_type=F32)
           + jnp.dot(hg.astype(BF16), wo_ref[ATT_WIDTH:, :], preferred_element_type=F32))
    x1 = x_ref[...] + mod[2:3] * mix
    x1_ref[...] = x1
    h2 = _rms(x1) * n2_ref[...] * (1.0 + mod[4:5]) + mod[3:4]
    h2_ref[...] = _pack_rows(h2)
    h_hi = h2.astype(BF16)
    h_mid = (h2 - h_hi.astype(F32)).astype(BF16)
    two = jnp.dot(h_hi, wr_ref[...], preferred_element_type=F32)
    logits = (two[:, :N_EXPERTS] + two[:, N_EXPERTS:]
              + jnp.dot(h_mid, wr_ref[:, :N_EXPERTS], preferred_element_type=F32))
    lg_ref[...] = logits.T


def _mixout(xt, o_att, o_fw, o_bw, proj, mod, hg_g, w_out, n2g, w_router, S, tm):
    T, D = xt.shape
    gate_col = proj.shape[1] // HG_WIDTH - 1
    row = lambda i: (i, 0)
    const = lambda i: (0, 0)
    return pl.pallas_call(
        _mixout_kernel,
        out_shape=(jax.ShapeDtypeStruct((T, D), F32),
                   jax.ShapeDtypeStruct((T, D // 2), I32),
                   jax.ShapeDtypeStruct((N_EXPERTS, T), F32)),
        grid=(T // tm,),
        in_specs=[pl.BlockSpec((tm, D), row),
                  pl.BlockSpec((tm, ATT_WIDTH), row),
                  pl.BlockSpec((tm, HG_WIDTH), row),
                  pl.BlockSpec((tm, HG_WIDTH), row),
                  pl.BlockSpec((tm, HG_WIDTH), lambda i: (i, gate_col)),
                  pl.BlockSpec((1, 6, D), lambda i: ((i * tm) // S, 0, 0)),
                  pl.BlockSpec((1, HEAD_W), const),
                  pl.BlockSpec(w_out.shape, const),
                  pl.BlockSpec((1, D), const),
                  pl.BlockSpec(w_router.shape, const)],
        out_specs=(pl.BlockSpec((tm, D), row),
                   pl.BlockSpec((tm, D // 2), row),
                   pl.BlockSpec((N_EXPERTS, tm), lambda i: (0, i))),
        compiler_params=_cparams(("parallel",)),
        name="mixout_norm2_router",
    )(xt, o_att, o_fw, o_bw, proj, mod, hg_g, w_out, n2g, w_router)


def _first_argmax(x, iota, size):
    mx = jnp.max(x, axis=0, keepdims=True)
    idx = jnp.min(jnp.where(x == mx, iota, size), axis=0, keepdims=True)
    return mx, idx


def _route_kernel(lg_ref, br_ref, tri_ref, e_ref, w_ref, r_ref, cnt_ref, carry_ref):
    i = pl.program_id(0)

    @pl.when(i == 0)
    def _init():
        carry_ref[...] = jnp.zeros(carry_ref.shape, F32)

    scores = jax.nn.sigmoid(lg_ref[...])
    biased = scores + br_ref[...]
    tm = scores.shape[1]
    giota = lax.broadcasted_iota(I32, (GROUP_SIZE, tm), 0)
    gs = []
    for g in range(N_GROUPS):
        blk = biased[g * GROUP_SIZE:(g + 1) * GROUP_SIZE]
        m1, i1 = _first_argmax(blk, giota, GROUP_SIZE)
        m2 = jnp.max(jnp.where(giota == i1, NEG_INF, blk), axis=0, keepdims=True)
        gs.append(m1 + m2)
    gsc = jnp.concatenate(gs, axis=0)
    gi = lax.broadcasted_iota(I32, (N_GROUPS, tm), 0)
    gsel = jnp.zeros((N_GROUPS, tm), jnp.bool_)
    for _ in range(TOPK_GROUPS):
        _, idx = _first_argmax(gsc, gi, N_GROUPS)
        hit = gi == idx
        gsel = jnp.logical_or(gsel, hit)
        gsc = jnp.where(hit, NEG_INF, gsc)
    masked = jnp.concatenate(
        [jnp.where(gsel[g:g + 1], biased[g * GROUP_SIZE:(g + 1) * GROUP_SIZE], NEG_INF)
         for g in range(N_GROUPS)], axis=0)
    ei = lax.broadcasted_iota(I32, (N_EXPERTS, tm), 0)
    eidx, wts = [], []
    onehot = jnp.zeros((N_EXPERTS, tm), F32)
    for _ in range(TOP_K):
        _, idx = _first_argmax(masked, ei, N_EXPERTS)
        hit = ei == idx
        eidx.append(idx)
        wts.append(jnp.sum(jnp.where(hit, scores, 0.0), axis=0, keepdims=True))
        onehot = jnp.where(hit, 1.0, onehot)
        masked = jnp.where(hit, NEG_INF, masked)
    w = jnp.concatenate(wts, axis=0)
    w = w / (jnp.sum(w, axis=0, keepdims=True) + 1e-20) * ROUTE_SCALE
    e_ref[...] = jnp.concatenate(eidx, axis=0)
    w_ref[...] = w
    before = jnp.dot(onehot.astype(BF16), tri_ref[...], preferred_element_type=F32)
    before = before + carry_ref[...]
    ranks = [jnp.sum(jnp.where(ei == idx, before, 0.0), axis=0, keepdims=True) for idx in eidx]
    r_ref[...] = jnp.concatenate(ranks, axis=0).astype(I32)
    carry = carry_ref[...] + jnp.sum(onehot, axis=1, keepdims=True)
    carry_ref[...] = carry
    cnt_ref[...] = carry.astype(I32)


def _route(logits_t, b_router, tm):
    E, T = logits_t.shape
    tri = (jnp.arange(tm)[:, None] < jnp.arange(tm)[None, :]).astype(BF16)
    tok = lambda i: (0, i)
    const = lambda i: (0, 0)
    return pl.pallas_call(
        _route_kernel,
        out_shape=(jax.ShapeDtypeStruct((TOP_K, T), I32),
                   jax.ShapeDtypeStruct((TOP_K, T), F32),
                   jax.ShapeDtypeStruct((TOP_K, T), I32),
                   jax.ShapeDtypeStruct((E, 1), I32)),
        grid=(T // tm,),
        in_specs=[pl.BlockSpec((E, tm), tok),
                  pl.BlockSpec((E, 1), const),
                  pl.BlockSpec((tm, tm), const)],
        out_specs=(pl.BlockSpec((TOP_K, tm), tok),
                   pl.BlockSpec((TOP_K, tm), tok),
                   pl.BlockSpec((TOP_K, tm), tok),
                   pl.BlockSpec((E, 1), const)),
        scratch_shapes=[pltpu.VMEM((E, 1), F32)],
        compiler_params=_cparams(("arbitrary",)),
        name="route_topk",
    )(logits_t, b_router, tri)


def _pos_kernel(e_ref, r_ref, ps_ref, o_ref):
    e = e_ref[...]
    tm = e.shape[1]
    ei = lax.broadcasted_iota(I32, (N_EXPERTS, tm), 0)
    ps = ps_ref[...]
    rows = [jnp.sum(jnp.where(ei == e[k:k + 1], ps, 0), axis=0, keepdims=True)
            for k in range(TOP_K)]
    o_ref[...] = jnp.concatenate(rows, axis=0) + r_ref[...]


def _positions(eidx, rank, pstart, tm):
    K, T = eidx.shape
    tok = lambda i: (0, i)
    return pl.pallas_call(
        _pos_kernel,
        out_shape=jax.ShapeDtypeStruct((K, T), I32),
        grid=(T // tm,),
        in_specs=[pl.BlockSpec((K, tm), tok), pl.BlockSpec((K, tm), tok),
                  pl.BlockSpec((N_EXPERTS, 1), lambda i: (0, 0))],
        out_specs=pl.BlockSpec((K, tm), tok),
        compiler_params=_cparams(("parallel",)),
        name="dispatch_positions",
    )(eidx, rank, pstart)


def _sc_mesh():
    return plsc.VectorSubcoreMesh(core_axis_name="c", subcore_axis_name="s")


def _worker_id():
    return lax.axis_index("s") * SC_CORES + lax.axis_index("c")


def _window_positions(pos):
    K, T = pos.shape
    nwin = T // (SC_WORKERS * SC_WINDOW)
    assert nwin * SC_WORKERS * SC_WINDOW == T, T
    return pos.reshape(K, SC_WORKERS, nwin, SC_WINDOW).transpose(1, 2, 0, 3)


def _sc_dispatch(h, pos, P):
    T, D = h.shape
    pos4 = _window_positions(pos)
    NW, nwin, K, W = pos4.shape

    @functools.partial(
        pl.kernel, mesh=_sc_mesh(),
        out_type=jax.ShapeDtypeStruct((P, D), h.dtype),
        scratch_types=[pltpu.VMEM((K, W), I32), pltpu.VMEM((W, D), h.dtype),
                       pltpu.SemaphoreType.DMA],
        name="sc_dispatch")
    def k(h_hbm, pos_hbm, xs_hbm, idx_v, rows_v, sem):
        wid = _worker_id()

        @pl.loop(0, nwin)
        def _(j):
            base = (wid * nwin + j) * W
            pltpu.sync_copy(pos_hbm.at[wid, j], idx_v)
            pltpu.sync_copy(h_hbm.at[pl.ds(base, W)], rows_v)
            copies = [pltpu.async_copy(rows_v, xs_hbm.at[idx_v.at[kk]], sem)
                      for kk in range(K)]
            for c in copies:
                c.wait()

    return k(h, pos4)


def _sc_gather(ys, pos):
    P, D = ys.shape
    T = pos.shape[1]
    pos4 = _window_positions(pos)
    NW, nwin, K, W = pos4.shape

    @functools.partial(
        pl.kernel, mesh=_sc_mesh(),
        out_type=jax.ShapeDtypeStruct((K, T, D), ys.dtype),
        scratch_types=[pltpu.VMEM((K, W), I32), pltpu.VMEM((W, D), ys.dtype),
                       pltpu.SemaphoreType.DMA],
        name="sc_gather")
    def k(ys_hbm, pos_hbm, yg_hbm, idx_v, rows_v, sem):
        wid = _worker_id()

        @pl.loop(0, nwin)
        def _(j):
            base = (wid * nwin + j) * W
            pltpu.sync_copy(pos_hbm.at[wid, j], idx_v)
            for kk in range(K):
                pltpu.async_copy(ys_hbm.at[idx_v.at[kk]], rows_v, sem).wait()
                pltpu.sync_copy(rows_v, yg_hbm.at[kk, pl.ds(base, W)])

    return k(ys, pos4)


def _expert_kernel(be_ref, nv_ref, x_ref, wg_ref, wu_ref, wd_ref, o_ref):
    i = pl.program_id(0)

    @pl.when(i < nv_ref[0])
    def _():
        lo, hi = _unpack_rows(x_ref[...])
        x = jnp.concatenate([lo, hi], axis=1).astype(BF16)
        a = (_silu(jnp.dot(x, wg_ref[0], preferred_element_type=F32))
             * jnp.dot(x, wu_ref[0], preferred_element_type=F32))
        o_ref[...] = _pack_rows(jnp.dot(a.astype(BF16), wd_ref[0], preferred_element_type=F32))


def _experts(xs, blk_e, nvalid, wg, wu, wd):
    P, Dh = xs.shape
    D = 2 * Dh
    bm = EXPERT_BLOCK
    nb = P // bm
    rowmap = lambda i, be, nv: (jnp.minimum(i, nv[0] - 1), 0)
    wmap = lambda i, be, nv: (be[i], 0, 0)
    grid_spec = pltpu.PrefetchScalarGridSpec(
        num_scalar_prefetch=2,
        grid=(nb,),
        in_specs=[pl.BlockSpec((bm, Dh), rowmap),
                  pl.BlockSpec((1, D, EXPERT_DIM), wmap),
                  pl.BlockSpec((1, D, EXPERT_DIM), wmap),
                  pl.BlockSpec((1, EXPERT_DIM, D), wmap)],
        out_specs=pl.BlockSpec((bm, Dh), rowmap),
    )
    return pl.pallas_call(
        _expert_kernel,
        out_shape=jax.ShapeDtypeStruct((P, Dh), I32),
        grid_spec=grid_spec,
        compiler_params=_cparams(("arbitrary",)),
        name="expert_ffn",
    )(blk_e, nvalid, xs, wg, wu, wd)


def _final_kernel(yg_ref, w_ref, h2_ref, x1_ref, mod_ref, modf_ref, nf_ref,
                  sg_ref, su_ref, sd_ref, o_ref):
    w = w_ref[...]
    y_lo = y_hi = None
    for k in range(TOP_K):
        lo, hi = _unpack_rows(yg_ref[k])
        wk = w[:, k:k + 1]
        y_lo = lo * wk if y_lo is None else y_lo + lo * wk
        y_hi = hi * wk if y_hi is None else y_hi + hi * wk
    y = jnp.concatenate([y_lo, y_hi], axis=1)
    hb = jnp.concatenate(_unpack_rows(h2_ref[...]), axis=1).astype(BF16)
    a = (_silu(jnp.dot(hb, sg_ref[...], preferred_element_type=F32))
         * jnp.dot(hb, su_ref[...], preferred_element_type=F32))
    shared = jnp.dot(a.astype(BF16), sd_ref[...], preferred_element_type=F32)
    mod = mod_ref[0]
    modf = modf_ref[0]
    x2 = x1_ref[...] + mod[5:6] * (y + shared)
    o_ref[...] = _rms(x2) * nf_ref[...] * (1.0 + modf[1:2]) + modf[0:1]


def _final(yg, wts_t, h2, x1, mod, modf, nfg, sg, su, sd, S, tm):
    T, D = x1.shape
    row = lambda i: (i, 0)
    const = lambda i: (0, 0)
    bat = lambda i: ((i * tm) // S, 0, 0)
    return pl.pallas_call(
        _final_kernel,
        out_shape=jax.ShapeDtypeStruct((T, D), F32),
        grid=(T // tm,),
        in_specs=[pl.BlockSpec((TOP_K, tm, D // 2), lambda i: (0, i, 0)),
                  pl.BlockSpec((tm, TOP_K), row),
                  pl.BlockSpec((tm, D // 2), row),
                  pl.BlockSpec((tm, D), row),
                  pl.BlockSpec((1, 6, D), bat),
                  pl.BlockSpec((1, 2, D), bat),
                  pl.BlockSpec((1, D), const),
                  pl.BlockSpec(sg.shape, const),
                  pl.BlockSpec(su.shape, const),
                  pl.BlockSpec(sd.shape, const)],
        out_specs=pl.BlockSpec((tm, D), row),
        compiler_params=_cparams(("parallel",)),
        name="combine_shared_final",
    )(yg, wts_t, h2, x1, mod, modf, nfg, sg, su, sd)


def _tile(n, pref):
    t = min(n, pref)
    assert n % t == 0, (n, pref)
    return t


def _plan_dispatch(logits_t, p):
    T = logits_t.shape[1]
    tm_r = _tile(T, 512)
    eidx, wts, rank, counts = _route(logits_t, p["b_router"], tm_r)
    bm = EXPERT_BLOCK
    counts = counts[:, 0]
    padded = (counts + bm - 1) // bm * bm
    pends = jnp.cumsum(padded)
    pstart = (pends - padded).astype(I32)
    nb = (T * TOP_K + N_EXPERTS * bm) // bm
    pos = _positions(eidx, rank, pstart[:, None], tm_r)
    blk_e = jnp.minimum(jnp.searchsorted(pends, jnp.arange(nb) * bm, side="right"),
                        N_EXPERTS - 1).astype(I32)
    nvalid = (pends[-1] // bm).astype(I32)
    blk_e = jnp.where(jnp.arange(nb) < nvalid, blk_e, blk_e[jnp.maximum(nvalid - 1, 0)])
    return wts.T, pos, blk_e, nvalid[None]


NORM_SLACK = 1.02
REF_GAP_LIMIT = 100.0


def _diff_attention(qh, kaug, vh, nrm, posq, p, B, S):
    qmax = jnp.sqrt(jnp.max(nrm[:, 0, 0, :2 * ATT_HEADS])) * NORM_SLACK
    kmax = jnp.sqrt(jnp.max(nrm[:, 1, 0, :2 * ATT_HEADS])) * NORM_SLACK
    in_range = 2.0 * qmax * kmax <= REF_GAP_LIMIT
    scal = jnp.concatenate([-p["slopes"], kmax[None]]).astype(F32)
    tq = _tile(S, 1024)
    tk = 2 * tq if S >= 16 * tq else tq

    def single_pass(_):
        return _attention_ref(qh, kaug, vh, scal, p["lam_p"], p["subln_g"], posq, B, S, tq, tk)

    def online(_):
        return _attention(qh, kaug, vh, p["slopes"], p["lam_p"], p["subln_g"], B, S,
                          _tile(S, 256), _tile(S, 512))

    return lax.cond(in_range, single_pass, online, None)


def _mixer_and_routing(x, mod, p):
    B, S, D = x.shape
    T = B * S
    xt = x.reshape(T, D)
    tm = _tile(S, 256)
    posq, posk = _alibi_tables(p["slopes"], S)
    proj, qh, kaug, vh, nrm = _proj(xt, mod, p["norm1_g"], p["w_in"], posk, S, tm)
    o_att = _diff_attention(qh, kaug, vh, nrm, posq, p, B, S)
    o_fw, o_bw = _hgrn2(proj, p["lbs"], B, S)
    x1, h2, logits_t = _mixout(xt, o_att, o_fw, o_bw, proj, mod, p["hg_norm_g"], p["w_out"],
                               p["norm2_g"], p["w_router"], S, tm)
    wts, pos, blk_e, nvalid = _plan_dispatch(logits_t, p)
    return dict(x1=x1, h2=h2, wts=wts, pos=pos, blk_e=blk_e, nvalid=nvalid, mod=mod, shape=(B, S, D))


def _expert_rows(st, p, after=None):
    T = st["h2"].shape[0]
    xs = _sc_dispatch(st["h2"], st["pos"], T * TOP_K + N_EXPERTS * EXPERT_BLOCK)
    if after is not None:
        xs, _ = lax.optimization_barrier((xs, after))
    return _experts(xs, st["blk_e"], st["nvalid"], p["wg"], p["wu"], p["wd"])


def _combine(st, ys, modf, p):
    B, S, D = st["shape"]
    yg = _sc_gather(ys, st["pos"])
    out = _final(yg, st["wts"], st["h2"], st["x1"], st["mod"], modf, p["normf_g"],
                 p["sg"], p["su"], p["sd"], S, _tile(S, 256))
    return out.reshape(B, S, D)


def _trunk(x, mod, modf, p):
    st = _mixer_and_routing(x, mod, p)
    return _combine(st, _expert_rows(st, p), modf, p)


def kernel(x_prompt, x_sample, c_prompt, c_sample, w_ada, b_ada, norm1_g, w_in, lam_q1, lam_k1, lam_q2, lam_k2, subln_g, hg_lb_logits, hg_norm_g, w_out, norm2_g, w_router, b_router, w_exp_gate, w_exp_up, w_exp_down, w_sh_gate, w_sh_up, w_sh_down, w_ada_f, b_ada_f, normf_g):
    D = x_prompt.shape[-1]
    Bp, Bs = c_prompt.shape[0], c_sample.shape[0]
    c_all = jnp.concatenate([c_prompt, c_sample], axis=0)
    R = -(-c_all.shape[0] // 8) * 8
    c_all = jnp.pad(c_all, ((0, R - c_all.shape[0]), (0, 0)))
    mod6 = _ada(c_all, w_ada[0], b_ada[0][None]).reshape(R, 6, D)
    mod2 = _ada(c_all, w_ada_f, b_ada_f[None]).reshape(R, 2, D)

    lbs = jax.nn.softmax(hg_lb_logits.astype(F32), axis=0)[0]
    lbs = lbs.reshape(2, HG_HEADS, HEAD_W).transpose(1, 0, 2)
    slopes = (jnp.exp2(-8.0 * (jnp.arange(ATT_HEADS, dtype=F32) + 1.0) / ATT_HEADS) * LOG2E)
    p = dict(
        norm1_g=norm1_g[0][None], w_in=w_in[0].astype(BF16),
        slopes=slopes.astype(F32),
        lam_p=jnp.stack([lam_q1[0], lam_k1[0], lam_q2[0], lam_k2[0]]).astype(F32),
        subln_g=subln_g[0][:, None], lbs=lbs, hg_norm_g=hg_norm_g[0][None],
        w_out=w_out[0].astype(BF16), norm2_g=norm2_g[0][None],
        w_router=jnp.concatenate(_split3(w_router[0].astype(F32))[:2], axis=1),
        b_router=b_router[0][:, None],
        wg=w_exp_gate[0].astype(BF16), wu=w_exp_up[0].astype(BF16),
        wd=w_exp_down[0].astype(BF16),
        sg=w_sh_gate[0].astype(BF16), su=w_sh_up[0].astype(BF16), sd=w_sh_down[0].astype(BF16),
        normf_g=normf_g[None],
    )
    st_p = _mixer_and_routing(x_prompt, mod6[:Bp], p)
    x_sample, _ = lax.optimization_barrier((x_sample, st_p["pos"]))
    st_s = _mixer_and_routing(x_sample, mod6[Bp:Bp + Bs], p)
    ys_p = _expert_rows(st_p, p)
    ys_s = _expert_rows(st_s, p, after=ys_p)
    y_prompt = _combine(st_p, ys_p, mod2[:Bp], p)
    y_sample = _combine(st_s, ys_s, mod2[Bp:Bp + Bs], p)
    return (y_prompt, y_sample)
```

```python
import functools
import math

import jax
import jax.numpy as jnp
from jax import lax
from jax.experimental import pallas as pl
from jax.experimental.pallas import tpu as pltpu
from jax.experimental.pallas import tpu_sc as plsc

F32 = jnp.float32
BF16 = jnp.bfloat16
I32 = jnp.int32

EPS = 1e-6
LOG2E = 1.4426950408889634

ATT_HEADS = 4
ATT_QK_DIM = 64
HEAD_W = 128
ATT_WIDTH = ATT_HEADS * HEAD_W
HG_HEADS = 4
HG_WIDTH = HG_HEADS * HEAD_W
N_EXPERTS = 256
TOP_K = 8
N_GROUPS = 8
TOPK_GROUPS = 4
GROUP_SIZE = N_EXPERTS // N_GROUPS
EXPERT_DIM = 256
ROUTE_SCALE = 2.5
LAM_INIT = 0.8 - 0.6 * math.exp(-0.3 * 0)

HG_CHUNK = 128
HG_SUB = 16
HG_MILD_LOG_GATE = 3.75
HG_STEP_CHUNKS = 4
EXPERT_BLOCK = 512
SC_CORES = 2
SC_SUBCORES = 16
SC_WORKERS = SC_CORES * SC_SUBCORES
SC_WINDOW = 32
VMEM_LIMIT = 56 * 1024 * 1024

NEG_INF = float("-inf")


def _cparams(sem):
    return pltpu.CompilerParams(dimension_semantics=sem, vmem_limit_bytes=VMEM_LIMIT)


def _silu(x):
    return x * jax.nn.sigmoid(x)


def _pack_rows(x):
    half = x.shape[1] // 2
    bits = lax.bitcast_convert_type(x.astype(BF16).astype(F32), jnp.uint32)
    return lax.bitcast_convert_type((bits[:, :half] >> 16) | bits[:, half:], I32)


def _unpack_rows(w):
    u = lax.bitcast_convert_type(w, jnp.uint32)
    lo = lax.bitcast_convert_type(u << 16, F32)
    hi = lax.bitcast_convert_type(u & jnp.uint32(0xFFFF0000), F32)
    return lo, hi


def _ada_kernel(c_ref, w_ref, b_ref, o_ref):
    cs = _silu(c_ref[...])
    o_ref[...] = jnp.dot(cs, w_ref[...], preferred_element_type=F32,
                         precision=lax.Precision.HIGHEST) + b_ref[...]


def _ada(c, w, b):
    R, D = c.shape
    N = w.shape[1]
    tn = 1024
    return pl.pallas_call(
        _ada_kernel,
        out_shape=jax.ShapeDtypeStruct((R, N), F32),
        grid=(N // tn,),
        in_specs=[pl.BlockSpec((R, D), lambda j: (0, 0)),
                  pl.BlockSpec((D, tn), lambda j: (0, j)),
                  pl.BlockSpec((1, tn), lambda j: (0, j))],
        out_specs=pl.BlockSpec((R, tn), lambda j: (0, j)),
        compiler_params=_cparams(("arbitrary",)),
        name="ada_mod",
    )(c, w, b)


def _rms(x):
    return x * lax.rsqrt(jnp.mean(x * x, axis=-1, keepdims=True) + EPS)


def _proj_kernel(x_ref, mod_ref, g_ref, w_ref, posk_ref, grp_ref,
                 o_ref, q_ref, ka_ref, v_ref, nrm_ref, *, qscale):
    x = x_ref[...]
    mod = mod_ref[0]
    h = _rms(x) * g_ref[...] * (1.0 + mod[1:2]) + mod[0:1]
    acc = jnp.dot(h.astype(BF16), w_ref[...], preferred_element_type=F32)
    qw = ATT_WIDTH
    qb = (acc[:, :qw] * qscale).astype(BF16)
    kb = acc[:, qw:2 * qw].astype(BF16)
    vb = acc[:, 2 * qw:3 * qw].astype(BF16)
    o_ref[...] = acc[:, 3 * qw:].astype(BF16)
    for hh in range(ATT_HEADS):
        cols = slice(hh * HEAD_W, (hh + 1) * HEAD_W)
        q_ref[hh] = qb[:, cols]
        v_ref[hh] = vb[:, cols]
        ka_ref[hh, :, :HEAD_W] = kb[:, cols]
        ka_ref[hh, :, HEAD_W:] = posk_ref[:, cols]
    for idx, t in enumerate((qb, kb)):
        tf = t.astype(F32)
        n2 = jnp.dot((tf * tf).astype(BF16), grp_ref[...], preferred_element_type=F32)
        nrm_ref[0, idx] = jnp.broadcast_to(jnp.max(n2, axis=0, keepdims=True), (8, HEAD_W))


def _proj(xt, mod, g, w, posk, S, tm):
    T, D = xt.shape
    N = w.shape[1]
    qscale = ATT_QK_DIM ** -0.5 * LOG2E
    nblk = T // tm
    spb = S // tm
    grp = (jnp.arange(ATT_WIDTH)[:, None] // ATT_QK_DIM == jnp.arange(HEAD_W)[None, :]).astype(BF16)
    return pl.pallas_call(
        functools.partial(_proj_kernel, qscale=qscale),
        out_shape=(jax.ShapeDtypeStruct((T, N - 3 * ATT_WIDTH), BF16),
                   jax.ShapeDtypeStruct((ATT_HEADS, T, HEAD_W), BF16),
                   jax.ShapeDtypeStruct((ATT_HEADS, T, 2 * HEAD_W), BF16),
                   jax.ShapeDtypeStruct((ATT_HEADS, T, HEAD_W), BF16),
                   jax.ShapeDtypeStruct((nblk, 2, 8, HEAD_W), F32)),
        grid=(nblk,),
        in_specs=[pl.BlockSpec((tm, D), lambda i: (i, 0)),
                  pl.BlockSpec((1, 6, D), lambda i: ((i * tm) // S, 0, 0)),
                  pl.BlockSpec((1, D), lambda i: (0, 0)),
                  pl.BlockSpec((D, N), lambda i: (0, 0)),
                  pl.BlockSpec((tm, ATT_WIDTH), lambda i: (i % spb, 0)),
                  pl.BlockSpec((ATT_WIDTH, HEAD_W), lambda i: (0, 0))],
        out_specs=(pl.BlockSpec((tm, N - 3 * ATT_WIDTH), lambda i: (i, 0)),
                   pl.BlockSpec((ATT_HEADS, tm, HEAD_W), lambda i: (0, i, 0)),
                   pl.BlockSpec((ATT_HEADS, tm, 2 * HEAD_W), lambda i: (0, i, 0)),
                   pl.BlockSpec((ATT_HEADS, tm, HEAD_W), lambda i: (0, i, 0)),
                   pl.BlockSpec((1, 2, 8, HEAD_W), lambda i: (i, 0, 0, 0))),
        compiler_params=_cparams(("parallel",)),
        name="norm1_proj",
    )(xt, mod, g, w, posk, grp)


def _split3(x):
    hi = x.astype(BF16)
    r = x - hi.astype(F32)
    mid = r.astype(BF16)
    lo = (r - mid.astype(F32)).astype(BF16)
    return hi, mid, lo


def _alibi_tables(slopes2, S):
    H = slopes2.shape[0]
    a = slopes2[:, None] * jnp.arange(S, dtype=F32)[None, :]
    ah, am, al = (t.astype(F32)[..., None] for t in _split3(a))
    lane = jnp.arange(HEAD_W)[None, None, :]
    terms = jnp.where(lane % 3 == 0, ah, jnp.where(lane % 3 == 1, am, al))
    posk = jnp.where(lane < 6, 1.0, jnp.where(lane < 9, terms, 0.0)).astype(BF16)
    posq = jnp.where(lane < 3, 0.0, jnp.where(lane < 6, -terms, jnp.where(lane < 9, 1.0, 0.0))).astype(BF16)
    return posq, posk.transpose(1, 0, 2).reshape(S, H * HEAD_W)


def _attn_kernel(slope_ref, lam_ref, q_ref, k_ref, v_ref, g_ref, o_ref,
                 qm_ref, m_ref, l_ref, acc_ref, *, tq, tk, nk):
    h = pl.program_id(1)
    qi = pl.program_id(2)
    ki = pl.program_id(3)

    @pl.when(ki == 0)
    def _init():
        q = q_ref[...]
        lane = lax.broadcasted_iota(I32, q.shape, 1)
        zero = jnp.zeros_like(q)
        qm_ref[0] = jnp.where(lane < ATT_QK_DIM, q, zero)
        qm_ref[1] = jnp.where(lane >= ATT_QK_DIM, q, zero)
        m_ref[...] = jnp.full(m_ref.shape, NEG_INF, F32)
        l_ref[...] = jnp.zeros(l_ref.shape, F32)
        acc_ref[...] = jnp.zeros(acc_ref.shape, F32)

    slope = slope_ref[h]
    k = k_ref[...]
    v = v_ref[...]
    row = lax.broadcasted_iota(I32, (tk, tq), 0)
    col = lax.broadcasted_iota(I32, (tk, tq), 1)
    dist = (col - row + (qi * tq - ki * tk)).astype(F32)
    bias = jnp.abs(dist) * (-slope)
    for m in range(2):
        s = lax.dot_general(k, qm_ref[m], (((1,), (1,)), ((), ())),
                            preferred_element_type=F32) + bias
        m_old = m_ref[m]
        m_new = jnp.maximum(m_old, jnp.max(s, axis=0, keepdims=True))
        alpha = jnp.exp2(m_old - m_new)
        p = jnp.exp2(s - m_new)
        l_ref[m] = alpha * l_ref[m] + jnp.sum(p, axis=0, keepdims=True)
        pv = lax.dot_general(v, p.astype(BF16), (((0,), (0,)), ((), ())),
                             preferred_element_type=F32)
        acc_ref[m] = alpha * acc_ref[m] + pv
        m_ref[m] = m_new

    @pl.when(ki == nk - 1)
    def _fin():
        lp = lam_ref[...]
        lam = (jnp.exp(jnp.sum(lp[0:1] * lp[1:2], axis=-1, keepdims=True))
               - jnp.exp(jnp.sum(lp[2:3] * lp[3:4], axis=-1, keepdims=True))
               + LAM_INIT)
        o = acc_ref[0] / l_ref[0] - lam * (acc_ref[1] / l_ref[1])
        ms = jnp.mean(o * o, axis=0, keepdims=True)
        y = o * lax.rsqrt(ms + EPS) * g_ref[...] * (1.0 - LAM_INIT)
        o_ref[...] = y.T.astype(o_ref.dtype)


def _attention(qh, kaug, vh, slopes, lam_p, subln_g, B, S, tq, tk):
    T = B * S
    nq, nk = S // tq, S // tk
    grid_spec = pltpu.PrefetchScalarGridSpec(
        num_scalar_prefetch=1,
        grid=(B, ATT_HEADS, nq, nk),
        in_specs=[
            pl.BlockSpec((4, ATT_QK_DIM), lambda b, h, qi, ki, s: (0, 0)),
            pl.BlockSpec((None, tq, HEAD_W), lambda b, h, qi, ki, s: (h, b * nq + qi, 0)),
            pl.BlockSpec((None, tk, HEAD_W), lambda b, h, qi, ki, s: (h, b * nk + ki, 0)),
            pl.BlockSpec((None, tk, HEAD_W), lambda b, h, qi, ki, s: (h, b * nk + ki, 0)),
            pl.BlockSpec((HEAD_W, 1), lambda b, h, qi, ki, s: (0, 0)),
        ],
        out_specs=pl.BlockSpec((tq, HEAD_W), lambda b, h, qi, ki, s: (b * nq + qi, h)),
        scratch_shapes=[pltpu.VMEM((2, tq, HEAD_W), BF16),
                        pltpu.VMEM((2, 1, tq), F32),
                        pltpu.VMEM((2, 1, tq), F32),
                        pltpu.VMEM((2, HEAD_W, tq), F32)],
    )
    return pl.pallas_call(
        functools.partial(_attn_kernel, tq=tq, tk=tk, nk=nk),
        out_shape=jax.ShapeDtypeStruct((T, ATT_WIDTH), BF16),
        grid_spec=grid_spec,
        compiler_params=_cparams(("parallel", "parallel", "parallel", "arbitrary")),
        name="diff_attention",
    )(slopes, lam_p, qh, kaug, vh, subln_g)


def _attn_finish(lam_ref, g_ref, o_ref, l_ref, acc_ref):
    lp = lam_ref[...]
    lam = (jnp.exp(jnp.sum(lp[0:1] * lp[1:2], axis=-1, keepdims=True))
           - jnp.exp(jnp.sum(lp[2:3] * lp[3:4], axis=-1, keepdims=True))
           + LAM_INIT)
    o = acc_ref[0] / l_ref[0] - lam * (acc_ref[1] / l_ref[1])
    ms = jnp.mean(o * o, axis=0, keepdims=True)
    y = o * lax.rsqrt(ms + EPS) * g_ref[...] * (1.0 - LAM_INIT)
    o_ref[...] = y.T.astype(o_ref.dtype)


def _attn_ref_kernel(sc_ref, lam_ref, q_ref, ka_ref, v_ref, g_ref, posq_ref, o_ref,
                     qa_ref, l_ref, acc_ref, *, tq, tk, nk):
    h = pl.program_id(1)
    qi = pl.program_id(2)
    ki = pl.program_id(3)

    @pl.when(ki == 0)
    def _init():
        q = q_ref[...]
        lane = lax.broadcasted_iota(I32, q.shape, 1)
        zero = jnp.zeros_like(q)
        kmax = sc_ref[ATT_HEADS]
        pa = posq_ref[...]
        for m in range(2):
            sel = (lane < ATT_QK_DIM) if m == 0 else (lane >= ATT_QK_DIM)
            qm = jnp.where(sel, q, zero)
            qf = qm.astype(F32)
            ub = jnp.sqrt(jnp.sum(qf * qf, axis=1, keepdims=True)) * kmax
            uh, um, ul = (t.astype(F32) for t in _split3(jnp.broadcast_to(ub, q.shape)))
            ubp = jnp.where(lane == 0, -uh, jnp.where(lane == 1, -um, jnp.where(lane == 2, -ul, 0.0)))
            ubp = ubp.astype(BF16)
            for var, pp in enumerate((pa + ubp, ubp - pa, ubp)):
                qa_ref[2 * var + m] = jnp.concatenate([qm, pp], axis=1)
        l_ref[...] = jnp.zeros(l_ref.shape, F32)
        acc_ref[...] = jnp.zeros(acc_ref.shape, F32)

    rel = qi * tq - ki * tk
    near = jnp.logical_and(rel > -tq, rel < tk)

    def body(with_bias):
        ka = ka_ref[...]
        if with_bias:
            row = lax.broadcasted_iota(I32, (tk, tq), 0)
            col = lax.broadcasted_iota(I32, (tk, tq), 1)
            bias = jnp.abs((col - row + rel).astype(F32)) * sc_ref[h]
            var = 2
        else:
            var = jnp.where(rel > 0, 0, 1)
        ss = [lax.dot_general(ka, qa_ref[2 * var + m], (((1,), (1,)), ((), ())),
                              preferred_element_type=F32) for m in range(2)]
        v = v_ref[...]
        for m in range(2):
            p = jnp.exp2(ss[m] + bias if with_bias else ss[m])
            l_ref[m] += jnp.sum(p, axis=0, keepdims=True)
            acc_ref[m] += lax.dot_general(v, p.astype(BF16), (((0,), (0,)), ((), ())),
                                          preferred_element_type=F32)

    @pl.when(near)
    def _():
        body(True)

    @pl.when(jnp.logical_not(near))
    def _():
        body(False)

    @pl.when(ki == nk - 1)
    def _fin():
        _attn_finish(lam_ref, g_ref, o_ref, l_ref, acc_ref)


def _attention_ref(qh, kaug, vh, scal, lam_p, subln_g, posq, B, S, tq, tk):
    T = B * S
    nq, nk = S // tq, S // tk
    grid_spec = pltpu.PrefetchScalarGridSpec(
        num_scalar_prefetch=1,
        grid=(B, ATT_HEADS, nq, nk),
        in_specs=[
            pl.BlockSpec((4, ATT_QK_DIM), lambda b, h, qi, ki, s: (0, 0)),
            pl.BlockSpec((None, tq, HEAD_W), lambda b, h, qi, ki, s: (h, b * nq + qi, 0)),
            pl.BlockSpec((None, tk, 2 * HEAD_W), lambda b, h, qi, ki, s: (h, b * nk + ki, 0)),
            pl.BlockSpec((None, tk, HEAD_W), lambda b, h, qi, ki, s: (h, b * nk + ki, 0)),
            pl.BlockSpec((HEAD_W, 1), lambda b, h, qi, ki, s: (0, 0)),
            pl.BlockSpec((None, tq, HEAD_W), lambda b, h, qi, ki, s: (h, qi, 0)),
        ],
        out_specs=pl.BlockSpec((tq, HEAD_W), lambda b, h, qi, ki, s: (b * nq + qi, h)),
        scratch_shapes=[pltpu.VMEM((6, tq, 2 * HEAD_W), BF16),
                        pltpu.VMEM((2, 1, tq), F32),
                        pltpu.VMEM((2, HEAD_W, tq), F32)],
    )
    return pl.pallas_call(
        functools.partial(_attn_ref_kernel, tq=tq, tk=tk, nk=nk),
        out_shape=jax.ShapeDtypeStruct((T, ATT_WIDTH), BF16),
        grid_spec=grid_spec,
        compiler_params=_cparams(("parallel", "parallel", "parallel", "arbitrary")),
        name="diff_attention_ref",
    )(scal, lam_p, qh, kaug, vh, subln_g, posq)


def _hg_chunk(q, kk, v, g, st, reverse):
    C = q.shape[0]
    nsub = C // HG_SUB
    row = lax.broadcasted_iota(I32, (C, HEAD_W), 0)
    b = _hg_cumsum(g, reverse)
    b_end = b[C - 1:C] if not reverse else b[0:1]

    qd = (q * jnp.exp(b)).astype(BF16)
    o_inter = lax.dot_general(qd, st.astype(BF16), (((1,), (1,)), ((), ())),
                              preferred_element_type=F32)

    ones = jnp.ones((HEAD_W, HEAD_W), BF16)
    srow = lax.broadcasted_iota(I32, (HG_SUB, HEAD_W), 0)
    vb = v.astype(BF16)
    outs = []
    for blk in range(nsub):
        lo, hi = blk * HG_SUB, (blk + 1) * HG_SUB
        qI, kI, vI, bI = q[lo:hi], kk[lo:hi], v[lo:hi], b[lo:hi]
        pieces = []
        for t in range(HG_SUB):
            dl = bI[t:t + 1] - bI
            keep = (srow <= t) if not reverse else (srow >= t)
            e = jnp.exp(jnp.where(keep, dl, NEG_INF))
            pieces.append(qI[t:t + 1] * kI * e)
        wst = jnp.concatenate(pieces, axis=0).astype(BF16)
        rsum = jnp.dot(wst, ones, preferred_element_type=F32)
        o_blk = jnp.sum(rsum.reshape(HG_SUB, HG_SUB, HEAD_W) * vI[None], axis=1)
        has_off = (blk > 0) if not reverse else (blk < nsub - 1)
        if has_off:
            if not reverse:
                r = b[lo - 1:lo]
                kmask = row < lo
            else:
                r = b[hi:hi + 1]
                kmask = row >= hi
            qs = (qI * jnp.exp(bI - r)).astype(BF16)
            ks = (kk * jnp.exp(jnp.where(kmask, r - b, NEG_INF))).astype(BF16)
            a = lax.dot_general(qs, ks, (((1,), (1,)), ((), ())),
                                preferred_element_type=F32)
            o_blk = o_blk + jnp.dot(a.astype(BF16), vb, preferred_element_type=F32)
        outs.append(o_blk)
    o = o_inter + jnp.concatenate(outs, axis=0)

    kd = (kk * jnp.exp(b_end - b)).astype(BF16)
    upd = lax.dot_general(vb, kd, (((0,), (0,)), ((), ())),
                          preferred_element_type=F32)
    st_new = st * jnp.exp(b_end) + upd
    return o, st_new


def _hg_cumsum(g, reverse):
    C = g.shape[0]
    row = lax.broadcasted_iota(I32, g.shape, 0)
    b = g
    d = 1
    while d < C:
        if not reverse:
            b = b + jnp.where(row >= d, pltpu.roll(b, d, axis=0), 0.0)
        else:
            b = b + jnp.where(row < C - d, pltpu.roll(b, C - d, axis=0), 0.0)
        d *= 2
    return b


def _hg_chunk_mild(q, kk, v, g, st, reverse):
    C = q.shape[0]
    nsub = C // HG_SUB
    row = lax.broadcasted_iota(I32, (C, HEAD_W), 0)
    b = _hg_cumsum(g, reverse)
    b_end = b[C - 1:C] if not reverse else b[0:1]
    qd = (q * jnp.exp(b)).astype(BF16)
    o_inter = lax.dot_general(qd, st.astype(BF16), (((1,), (1,)), ((), ())),
                              preferred_element_type=F32)
    vb = v.astype(BF16)
    qrow = lax.broadcasted_iota(I32, (HG_SUB, C), 0)
    kcol = lax.broadcasted_iota(I32, (HG_SUB, C), 1)
    zero_ref = jnp.zeros((1, HEAD_W), F32)
    a_rows = []
    for blk in range(nsub):
        lo, hi = blk * HG_SUB, (blk + 1) * HG_SUB
        if not reverse:
            r = b[lo - 1:lo] if blk > 0 else zero_ref
            kmask = row < hi
            amask = kcol <= qrow + lo
        else:
            r = b[hi:hi + 1] if blk < nsub - 1 else zero_ref
            kmask = row >= lo
            amask = kcol >= qrow + lo
        qs = (q[lo:hi] * jnp.exp(b[lo:hi] - r)).astype(BF16)
        ks = (kk * jnp.exp(jnp.where(kmask, r - b, NEG_INF))).astype(BF16)
        a = lax.dot_general(qs, ks, (((1,), (1,)), ((), ())), preferred_element_type=F32)
        a_rows.append(jnp.where(amask, a, 0.0))
    a_full = jnp.concatenate(a_rows, axis=0).astype(BF16)
    o = o_inter + jnp.dot(a_full, vb, preferred_element_type=F32)
    kd = (kk * jnp.exp(b_end - b)).astype(BF16)
    upd = lax.dot_general(vb, kd, (((0,), (0,)), ((), ())), preferred_element_type=F32)
    return o, st * jnp.exp(b_end) + upd


def _hg_kernel(qf_ref, ff_ref, if_ref, qb_ref, fb_ref, ib_ref, lb_ref, thr_ref,
               of_ref, ob_ref, sf_ref, sb_ref, *, nchunk):
    j = pl.program_id(2)

    @pl.when(j == 0)
    def _init():
        sf_ref[...] = jnp.zeros(sf_ref.shape, F32)
        sb_ref[...] = jnp.zeros(sb_ref.shape, F32)

    lb = lb_ref[0]
    C = HG_CHUNK

    def prep(q_ref, f_ref, i_ref, lbd, rows):
        q = _silu(q_ref[rows, :].astype(F32))
        f = lbd + (1.0 - lbd) * jax.nn.sigmoid(f_ref[rows, :].astype(F32))
        return q, 1.0 - f, i_ref[rows, :].astype(F32), jnp.log(f)

    def step(chunk_fn, c, carry):
        rows_f = pl.ds(pl.multiple_of(c * C, C), C)
        rows_b = pl.ds(pl.multiple_of((nchunk - 1 - c) * C, C), C)
        o, st = chunk_fn(*prep(qf_ref, ff_ref, if_ref, lb[0:1], rows_f), sf_ref[...], reverse=False)
        of_ref[rows_f, :] = o
        sf_ref[...] = st
        o, st = chunk_fn(*prep(qb_ref, fb_ref, ib_ref, lb[1:2], rows_b), sb_ref[...], reverse=True)
        ob_ref[rows_b, :] = o
        sb_ref[...] = st
        return carry

    thr = thr_ref[0]
    above = jnp.minimum(
        jnp.min(jnp.where(ff_ref[...].astype(F32) >= thr[0:1], 1.0, 0.0)),
        jnp.min(jnp.where(fb_ref[...].astype(F32) >= thr[1:2], 1.0, 0.0)))
    mild = above > 0.5

    @pl.when(mild)
    def _():
        lax.fori_loop(0, nchunk, functools.partial(step, _hg_chunk_mild), 0)

    @pl.when(jnp.logical_not(mild))
    def _():
        lax.fori_loop(0, nchunk, functools.partial(step, _hg_chunk), 0)


def _hgrn2(proj, lbs, B, S):
    T = B * S
    nchunk = min(HG_STEP_CHUNKS, S // HG_CHUNK)
    C = HG_CHUNK * nchunk
    n = S // C
    nh = HG_HEADS
    qc, ffc, fbc, ic = 0, nh, 2 * nh, 3 * nh

    def fw(col):
        return pl.BlockSpec((C, HEAD_W), lambda b, h, j: (b * n + j, col + h))

    def bw(col):
        return pl.BlockSpec((C, HEAD_W), lambda b, h, j: (b * n + n - 1 - j, col + h))

    out_f = pl.BlockSpec((C, HEAD_W), lambda b, h, j: (b * n + j, h))
    out_b = pl.BlockSpec((C, HEAD_W), lambda b, h, j: (b * n + n - 1 - j, h))
    need = (math.exp(-HG_MILD_LOG_GATE) - lbs) / (1.0 - lbs)
    thr = jnp.where(need > 0.0, jnp.log(jnp.maximum(need, 1e-30) / (1.0 - need)), NEG_INF).astype(F32)
    per_head = pl.BlockSpec((1, 2, HEAD_W), lambda b, h, j: (h, 0, 0))
    return pl.pallas_call(
        functools.partial(_hg_kernel, nchunk=nchunk),
        out_shape=(jax.ShapeDtypeStruct((T, HG_WIDTH), F32),
                   jax.ShapeDtypeStruct((T, HG_WIDTH), F32)),
        grid=(B, nh, n),
        in_specs=[fw(qc), fw(ffc), fw(ic), bw(qc), bw(fbc), bw(ic), per_head, per_head],
        out_specs=(out_f, out_b),
        scratch_shapes=[pltpu.VMEM((HEAD_W, HEAD_W), F32),
                        pltpu.VMEM((HEAD_W, HEAD_W), F32)],
        compiler_params=_cparams(("parallel", "parallel", "arbitrary")),
        name="hgrn2_scan",
    )(proj, proj, proj, proj, proj, proj, lbs, thr)


def _mixout_kernel(x_ref, oa_ref, of_ref, ob_ref, gate_ref, mod_ref, hgg_ref, wo_ref,
                   n2_ref, wr_ref, x1_ref, h2_ref, lg_ref):
    mod = mod_ref[0]
    o = of_ref[...] + ob_ref[...]
    gate = _silu(gate_ref[...].astype(F32))
    hg = jnp.concatenate(
        [_rms(o[:, h * HEAD_W:(h + 1) * HEAD_W]) * hgg_ref[...] for h in range(HG_HEADS)],
        axis=-1) * gate
    mix = (jnp.dot(oa_ref[...], wo_ref[:ATT_WIDTH, :], preferred_element_type=F32)
           + jnp.dot(hg.astype(BF16), wo_ref[ATT_WIDTH:, :], preferred_element_type=F32))
    x1 = x_ref[...] + mod[2:3] * mix
    x1_ref[...] = x1
    h2 = _rms(x1) * n2_ref[...] * (1.0 + mod[4:5]) + mod[3:4]
    h2_ref[...] = _pack_rows(h2)
    h_hi = h2.astype(BF16)
    h_mid = (h2 - h_hi.astype(F32)).astype(BF16)
    two = jnp.dot(h_hi, wr_ref[...], preferred_element_type=F32)
    logits = (two[:, :N_EXPERTS] + two[:, N_EXPERTS:]
              + jnp.dot(h_mid, wr_ref[:, :N_EXPERTS], preferred_element_type=F32))
    lg_ref[...] = logits.T


def _mixout(xt, o_att, o_fw, o_bw, proj, mod, hg_g, w_out, n2g, w_router, S, tm):
    T, D = xt.shape
    gate_col = proj.shape[1] // HG_WIDTH - 1
    row = lambda i: (i, 0)
    const = lambda i: (0, 0)
    return pl.pallas_call(
        _mixout_kernel,
        out_shape=(jax.ShapeDtypeStruct((T, D), F32),
                   jax.ShapeDtypeStruct((T, D // 2), I32),
                   jax.ShapeDtypeStruct((N_EXPERTS, T), F32)),
        grid=(T // tm,),
        in_specs=[pl.BlockSpec((tm, D), row),
                  pl.BlockSpec((tm, ATT_WIDTH), row),
                  pl.BlockSpec((tm, HG_WIDTH), row),
                  pl.BlockSpec((tm, HG_WIDTH), row),
                  pl.BlockSpec((tm, HG_WIDTH), lambda i: (i, gate_col)),
                  pl.BlockSpec((1, 6, D), lambda i: ((i * tm) // S, 0, 0)),
                  pl.BlockSpec((1, HEAD_W), const),
                  pl.BlockSpec(w_out.shape, const),
                  pl.BlockSpec((1, D), const),
                  pl.BlockSpec(w_router.shape, const)],
        out_specs=(pl.BlockSpec((tm, D), row),
                   pl.BlockSpec((tm, D // 2), row),
                   pl.BlockSpec((N_EXPERTS, tm), lambda i: (0, i))),
        compiler_params=_cparams(("parallel",)),
        name="mixout_norm2_router",
    )(xt, o_att, o_fw, o_bw, proj, mod, hg_g, w_out, n2g, w_router)


def _first_argmax(x, iota, size):
    mx = jnp.max(x, axis=0, keepdims=True)
    idx = jnp.min(jnp.where(x == mx, iota, size), axis=0, keepdims=True)
    return mx, idx


def _route_kernel(lg_ref, br_ref, tri_ref, e_ref, w_ref, r_ref, cnt_ref, carry_ref):
    i = pl.program_id(0)

    @pl.when(i == 0)
    def _init():
        carry_ref[...] = jnp.zeros(carry_ref.shape, F32)

    scores = jax.nn.sigmoid(lg_ref[...])
    biased = scores + br_ref[...]
    tm = scores.shape[1]
    giota = lax.broadcasted_iota(I32, (GROUP_SIZE, tm), 0)
    gs = []
    for g in range(N_GROUPS):
        blk = biased[g * GROUP_SIZE:(g + 1) * GROUP_SIZE]
        m1, i1 = _first_argmax(blk, giota, GROUP_SIZE)
        m2 = jnp.max(jnp.where(giota == i1, NEG_INF, blk), axis=0, keepdims=True)
        gs.append(m1 + m2)
    gsc = jnp.concatenate(gs, axis=0)
    gi = lax.broadcasted_iota(I32, (N_GROUPS, tm), 0)
    gsel = jnp.zeros((N_GROUPS, tm), jnp.bool_)
    for _ in range(TOPK_GROUPS):
        _, idx = _first_argmax(gsc, gi, N_GROUPS)
        hit = gi == idx
        gsel = jnp.logical_or(gsel, hit)
        gsc = jnp.where(hit, NEG_INF, gsc)
    masked = jnp.concatenate(
        [jnp.where(gsel[g:g + 1], biased[g * GROUP_SIZE:(g + 1) * GROUP_SIZE], NEG_INF)
         for g in range(N_GROUPS)], axis=0)
    ei = lax.broadcasted_iota(I32, (N_EXPERTS, tm), 0)
    eidx, wts = [], []
    onehot = jnp.zeros((N_EXPERTS, tm), F32)
    for _ in range(TOP_K):
        _, idx = _first_argmax(masked, ei, N_EXPERTS)
        hit = ei == idx
        eidx.append(idx)
        wts.append(jnp.sum(jnp.where(hit, scores, 0.0), axis=0, keepdims=True))
        onehot = jnp.where(hit, 1.0, onehot)
        masked = jnp.where(hit, NEG_INF, masked)
    w = jnp.concatenate(wts, axis=0)
    w = w / (jnp.sum(w, axis=0, keepdims=True) + 1e-20) * ROUTE_SCALE
    e_ref[...] = jnp.concatenate(eidx, axis=0)
    w_ref[...] = w
    before = jnp.dot(onehot.astype(BF16), tri_ref[...], preferred_element_type=F32)
    before = before + carry_ref[...]
    ranks = [jnp.sum(jnp.where(ei == idx, before, 0.0), axis=0, keepdims=True) for idx in eidx]
    r_ref[...] = jnp.concatenate(ranks, axis=0).astype(I32)
    carry = carry_ref[...] + jnp.sum(onehot, axis=1, keepdims=True)
    carry_ref[...] = carry
    cnt_ref[...] = carry.astype(I32)


def _route(logits_t, b_router, tm):
    E, T = logits_t.shape
    tri = (jnp.arange(tm)[:, None] < jnp.arange(tm)[None, :]).astype(BF16)
    tok = lambda i: (0, i)
    const = lambda i: (0, 0)
    return pl.pallas_call(
        _route_kernel,
        out_shape=(jax.ShapeDtypeStruct((TOP_K, T), I32),
                   jax.ShapeDtypeStruct((TOP_K, T), F32),
                   jax.ShapeDtypeStruct((TOP_K, T), I32),
                   jax.ShapeDtypeStruct((E, 1), I32)),
        grid=(T // tm,),
        in_specs=[pl.BlockSpec((E, tm), tok),
                  pl.BlockSpec((E, 1), const),
                  pl.BlockSpec((tm, tm), const)],
        out_specs=(pl.BlockSpec((TOP_K, tm), tok),
                   pl.BlockSpec((TOP_K, tm), tok),
                   pl.BlockSpec((TOP_K, tm), tok),
                   pl.BlockSpec((E, 1), const)),
        scratch_shapes=[pltpu.VMEM((E, 1), F32)],
        compiler_params=_cparams(("arbitrary",)),
        name="route_topk",
    )(logits_t, b_router, tri)


def _pos_kernel(e_ref, r_ref, ps_ref, o_ref):
    e = e_ref[...]
    tm = e.shape[1]
    ei = lax.broadcasted_iota(I32, (N_EXPERTS, tm), 0)
    ps = ps_ref[...]
    rows = [jnp.sum(jnp.where(ei == e[k:k + 1], ps, 0), axis=0, keepdims=True)
            for k in range(TOP_K)]
    o_ref[...] = jnp.concatenate(rows, axis=0) + r_ref[...]


def _positions(eidx, rank, pstart, tm):
    K, T = eidx.shape
    tok = lambda i: (0, i)
    return pl.pallas_call(
        _pos_kernel,
        out_shape=jax.ShapeDtypeStruct((K, T), I32),
        grid=(T // tm,),
        in_specs=[pl.BlockSpec((K, tm), tok), pl.BlockSpec((K, tm), tok),
                  pl.BlockSpec((N_EXPERTS, 1), lambda i: (0, 0))],
        out_specs=pl.BlockSpec((K, tm), tok),
        compiler_params=_cparams(("parallel",)),
        name="dispatch_positions",
    )(eidx, rank, pstart)


def _sc_mesh():
    return plsc.VectorSubcoreMesh(core_axis_name="c", subcore_axis_name="s")


def _worker_id():
    return lax.axis_index("s") * SC_CORES + lax.axis_index("c")


def _window_positions(pos):
    K, T = pos.shape
    nwin = T // (SC_WORKERS * SC_WINDOW)
    assert nwin * SC_WORKERS * SC_WINDOW == T, T
    return pos.reshape(K, SC_WORKERS, nwin, SC_WINDOW).transpose(1, 2, 0, 3)


def _sc_dispatch(h, pos, P):
    T, D = h.shape
    pos4 = _window_positions(pos)
    NW, nwin, K, W = pos4.shape

    @functools.partial(
        pl.kernel, mesh=_sc_mesh(),
        out_type=jax.ShapeDtypeStruct((P, D), h.dtype),
        scratch_types=[pltpu.VMEM((K, W), I32), pltpu.VMEM((W, D), h.dtype),
                       pltpu.SemaphoreType.DMA],
        name="sc_dispatch")
    def k(h_hbm, pos_hbm, xs_hbm, idx_v, rows_v, sem):
        wid = _worker_id()

        @pl.loop(0, nwin)
        def _(j):
            base = (wid * nwin + j) * W
            pltpu.sync_copy(pos_hbm.at[wid, j], idx_v)
            pltpu.sync_copy(h_hbm.at[pl.ds(base, W)], rows_v)
            copies = [pltpu.async_copy(rows_v, xs_hbm.at[idx_v.at[kk]], sem)
                      for kk in range(K)]
            for c in copies:
                c.wait()

    return k(h, pos4)


def _sc_gather(ys, pos):
    P, D = ys.shape
    T = pos.shape[1]
    pos4 = _window_positions(pos)
    NW, nwin, K, W = pos4.shape

    @functools.partial(
        pl.kernel, mesh=_sc_mesh(),
        out_type=jax.ShapeDtypeStruct((K, T, D), ys.dtype),
        scratch_types=[pltpu.VMEM((K, W), I32), pltpu.VMEM((W, D), ys.dtype),
                       pltpu.SemaphoreType.DMA],
        name="sc_gather")
    def k(ys_hbm, pos_hbm, yg_hbm, idx_v, rows_v, sem):
        wid = _worker_id()

        @pl.loop(0, nwin)
        def _(j):
            base = (wid * nwin + j) * W
            pltpu.sync_copy(pos_hbm.at[wid, j], idx_v)
            for kk in range(K):
                pltpu.async_copy(ys_hbm.at[idx_v.at[kk]], rows_v, sem).wait()
                pltpu.sync_copy(rows_v, yg_hbm.at[kk, pl.ds(base, W)])

    return k(ys, pos4)


def _expert_kernel(be_ref, nv_ref, x_ref, wg_ref, wu_ref, wd_ref, o_ref):
    i = pl.program_id(0)

    @pl.when(i < nv_ref[0])
    def _():
        lo, hi = _unpack_rows(x_ref[...])
        x = jnp.concatenate([lo, hi], axis=1).astype(BF16)
        a = (_silu(jnp.dot(x, wg_ref[0], preferred_element_type=F32))
             * jnp.dot(x, wu_ref[0], preferred_element_type=F32))
        o_ref[...] = _pack_rows(jnp.dot(a.astype(BF16), wd_ref[0], preferred_element_type=F32))


def _experts(xs, blk_e, nvalid, wg, wu, wd):
    P, Dh = xs.shape
    D = 2 * Dh
    bm = EXPERT_BLOCK
    nb = P // bm
    rowmap = lambda i, be, nv: (jnp.minimum(i, nv[0] - 1), 0)
    wmap = lambda i, be, nv: (be[i], 0, 0)
    grid_spec = pltpu.PrefetchScalarGridSpec(
        num_scalar_prefetch=2,
        grid=(nb,),
        in_specs=[pl.BlockSpec((bm, Dh), rowmap),
                  pl.BlockSpec((1, D, EXPERT_DIM), wmap),
                  pl.BlockSpec((1, D, EXPERT_DIM), wmap),
                  pl.BlockSpec((1, EXPERT_DIM, D), wmap)],
        out_specs=pl.BlockSpec((bm, Dh), rowmap),
    )
    return pl.pallas_call(
        _expert_kernel,
        out_shape=jax.ShapeDtypeStruct((P, Dh), I32),
        grid_spec=grid_spec,
        compiler_params=_cparams(("arbitrary",)),
        name="expert_ffn",
    )(blk_e, nvalid, xs, wg, wu, wd)


def _final_kernel(yg_ref, w_ref, h2_ref, x1_ref, mod_ref, modf_ref, nf_ref,
                  sg_ref, su_ref, sd_ref, o_ref):
    w2 = w_ref[...].astype(BF16)
    acc = None
    for k in range(TOP_K):
        term = pltpu.bitcast(yg_ref[k], BF16) * w2[:, k:k + 1]
        acc = term if acc is None else acc + term
    y = jnp.concatenate(_unpack_rows(pltpu.bitcast(acc, I32)), axis=1)
    hb = jnp.concatenate(_unpack_rows(h2_ref[...]), axis=1).astype(BF16)
    a = (_silu(jnp.dot(hb, sg_ref[...], preferred_element_type=F32))
         * jnp.dot(hb, su_ref[...], preferred_element_type=F32))
    shared = jnp.dot(a.astype(BF16), sd_ref[...], preferred_element_type=F32)
    mod = mod_ref[0]
    modf = modf_ref[0]
    x2 = x1_ref[...] + mod[5:6] * (y + shared)
    o_ref[...] = _rms(x2) * nf_ref[...] * (1.0 + modf[1:2]) + modf[0:1]


def _final(yg, wts_t, h2, x1, mod, modf, nfg, sg, su, sd, S, tm):
    T, D = x1.shape
    row = lambda i: (i, 0)
    const = lambda i: (0, 0)
    bat = lambda i: ((i * tm) // S, 0, 0)
    return pl.pallas_call(
        _final_kernel,
        out_shape=jax.ShapeDtypeStruct((T, D), F32),
        grid=(T // tm,),
        in_specs=[pl.BlockSpec((TOP_K, tm, D // 2), lambda i: (0, i, 0)),
                  pl.BlockSpec((2 * tm, TOP_K), row),
                  pl.BlockSpec((tm, D // 2), row),
                  pl.BlockSpec((tm, D), row),
                  pl.BlockSpec((1, 6, D), bat),
                  pl.BlockSpec((1, 2, D), bat),
                  pl.BlockSpec((1, D), const),
                  pl.BlockSpec(sg.shape, const),
                  pl.BlockSpec(su.shape, const),
                  pl.BlockSpec(sd.shape, const)],
        out_specs=pl.BlockSpec((tm, D), row),
        compiler_params=_cparams(("parallel",)),
        name="combine_shared_final",
    )(yg, wts_t, h2, x1, mod, modf, nfg, sg, su, sd)


def _tile(n, pref):
    t = min(n, pref)
    assert n % t == 0, (n, pref)
    return t


def _plan_dispatch(logits_t, p):
    T = logits_t.shape[1]
    tm_r = _tile(T, 512)
    eidx, wts, rank, counts = _route(logits_t, p["b_router"], tm_r)
    bm = EXPERT_BLOCK
    counts = counts[:, 0]
    padded = (counts + bm - 1) // bm * bm
    pends = jnp.cumsum(padded)
    pstart = (pends - padded).astype(I32)
    nb = (T * TOP_K + N_EXPERTS * bm) // bm
    pos = _positions(eidx, rank, pstart[:, None], tm_r)
    blk_e = jnp.minimum(jnp.searchsorted(pends, jnp.arange(nb) * bm, side="right"),
                        N_EXPERTS - 1).astype(I32)
    nvalid = (pends[-1] // bm).astype(I32)
    blk_e = jnp.where(jnp.arange(nb) < nvalid, blk_e, blk_e[jnp.maximum(nvalid - 1, 0)])
    return jnp.repeat(wts.T, 2, axis=0), pos, blk_e, nvalid[None]


NORM_SLACK = 1.02
REF_GAP_LIMIT = 100.0


def _diff_attention(qh, kaug, vh, nrm, posq, p, B, S):
    qmax = jnp.sqrt(jnp.max(nrm[:, 0, 0, :2 * ATT_HEADS])) * NORM_SLACK
    kmax = jnp.sqrt(jnp.max(nrm[:, 1, 0, :2 * ATT_HEADS])) * NORM_SLACK
    in_range = 2.0 * qmax * kmax <= REF_GAP_LIMIT
    scal = jnp.concatenate([-p["slopes"], kmax[None]]).astype(F32)
    tq = _tile(S, 1024)
    tk = 2 * tq if S >= 16 * tq else tq

    def single_pass(_):
        return _attention_ref(qh, kaug, vh, scal, p["lam_p"], p["subln_g"], posq, B, S, tq, tk)

    def online(_):
        return _attention(qh, kaug, vh, p["slopes"], p["lam_p"], p["subln_g"], B, S,
                          _tile(S, 256), _tile(S, 512))

    return lax.cond(in_range, single_pass, online, None)


def _mixer_and_routing(x, mod, p):
    B, S, D = x.shape
    T = B * S
    xt = x.reshape(T, D)
    tm = _tile(S, 256)
    posq, posk = _alibi_tables(p["slopes"], S)
    proj, qh, kaug, vh, nrm = _proj(xt, mod, p["norm1_g"], p["w_in"], posk, S, tm)
    o_att = _diff_attention(qh, kaug, vh, nrm, posq, p, B, S)
    o_fw, o_bw = _hgrn2(proj, p["lbs"], B, S)
    x1, h2, logits_t = _mixout(xt, o_att, o_fw, o_bw, proj, mod, p["hg_norm_g"], p["w_out"],
                               p["norm2_g"], p["w_router"], S, tm)
    wts, pos, blk_e, nvalid = _plan_dispatch(logits_t, p)
    return dict(x1=x1, h2=h2, wts=wts, pos=pos, blk_e=blk_e, nvalid=nvalid, mod=mod, shape=(B, S, D))


def _expert_rows(st, p, after=None):
    T = st["h2"].shape[0]
    xs = _sc_dispatch(st["h2"], st["pos"], T * TOP_K + N_EXPERTS * EXPERT_BLOCK)
    if after is not None:
        xs, _ = lax.optimization_barrier((xs, after))
    return _experts(xs, st["blk_e"], st["nvalid"], p["wg"], p["wu"], p["wd"])


def _combine(st, ys, modf, p):
    B, S, D = st["shape"]
    yg = _sc_gather(ys, st["pos"])
    out = _final(yg, st["wts"], st["h2"], st["x1"], st["mod"], modf, p["normf_g"],
                 p["sg"], p["su"], p["sd"], S, _tile(S, 256))
    return out.reshape(B, S, D)


def _trunk(x, mod, modf, p):
    st = _mixer_and_routing(x, mod, p)
    return _combine(st, _expert_rows(st, p), modf, p)


def kernel(x_prompt, x_sample, c_prompt, c_sample, w_ada, b_ada, norm1_g, w_in, lam_q1, lam_k1, lam_q2, lam_k2, subln_g, hg_lb_logits, hg_norm_g, w_out, norm2_g, w_router, b_router, w_exp_gate, w_exp_up, w_exp_down, w_sh_gate, w_sh_up, w_sh_down, w_ada_f, b_ada_f, normf_g):
    D = x_prompt.shape[-1]
    Bp, Bs = c_prompt.shape[0], c_sample.shape[0]
    c_all = jnp.concatenate([c_prompt, c_sample], axis=0)
    R = -(-c_all.shape[0] // 8) * 8
    c_all = jnp.pad(c_all, ((0, R - c_all.shape[0]), (0, 0)))
    mod6 = _ada(c_all, w_ada[0], b_ada[0][None]).reshape(R, 6, D)
    mod2 = _ada(c_all, w_ada_f, b_ada_f[None]).reshape(R, 2, D)

    lbs = jax.nn.softmax(hg_lb_logits.astype(F32), axis=0)[0]
    lbs = lbs.reshape(2, HG_HEADS, HEAD_W).transpose(1, 0, 2)
    slopes = (jnp.exp2(-8.0 * (jnp.arange(ATT_HEADS, dtype=F32) + 1.0) / ATT_HEADS) * LOG2E)
    p = dict(
        norm1_g=norm1_g[0][None], w_in=w_in[0].astype(BF16),
        slopes=slopes.astype(F32),
        lam_p=jnp.stack([lam_q1[0], lam_k1[0], lam_q2[0], lam_k2[0]]).astype(F32),
        subln_g=subln_g[0][:, None], lbs=lbs, hg_norm_g=hg_norm_g[0][None],
        w_out=w_out[0].astype(BF16), norm2_g=norm2_g[0][None],
        w_router=jnp.concatenate(_split3(w_router[0].astype(F32))[:2], axis=1),
        b_router=b_router[0][:, None],
        wg=w_exp_gate[0].astype(BF16), wu=w_exp_up[0].astype(BF16),
        wd=w_exp_down[0].astype(BF16),
        sg=w_sh_gate[0].astype(BF16), su=w_sh_up[0].astype(BF16), sd=w_sh_down[0].astype(BF16),
        normf_g=normf_g[None],
    )
    st_p = _mixer_and_routing(x_prompt, mod6[:Bp], p)
    x_sample, _ = lax.optimization_barrier((x_sample, st_p["pos"]))
    st_s = _mixer_and_routing(x_sample, mod6[Bp:Bp + Bs], p)
    ys_p = _expert_rows(st_p, p)
    ys_s = _expert_rows(st_s, p, after=ys_p)
    y_prompt = _combine(st_p, ys_p, mod2[:Bp], p)
    y_sample = _combine(st_s, ys_s, mod2[Bp:Bp + Bs], p)
    return (y_prompt, y_sample)
```

```python
import functools
import math

import jax
import jax.numpy as jnp
from jax import lax
from jax.experimental import pallas as pl
from jax.experimental.pallas import tpu as pltpu
from jax.experimental.pallas import tpu_sc as plsc

F32 = jnp.float32
BF16 = jnp.bfloat16
I32 = jnp.int32

EPS = 1e-6
LOG2E = 1.4426950408889634

ATT_HEADS = 4
ATT_QK_DIM = 64
HEAD_W = 128
ATT_WIDTH = ATT_HEADS * HEAD_W
HG_HEADS = 4
HG_WIDTH = HG_HEADS * HEAD_W
N_EXPERTS = 256
TOP_K = 8
N_GROUPS = 8
TOPK_GROUPS = 4
GROUP_SIZE = N_EXPERTS // N_GROUPS
EXPERT_DIM = 256
ROUTE_SCALE = 2.5
LAM_INIT = 0.8 - 0.6 * math.exp(-0.3 * 0)

HG_CHUNK = 128
HG_SUB = 16
HG_MILD_LOG_GATE = 3.75
HG_STEP_CHUNKS = 8
EXPERT_BLOCK = 512
SC_CORES = 2
SC_SUBCORES = 16
SC_WORKERS = SC_CORES * SC_SUBCORES
SC_WINDOW = 32
VMEM_LIMIT = 56 * 1024 * 1024

NEG_INF = float("-inf")


def _cparams(sem):
    return pltpu.CompilerParams(dimension_semantics=sem, vmem_limit_bytes=VMEM_LIMIT)


def _silu(x):
    return x * jax.nn.sigmoid(x)


def _pack_rows(x):
    half = x.shape[1] // 2
    bits = lax.bitcast_convert_type(x.astype(BF16).astype(F32), jnp.uint32)
    return lax.bitcast_convert_type((bits[:, :half] >> 16) | bits[:, half:], I32)


def _unpack_rows(w):
    u = lax.bitcast_convert_type(w, jnp.uint32)
    lo = lax.bitcast_convert_type(u << 16, F32)
    hi = lax.bitcast_convert_type(u & jnp.uint32(0xFFFF0000), F32)
    return lo, hi


def _ada_kernel(c_ref, w_ref, b_ref, o_ref):
    cs = _silu(c_ref[...])
    o_ref[...] = jnp.dot(cs, w_ref[...], preferred_element_type=F32,
                         precision=lax.Precision.HIGHEST) + b_ref[...]


def _ada(c, w, b):
    R, D = c.shape
    N = w.shape[1]
    tn = 1024
    return pl.pallas_call(
        _ada_kernel,
        out_shape=jax.ShapeDtypeStruct((R, N), F32),
        grid=(N // tn,),
        in_specs=[pl.BlockSpec((R, D), lambda j: (0, 0)),
                  pl.BlockSpec((D, tn), lambda j: (0, j)),
                  pl.BlockSpec((1, tn), lambda j: (0, j))],
        out_specs=pl.BlockSpec((R, tn), lambda j: (0, j)),
        compiler_params=_cparams(("arbitrary",)),
        name="ada_mod",
    )(c, w, b)


def _rms(x):
    return x * lax.rsqrt(jnp.mean(x * x, axis=-1, keepdims=True) + EPS)


def _proj_kernel(x_ref, mod_ref, g_ref, w_ref, posk_ref, grp_ref,
                 o_ref, q_ref, ka_ref, v_ref, nrm_ref, *, qscale):
    x = x_ref[...]
    mod = mod_ref[0]
    h = _rms(x) * g_ref[...] * (1.0 + mod[1:2]) + mod[0:1]
    acc = jnp.dot(h.astype(BF16), w_ref[...], preferred_element_type=F32)
    qw = ATT_WIDTH
    qb = (acc[:, :qw] * qscale).astype(BF16)
    kb = acc[:, qw:2 * qw].astype(BF16)
    vb = acc[:, 2 * qw:3 * qw].astype(BF16)
    o_ref[...] = acc[:, 3 * qw:].astype(BF16)
    for hh in range(ATT_HEADS):
        cols = slice(hh * HEAD_W, (hh + 1) * HEAD_W)
        q_ref[hh] = qb[:, cols]
        v_ref[hh] = vb[:, cols]
        ka_ref[hh, :, :HEAD_W] = kb[:, cols]
        ka_ref[hh, :, HEAD_W:] = posk_ref[:, cols]
    for idx, t in enumerate((qb, kb)):
        tf = t.astype(F32)
        n2 = jnp.dot((tf * tf).astype(BF16), grp_ref[...], preferred_element_type=F32)
        nrm_ref[0, idx] = jnp.broadcast_to(jnp.max(n2, axis=0, keepdims=True), (8, HEAD_W))


def _proj(xt, mod, g, w, posk, S, tm):
    T, D = xt.shape
    N = w.shape[1]
    qscale = ATT_QK_DIM ** -0.5 * LOG2E
    nblk = T // tm
    spb = S // tm
    grp = (jnp.arange(ATT_WIDTH)[:, None] // ATT_QK_DIM == jnp.arange(HEAD_W)[None, :]).astype(BF16)
    return pl.pallas_call(
        functools.partial(_proj_kernel, qscale=qscale),
        out_shape=(jax.ShapeDtypeStruct((T, N - 3 * ATT_WIDTH), BF16),
                   jax.ShapeDtypeStruct((ATT_HEADS, T, HEAD_W), BF16),
                   jax.ShapeDtypeStruct((ATT_HEADS, T, 2 * HEAD_W), BF16),
                   jax.ShapeDtypeStruct((ATT_HEADS, T, HEAD_W), BF16),
                   jax.ShapeDtypeStruct((nblk, 2, 8, HEAD_W), F32)),
        grid=(nblk,),
        in_specs=[pl.BlockSpec((tm, D), lambda i: (i, 0)),
                  pl.BlockSpec((1, 6, D), lambda i: ((i * tm) // S, 0, 0)),
                  pl.BlockSpec((1, D), lambda i: (0, 0)),
                  pl.BlockSpec((D, N), lambda i: (0, 0)),
                  pl.BlockSpec((tm, ATT_WIDTH), lambda i: (i % spb, 0)),
                  pl.BlockSpec((ATT_WIDTH, HEAD_W), lambda i: (0, 0))],
        out_specs=(pl.BlockSpec((tm, N - 3 * ATT_WIDTH), lambda i: (i, 0)),
                   pl.BlockSpec((ATT_HEADS, tm, HEAD_W), lambda i: (0, i, 0)),
                   pl.BlockSpec((ATT_HEADS, tm, 2 * HEAD_W), lambda i: (0, i, 0)),
                   pl.BlockSpec((ATT_HEADS, tm, HEAD_W), lambda i: (0, i, 0)),
                   pl.BlockSpec((1, 2, 8, HEAD_W), lambda i: (i, 0, 0, 0))),
        compiler_params=_cparams(("parallel",)),
        name="norm1_proj",
    )(xt, mod, g, w, posk, grp)


def _split3(x):
    hi = x.astype(BF16)
    r = x - hi.astype(F32)
    mid = r.astype(BF16)
    lo = (r - mid.astype(F32)).astype(BF16)
    return hi, mid, lo


def _alibi_tables(slopes2, S):
    H = slopes2.shape[0]
    a = slopes2[:, None] * jnp.arange(S, dtype=F32)[None, :]
    ah, am, al = (t.astype(F32)[..., None] for t in _split3(a))
    lane = jnp.arange(HEAD_W)[None, None, :]
    terms = jnp.where(lane % 3 == 0, ah, jnp.where(lane % 3 == 1, am, al))
    posk = jnp.where(lane < 6, 1.0, jnp.where(lane < 9, terms, 0.0)).astype(BF16)
    posq = jnp.where(lane < 3, 0.0, jnp.where(lane < 6, -terms, jnp.where(lane < 9, 1.0, 0.0))).astype(BF16)
    return posq, posk.transpose(1, 0, 2).reshape(S, H * HEAD_W)


def _attn_kernel(slope_ref, lam_ref, q_ref, k_ref, v_ref, g_ref, o_ref,
                 qm_ref, m_ref, l_ref, acc_ref, *, tq, tk, nk):
    h = pl.program_id(1)
    qi = pl.program_id(2)
    ki = pl.program_id(3)

    @pl.when(ki == 0)
    def _init():
        q = q_ref[...]
        lane = lax.broadcasted_iota(I32, q.shape, 1)
        zero = jnp.zeros_like(q)
        qm_ref[0] = jnp.where(lane < ATT_QK_DIM, q, zero)
        qm_ref[1] = jnp.where(lane >= ATT_QK_DIM, q, zero)
        m_ref[...] = jnp.full(m_ref.shape, NEG_INF, F32)
        l_ref[...] = jnp.zeros(l_ref.shape, F32)
        acc_ref[...] = jnp.zeros(acc_ref.shape, F32)

    slope = slope_ref[h]
    k = k_ref[...]
    v = v_ref[...]
    row = lax.broadcasted_iota(I32, (tk, tq), 0)
    col = lax.broadcasted_iota(I32, (tk, tq), 1)
    dist = (col - row + (qi * tq - ki * tk)).astype(F32)
    bias = jnp.abs(dist) * (-slope)
    for m in range(2):
        s = lax.dot_general(k, qm_ref[m], (((1,), (1,)), ((), ())),
                            preferred_element_type=F32) + bias
        m_old = m_ref[m]
        m_new = jnp.maximum(m_old, jnp.max(s, axis=0, keepdims=True))
        alpha = jnp.exp2(m_old - m_new)
        p = jnp.exp2(s - m_new)
        l_ref[m] = alpha * l_ref[m] + jnp.sum(p, axis=0, keepdims=True)
        pv = lax.dot_general(v, p.astype(BF16), (((0,), (0,)), ((), ())),
                             preferred_element_type=F32)
        acc_ref[m] = alpha * acc_ref[m] + pv
        m_ref[m] = m_new

    @pl.when(ki == nk - 1)
    def _fin():
        lp = lam_ref[...]
        lam = (jnp.exp(jnp.sum(lp[0:1] * lp[1:2], axis=-1, keepdims=True))
               - jnp.exp(jnp.sum(lp[2:3] * lp[3:4], axis=-1, keepdims=True))
               + LAM_INIT)
        o = acc_ref[0] / l_ref[0] - lam * (acc_ref[1] / l_ref[1])
        ms = jnp.mean(o * o, axis=0, keepdims=True)
        y = o * lax.rsqrt(ms + EPS) * g_ref[...] * (1.0 - LAM_INIT)
        o_ref[...] = y.T.astype(o_ref.dtype)


def _attention(qh, kaug, vh, slopes, lam_p, subln_g, B, S, tq, tk):
    T = B * S
    nq, nk = S // tq, S // tk
    grid_spec = pltpu.PrefetchScalarGridSpec(
        num_scalar_prefetch=1,
        grid=(B, ATT_HEADS, nq, nk),
        in_specs=[
            pl.BlockSpec((4, ATT_QK_DIM), lambda b, h, qi, ki, s: (0, 0)),
            pl.BlockSpec((None, tq, HEAD_W), lambda b, h, qi, ki, s: (h, b * nq + qi, 0)),
            pl.BlockSpec((None, tk, HEAD_W), lambda b, h, qi, ki, s: (h, b * nk + ki, 0)),
            pl.BlockSpec((None, tk, HEAD_W), lambda b, h, qi, ki, s: (h, b * nk + ki, 0)),
            pl.BlockSpec((HEAD_W, 1), lambda b, h, qi, ki, s: (0, 0)),
        ],
        out_specs=pl.BlockSpec((tq, HEAD_W), lambda b, h, qi, ki, s: (b * nq + qi, h)),
        scratch_shapes=[pltpu.VMEM((2, tq, HEAD_W), BF16),
                        pltpu.VMEM((2, 1, tq), F32),
                        pltpu.VMEM((2, 1, tq), F32),
                        pltpu.VMEM((2, HEAD_W, tq), F32)],
    )
    return pl.pallas_call(
        functools.partial(_attn_kernel, tq=tq, tk=tk, nk=nk),
        out_shape=jax.ShapeDtypeStruct((T, ATT_WIDTH), BF16),
        grid_spec=grid_spec,
        compiler_params=_cparams(("parallel", "parallel", "parallel", "arbitrary")),
        name="diff_attention",
    )(slopes, lam_p, qh, kaug, vh, subln_g)


def _attn_finish(lam_ref, g_ref, o_ref, l_ref, acc_ref):
    lp = lam_ref[...]
    lam = (jnp.exp(jnp.sum(lp[0:1] * lp[1:2], axis=-1, keepdims=True))
           - jnp.exp(jnp.sum(lp[2:3] * lp[3:4], axis=-1, keepdims=True))
           + LAM_INIT)
    o = acc_ref[0] / l_ref[0] - lam * (acc_ref[1] / l_ref[1])
    ms = jnp.mean(o * o, axis=0, keepdims=True)
    y = o * lax.rsqrt(ms + EPS) * g_ref[...] * (1.0 - LAM_INIT)
    o_ref[...] = y.T.astype(o_ref.dtype)


def _attn_ref_kernel(sc_ref, lam_ref, q_ref, ka_ref, v_ref, g_ref, posq_ref, o_ref,
                     qa_ref, l_ref, acc_ref, *, tq, tk, nk):
    h = pl.program_id(1)
    qi = pl.program_id(2)
    ki = pl.program_id(3)

    @pl.when(ki == 0)
    def _init():
        q = q_ref[...]
        lane = lax.broadcasted_iota(I32, q.shape, 1)
        zero = jnp.zeros_like(q)
        kmax = sc_ref[ATT_HEADS]
        pa = posq_ref[...]
        for m in range(2):
            sel = (lane < ATT_QK_DIM) if m == 0 else (lane >= ATT_QK_DIM)
            qm = jnp.where(sel, q, zero)
            qf = qm.astype(F32)
            ub = jnp.sqrt(jnp.sum(qf * qf, axis=1, keepdims=True)) * kmax
            uh, um, ul = (t.astype(F32) for t in _split3(jnp.broadcast_to(ub, q.shape)))
            ubp = jnp.where(lane == 0, -uh, jnp.where(lane == 1, -um, jnp.where(lane == 2, -ul, 0.0)))
            ubp = ubp.astype(BF16)
            for var, pp in enumerate((pa + ubp, ubp - pa, ubp)):
                qa_ref[2 * var + m] = jnp.concatenate([qm, pp], axis=1)
        l_ref[...] = jnp.zeros(l_ref.shape, F32)
        acc_ref[...] = jnp.zeros(acc_ref.shape, F32)

    rel = qi * tq - ki * tk
    near = jnp.logical_and(rel > -tq, rel < tk)

    def body(with_bias):
        ka = ka_ref[...]
        if with_bias:
            row = lax.broadcasted_iota(I32, (tk, tq), 0)
            col = lax.broadcasted_iota(I32, (tk, tq), 1)
            bias = jnp.abs((col - row + rel).astype(F32)) * sc_ref[h]
            var = 2
        else:
            var = jnp.where(rel > 0, 0, 1)
        ss = [lax.dot_general(ka, qa_ref[2 * var + m], (((1,), (1,)), ((), ())),
                              preferred_element_type=F32) for m in range(2)]
        v = v_ref[...]
        for m in range(2):
            p = jnp.exp2(ss[m] + bias if with_bias else ss[m])
            l_ref[m] += jnp.sum(p, axis=0, keepdims=True)
            acc_ref[m] += lax.dot_general(v, p.astype(BF16), (((0,), (0,)), ((), ())),
                                          preferred_element_type=F32)

    @pl.when(near)
    def _():
        body(True)

    @pl.when(jnp.logical_not(near))
    def _():
        body(False)

    @pl.when(ki == nk - 1)
    def _fin():
        _attn_finish(lam_ref, g_ref, o_ref, l_ref, acc_ref)


def _attention_ref(qh, kaug, vh, scal, lam_p, subln_g, posq, B, S, tq, tk):
    T = B * S
    nq, nk = S // tq, S // tk
    grid_spec = pltpu.PrefetchScalarGridSpec(
        num_scalar_prefetch=1,
        grid=(B, ATT_HEADS, nq, nk),
        in_specs=[
            pl.BlockSpec((4, ATT_QK_DIM), lambda b, h, qi, ki, s: (0, 0)),
            pl.BlockSpec((None, tq, HEAD_W), lambda b, h, qi, ki, s: (h, b * nq + qi, 0)),
            pl.BlockSpec((None, tk, 2 * HEAD_W), lambda b, h, qi, ki, s: (h, b * nk + ki, 0)),
            pl.BlockSpec((None, tk, HEAD_W), lambda b, h, qi, ki, s: (h, b * nk + ki, 0)),
            pl.BlockSpec((HEAD_W, 1), lambda b, h, qi, ki, s: (0, 0)),
            pl.BlockSpec((None, tq, HEAD_W), lambda b, h, qi, ki, s: (h, qi, 0)),
        ],
        out_specs=pl.BlockSpec((tq, HEAD_W), lambda b, h, qi, ki, s: (b * nq + qi, h)),
        scratch_shapes=[pltpu.VMEM((6, tq, 2 * HEAD_W), BF16),
                        pltpu.VMEM((2, 1, tq), F32),
                        pltpu.VMEM((2, HEAD_W, tq), F32)],
    )
    return pl.pallas_call(
        functools.partial(_attn_ref_kernel, tq=tq, tk=tk, nk=nk),
        out_shape=jax.ShapeDtypeStruct((T, ATT_WIDTH), BF16),
        grid_spec=grid_spec,
        compiler_params=_cparams(("parallel", "parallel", "parallel", "arbitrary")),
        name="diff_attention_ref",
    )(scal, lam_p, qh, kaug, vh, subln_g, posq)


def _hg_chunk(q, kk, v, g, st, reverse):
    C = q.shape[0]
    nsub = C // HG_SUB
    row = lax.broadcasted_iota(I32, (C, HEAD_W), 0)
    b = _hg_cumsum(g, reverse)
    b_end = b[C - 1:C] if not reverse else b[0:1]

    qd = (q * jnp.exp(b)).astype(BF16)
    o_inter = lax.dot_general(qd, st.astype(BF16), (((1,), (1,)), ((), ())),
                              preferred_element_type=F32)

    ones = jnp.ones((HEAD_W, HEAD_W), BF16)
    srow = lax.broadcasted_iota(I32, (HG_SUB, HEAD_W), 0)
    vb = v.astype(BF16)
    outs = []
    for blk in range(nsub):
        lo, hi = blk * HG_SUB, (blk + 1) * HG_SUB
        qI, kI, vI, bI = q[lo:hi], kk[lo:hi], v[lo:hi], b[lo:hi]
        pieces = []
        for t in range(HG_SUB):
            dl = bI[t:t + 1] - bI
            keep = (srow <= t) if not reverse else (srow >= t)
            e = jnp.exp(jnp.where(keep, dl, NEG_INF))
            pieces.append(qI[t:t + 1] * kI * e)
        wst = jnp.concatenate(pieces, axis=0).astype(BF16)
        rsum = jnp.dot(wst, ones, preferred_element_type=F32)
        o_blk = jnp.sum(rsum.reshape(HG_SUB, HG_SUB, HEAD_W) * vI[None], axis=1)
        has_off = (blk > 0) if not reverse else (blk < nsub - 1)
        if has_off:
            if not reverse:
                r = b[lo - 1:lo]
                kmask = row < lo
            else:
                r = b[hi:hi + 1]
                kmask = row >= hi
            qs = (qI * jnp.exp(bI - r)).astype(BF16)
            ks = (kk * jnp.exp(jnp.where(kmask, r - b, NEG_INF))).astype(BF16)
            a = lax.dot_general(qs, ks, (((1,), (1,)), ((), ())),
                                preferred_element_type=F32)
            o_blk = o_blk + jnp.dot(a.astype(BF16), vb, preferred_element_type=F32)
        outs.append(o_blk)
    o = o_inter + jnp.concatenate(outs, axis=0)

    kd = (kk * jnp.exp(b_end - b)).astype(BF16)
    upd = lax.dot_general(vb, kd, (((0,), (0,)), ((), ())),
                          preferred_element_type=F32)
    st_new = st * jnp.exp(b_end) + upd
    return o, st_new


def _hg_cumsum(g, reverse):
    C = g.shape[0]
    row = lax.broadcasted_iota(I32, g.shape, 0)
    b = g
    d = 1
    while d < C:
        if not reverse:
            b = b + jnp.where(row >= d, pltpu.roll(b, d, axis=0), 0.0)
        else:
            b = b + jnp.where(row < C - d, pltpu.roll(b, C - d, axis=0), 0.0)
        d *= 2
    return b


def _hg_chunk_mild(q, kk, v, g, st, reverse):
    C = q.shape[0]
    nsub = C // HG_SUB
    row = lax.broadcasted_iota(I32, (C, HEAD_W), 0)
    b = _hg_cumsum(g, reverse)
    b_end = b[C - 1:C] if not reverse else b[0:1]
    qd = (q * jnp.exp(b)).astype(BF16)
    o_inter = lax.dot_general(qd, st.astype(BF16), (((1,), (1,)), ((), ())),
                              preferred_element_type=F32)
    vb = v.astype(BF16)
    qrow = lax.broadcasted_iota(I32, (HG_SUB, C), 0)
    kcol = lax.broadcasted_iota(I32, (HG_SUB, C), 1)
    zero_ref = jnp.zeros((1, HEAD_W), F32)
    a_rows = []
    for blk in range(nsub):
        lo, hi = blk * HG_SUB, (blk + 1) * HG_SUB
        if not reverse:
            r = b[lo - 1:lo] if blk > 0 else zero_ref
            kmask = row < hi
            amask = kcol <= qrow + lo
        else:
            r = b[hi:hi + 1] if blk < nsub - 1 else zero_ref
            kmask = row >= lo
            amask = kcol >= qrow + lo
        qs = (q[lo:hi] * jnp.exp(b[lo:hi] - r)).astype(BF16)
        ks = (kk * jnp.exp(jnp.where(kmask, r - b, NEG_INF))).astype(BF16)
        a = lax.dot_general(qs, ks, (((1,), (1,)), ((), ())), preferred_element_type=F32)
        a_rows.append(jnp.where(amask, a, 0.0))
    a_full = jnp.concatenate(a_rows, axis=0).astype(BF16)
    o = o_inter + jnp.dot(a_full, vb, preferred_element_type=F32)
    kd = (kk * jnp.exp(b_end - b)).astype(BF16)
    upd = lax.dot_general(vb, kd, (((0,), (0,)), ((), ())), preferred_element_type=F32)
    return o, st * jnp.exp(b_end) + upd


def _hg_kernel(qf_ref, ff_ref, if_ref, qb_ref, fb_ref, ib_ref, lb_ref, thr_ref,
               of_ref, ob_ref, sf_ref, sb_ref, *, nchunk):
    j = pl.program_id(2)

    @pl.when(j == 0)
    def _init():
        sf_ref[...] = jnp.zeros(sf_ref.shape, F32)
        sb_ref[...] = jnp.zeros(sb_ref.shape, F32)

    lb = lb_ref[0]
    C = HG_CHUNK

    def prep(q_ref, f_ref, i_ref, lbd, rows):
        q = _silu(q_ref[rows, :].astype(F32))
        f = lbd + (1.0 - lbd) * jax.nn.sigmoid(f_ref[rows, :].astype(F32))
        return q, 1.0 - f, i_ref[rows, :].astype(F32), jnp.log(f)

    def step(chunk_fn, c, carry):
        rows_f = pl.ds(pl.multiple_of(c * C, C), C)
        rows_b = pl.ds(pl.multiple_of((nchunk - 1 - c) * C, C), C)
        o, st = chunk_fn(*prep(qf_ref, ff_ref, if_ref, lb[0:1], rows_f), sf_ref[...], reverse=False)
        of_ref[rows_f, :] = o
        sf_ref[...] = st
        o, st = chunk_fn(*prep(qb_ref, fb_ref, ib_ref, lb[1:2], rows_b), sb_ref[...], reverse=True)
        ob_ref[rows_b, :] = o
        sb_ref[...] = st
        return carry

    thr = thr_ref[0]
    above = jnp.minimum(
        jnp.min(jnp.where(ff_ref[...].astype(F32) >= thr[0:1], 1.0, 0.0)),
        jnp.min(jnp.where(fb_ref[...].astype(F32) >= thr[1:2], 1.0, 0.0)))
    mild = above > 0.5

    @pl.when(mild)
    def _():
        lax.fori_loop(0, nchunk, functools.partial(step, _hg_chunk_mild), 0)

    @pl.when(jnp.logical_not(mild))
    def _():
        lax.fori_loop(0, nchunk, functools.partial(step, _hg_chunk), 0)


def _hgrn2(proj, lbs, B, S):
    T = B * S
    nchunk = min(HG_STEP_CHUNKS, S // HG_CHUNK)
    C = HG_CHUNK * nchunk
    n = S // C
    nh = HG_HEADS
    qc, ffc, fbc, ic = 0, nh, 2 * nh, 3 * nh

    def fw(col):
        return pl.BlockSpec((C, HEAD_W), lambda b, h, j: (b * n + j, col + h))

    def bw(col):
        return pl.BlockSpec((C, HEAD_W), lambda b, h, j: (b * n + n - 1 - j, col + h))

    out_f = pl.BlockSpec((C, HEAD_W), lambda b, h, j: (b * n + j, h))
    out_b = pl.BlockSpec((C, HEAD_W), lambda b, h, j: (b * n + n - 1 - j, h))
    need = (math.exp(-HG_MILD_LOG_GATE) - lbs) / (1.0 - lbs)
    thr = jnp.where(need > 0.0, jnp.log(jnp.maximum(need, 1e-30) / (1.0 - need)), NEG_INF).astype(F32)
    per_head = pl.BlockSpec((1, 2, HEAD_W), lambda b, h, j: (h, 0, 0))
    return pl.pallas_call(
        functools.partial(_hg_kernel, nchunk=nchunk),
        out_shape=(jax.ShapeDtypeStruct((T, HG_WIDTH), F32),
                   jax.ShapeDtypeStruct((T, HG_WIDTH), F32)),
        grid=(B, nh, n),
        in_specs=[fw(qc), fw(ffc), fw(ic), bw(qc), bw(fbc), bw(ic), per_head, per_head],
        out_specs=(out_f, out_b),
        scratch_shapes=[pltpu.VMEM((HEAD_W, HEAD_W), F32),
                        pltpu.VMEM((HEAD_W, HEAD_W), F32)],
        compiler_params=_cparams(("parallel", "parallel", "arbitrary")),
        name="hgrn2_scan",
    )(proj, proj, proj, proj, proj, proj, lbs, thr)


def _mixout_kernel(x_ref, oa_ref, of_ref, ob_ref, gate_ref, mod_ref, hgg_ref, wo_ref,
                   n2_ref, wr_ref, x1_ref, h2_ref, lg_ref):
    mod = mod_ref[0]
    o = of_ref[...] + ob_ref[...]
    gate = _silu(gate_ref[...].astype(F32))
    hg = jnp.concatenate(
        [_rms(o[:, h * HEAD_W:(h + 1) * HEAD_W]) * hgg_ref[...] for h in range(HG_HEADS)],
        axis=-1) * gate
    mix = (jnp.dot(oa_ref[...], wo_ref[:ATT_WIDTH, :], preferred_element_type=F32)
           + jnp.dot(hg.astype(BF16), wo_ref[ATT_WIDTH:, :], preferred_element_type=F32))
    x1 = x_ref[...] + mod[2:3] * mix
    x1_ref[...] = x1
    h2 = _rms(x1) * n2_ref[...] * (1.0 + mod[4:5]) + mod[3:4]
    h2_ref[...] = _pack_rows(h2)
    h_hi = h2.astype(BF16)
    h_mid = (h2 - h_hi.astype(F32)).astype(BF16)
    two = jnp.dot(h_hi, wr_ref[...], preferred_element_type=F32)
    logits = (two[:, :N_EXPERTS] + two[:, N_EXPERTS:]
              + jnp.dot(h_mid, wr_ref[:, :N_EXPERTS], preferred_element_type=F32))
    lg_ref[...] = logits.T


def _mixout(xt, o_att, o_fw, o_bw, proj, mod, hg_g, w_out, n2g, w_router, S, tm):
    T, D = xt.shape
    gate_col = proj.shape[1] // HG_WIDTH - 1
    row = lambda i: (i, 0)
    const = lambda i: (0, 0)
    return pl.pallas_call(
        _mixout_kernel,
        out_shape=(jax.ShapeDtypeStruct((T, D), F32),
                   jax.ShapeDtypeStruct((T, D // 2), I32),
                   jax.ShapeDtypeStruct((N_EXPERTS, T), F32)),
        grid=(T // tm,),
        in_specs=[pl.BlockSpec((tm, D), row),
                  pl.BlockSpec((tm, ATT_WIDTH), row),
                  pl.BlockSpec((tm, HG_WIDTH), row),
                  pl.BlockSpec((tm, HG_WIDTH), row),
                  pl.BlockSpec((tm, HG_WIDTH), lambda i: (i, gate_col)),
                  pl.BlockSpec((1, 6, D), lambda i: ((i * tm) // S, 0, 0)),
                  pl.BlockSpec((1, HEAD_W), const),
                  pl.BlockSpec(w_out.shape, const),
                  pl.BlockSpec((1, D), const),
                  pl.BlockSpec(w_router.shape, const)],
        out_specs=(pl.BlockSpec((tm, D), row),
                   pl.BlockSpec((tm, D // 2), row),
                   pl.BlockSpec((N_EXPERTS, tm), lambda i: (0, i))),
        compiler_params=_cparams(("parallel",)),
        name="mixout_norm2_router",
    )(xt, o_att, o_fw, o_bw, proj, mod, hg_g, w_out, n2g, w_router)


def _first_argmax(x, iota, size):
    mx = jnp.max(x, axis=0, keepdims=True)
    idx = jnp.min(jnp.where(x == mx, iota, size), axis=0, keepdims=True)
    return mx, idx


def _route_kernel(lg_ref, br_ref, tri_ref, e_ref, w_ref, r_ref, cnt_ref, carry_ref):
    i = pl.program_id(0)

    @pl.when(i == 0)
    def _init():
        carry_ref[...] = jnp.zeros(carry_ref.shape, F32)

    scores = jax.nn.sigmoid(lg_ref[...])
    biased = scores + br_ref[...]
    tm = scores.shape[1]
    giota = lax.broadcasted_iota(I32, (GROUP_SIZE, tm), 0)
    gs = []
    for g in range(N_GROUPS):
        blk = biased[g * GROUP_SIZE:(g + 1) * GROUP_SIZE]
        m1, i1 = _first_argmax(blk, giota, GROUP_SIZE)
        m2 = jnp.max(jnp.where(giota == i1, NEG_INF, blk), axis=0, keepdims=True)
        gs.append(m1 + m2)
    gsc = jnp.concatenate(gs, axis=0)
    gi = lax.broadcasted_iota(I32, (N_GROUPS, tm), 0)
    gsel = jnp.zeros((N_GROUPS, tm), jnp.bool_)
    for _ in range(TOPK_GROUPS):
        _, idx = _first_argmax(gsc, gi, N_GROUPS)
        hit = gi == idx
        gsel = jnp.logical_or(gsel, hit)
        gsc = jnp.where(hit, NEG_INF, gsc)
    masked = jnp.concatenate(
        [jnp.where(gsel[g:g + 1], biased[g * GROUP_SIZE:(g + 1) * GROUP_SIZE], NEG_INF)
         for g in range(N_GROUPS)], axis=0)
    ei = lax.broadcasted_iota(I32, (N_EXPERTS, tm), 0)
    eidx, wts = [], []
    onehot = jnp.zeros((N_EXPERTS, tm), F32)
    for _ in range(TOP_K):
        _, idx = _first_argmax(masked, ei, N_EXPERTS)
        hit = ei == idx
        eidx.append(idx)
        wts.append(jnp.sum(jnp.where(hit, scores, 0.0), axis=0, keepdims=True))
        onehot = jnp.where(hit, 1.0, onehot)
        masked = jnp.where(hit, NEG_INF, masked)
    w = jnp.concatenate(wts, axis=0)
    w = w / (jnp.sum(w, axis=0, keepdims=True) + 1e-20) * ROUTE_SCALE
    e_ref[...] = jnp.concatenate(eidx, axis=0)
    w_ref[...] = w
    before = jnp.dot(onehot.astype(BF16), tri_ref[...], preferred_element_type=F32)
    before = before + carry_ref[...]
    ranks = [jnp.sum(jnp.where(ei == idx, before, 0.0), axis=0, keepdims=True) for idx in eidx]
    r_ref[...] = jnp.concatenate(ranks, axis=0).astype(I32)
    carry = carry_ref[...] + jnp.sum(onehot, axis=1, keepdims=True)
    carry_ref[...] = carry
    cnt_ref[...] = carry.astype(I32)


def _route(logits_t, b_router, tm):
    E, T = logits_t.shape
    tri = (jnp.arange(tm)[:, None] < jnp.arange(tm)[None, :]).astype(BF16)
    tok = lambda i: (0, i)
    const = lambda i: (0, 0)
    return pl.pallas_call(
        _route_kernel,
        out_shape=(jax.ShapeDtypeStruct((TOP_K, T), I32),
                   jax.ShapeDtypeStruct((TOP_K, T), F32),
                   jax.ShapeDtypeStruct((TOP_K, T), I32),
                   jax.ShapeDtypeStruct((E, 1), I32)),
        grid=(T // tm,),
        in_specs=[pl.BlockSpec((E, tm), tok),
                  pl.BlockSpec((E, 1), const),
                  pl.BlockSpec((tm, tm), const)],
        out_specs=(pl.BlockSpec((TOP_K, tm), tok),
                   pl.BlockSpec((TOP_K, tm), tok),
                   pl.BlockSpec((TOP_K, tm), tok),
                   pl.BlockSpec((E, 1), const)),
        scratch_shapes=[pltpu.VMEM((E, 1), F32)],
        compiler_params=_cparams(("arbitrary",)),
        name="route_topk",
    )(logits_t, b_router, tri)


def _pos_kernel(e_ref, r_ref, ps_ref, o_ref):
    e = e_ref[...]
    tm = e.shape[1]
    ei = lax.broadcasted_iota(I32, (N_EXPERTS, tm), 0)
    ps = ps_ref[...]
    rows = [jnp.sum(jnp.where(ei == e[k:k + 1], ps, 0), axis=0, keepdims=True)
            for k in range(TOP_K)]
    o_ref[...] = jnp.concatenate(rows, axis=0) + r_ref[...]


def _positions(eidx, rank, pstart, tm):
    K, T = eidx.shape
    tok = lambda i: (0, i)
    return pl.pallas_call(
        _pos_kernel,
        out_shape=jax.ShapeDtypeStruct((K, T), I32),
        grid=(T // tm,),
        in_specs=[pl.BlockSpec((K, tm), tok), pl.BlockSpec((K, tm), tok),
                  pl.BlockSpec((N_EXPERTS, 1), lambda i: (0, 0))],
        out_specs=pl.BlockSpec((K, tm), tok),
        compiler_params=_cparams(("parallel",)),
        name="dispatch_positions",
    )(eidx, rank, pstart)


def _sc_mesh():
    return plsc.VectorSubcoreMesh(core_axis_name="c", subcore_axis_name="s")


def _worker_id():
    return lax.axis_index("s") * SC_CORES + lax.axis_index("c")


def _window_positions(pos):
    K, T = pos.shape
    nwin = T // (SC_WORKERS * SC_WINDOW)
    assert nwin * SC_WORKERS * SC_WINDOW == T, T
    return pos.reshape(K, SC_WORKERS, nwin, SC_WINDOW).transpose(1, 2, 0, 3)


def _sc_dispatch(h, pos, P):
    T, D = h.shape
    pos4 = _window_positions(pos)
    NW, nwin, K, W = pos4.shape

    @functools.partial(
        pl.kernel, mesh=_sc_mesh(),
        out_type=jax.ShapeDtypeStruct((P, D), h.dtype),
        scratch_types=[pltpu.VMEM((K, W), I32), pltpu.VMEM((W, D), h.dtype),
                       pltpu.SemaphoreType.DMA],
        name="sc_dispatch")
    def k(h_hbm, pos_hbm, xs_hbm, idx_v, rows_v, sem):
        wid = _worker_id()

        @pl.loop(0, nwin)
        def _(j):
            base = (wid * nwin + j) * W
            pltpu.sync_copy(pos_hbm.at[wid, j], idx_v)
            pltpu.sync_copy(h_hbm.at[pl.ds(base, W)], rows_v)
            copies = [pltpu.async_copy(rows_v, xs_hbm.at[idx_v.at[kk]], sem)
                      for kk in range(K)]
            for c in copies:
                c.wait()

    return k(h, pos4)


def _sc_gather(ys, pos):
    P, D = ys.shape
    T = pos.shape[1]
    pos4 = _window_positions(pos)
    NW, nwin, K, W = pos4.shape

    @functools.partial(
        pl.kernel, mesh=_sc_mesh(),
        out_type=jax.ShapeDtypeStruct((K, T, D), ys.dtype),
        scratch_types=[pltpu.VMEM((K, W), I32), pltpu.VMEM((W, D), ys.dtype),
                       pltpu.SemaphoreType.DMA],
        name="sc_gather")
    def k(ys_hbm, pos_hbm, yg_hbm, idx_v, rows_v, sem):
        wid = _worker_id()

        @pl.loop(0, nwin)
        def _(j):
            base = (wid * nwin + j) * W
            pltpu.sync_copy(pos_hbm.at[wid, j], idx_v)
            for kk in range(K):
                pltpu.async_copy(ys_hbm.at[idx_v.at[kk]], rows_v, sem).wait()
                pltpu.sync_copy(rows_v, yg_hbm.at[kk, pl.ds(base, W)])

    return k(ys, pos4)


def _expert_kernel(be_ref, nv_ref, x_ref, wg_ref, wu_ref, wd_ref, o_ref):
    i = pl.program_id(0)

    @pl.when(i < nv_ref[0])
    def _():
        lo, hi = _unpack_rows(x_ref[...])
        x = jnp.concatenate([lo, hi], axis=1).astype(BF16)
        a = (_silu(jnp.dot(x, wg_ref[0].astype(BF16), preferred_element_type=F32))
             * jnp.dot(x, wu_ref[0].astype(BF16), preferred_element_type=F32))
        o_ref[...] = _pack_rows(jnp.dot(a.astype(BF16), wd_ref[0].astype(BF16),
                                        preferred_element_type=F32))


def _experts(xs, blk_e, nvalid, wg, wu, wd):
    P, Dh = xs.shape
    D = 2 * Dh
    bm = EXPERT_BLOCK
    nb = P // bm
    rowmap = lambda i, be, nv: (jnp.minimum(i, nv[0] - 1), 0)
    wmap = lambda i, be, nv: (be[i], 0, 0)
    grid_spec = pltpu.PrefetchScalarGridSpec(
        num_scalar_prefetch=2,
        grid=(nb,),
        in_specs=[pl.BlockSpec((bm, Dh), rowmap),
                  pl.BlockSpec((1, D, EXPERT_DIM), wmap),
                  pl.BlockSpec((1, D, EXPERT_DIM), wmap),
                  pl.BlockSpec((1, EXPERT_DIM, D), wmap)],
        out_specs=pl.BlockSpec((bm, Dh), rowmap),
    )
    return pl.pallas_call(
        _expert_kernel,
        out_shape=jax.ShapeDtypeStruct((P, Dh), I32),
        grid_spec=grid_spec,
        compiler_params=_cparams(("arbitrary",)),
        name="expert_ffn",
    )(blk_e, nvalid, xs, wg, wu, wd)


def _final_kernel(yg_ref, w_ref, h2_ref, x1_ref, mod_ref, modf_ref, nf_ref,
                  sg_ref, su_ref, sd_ref, o_ref):
    w2 = w_ref[...].astype(BF16)
    acc = None
    for k in range(TOP_K):
        term = pltpu.bitcast(yg_ref[k], BF16) * w2[:, k:k + 1]
        acc = term if acc is None else acc + term
    y = jnp.concatenate(_unpack_rows(pltpu.bitcast(acc, I32)), axis=1)
    hb = jnp.concatenate(_unpack_rows(h2_ref[...]), axis=1).astype(BF16)
    a = (_silu(jnp.dot(hb, sg_ref[...], preferred_element_type=F32))
         * jnp.dot(hb, su_ref[...], preferred_element_type=F32))
    shared = jnp.dot(a.astype(BF16), sd_ref[...], preferred_element_type=F32)
    mod = mod_ref[0]
    modf = modf_ref[0]
    x2 = x1_ref[...] + mod[5:6] * (y + shared)
    o_ref[...] = _rms(x2) * nf_ref[...] * (1.0 + modf[1:2]) + modf[0:1]


def _final(yg, wts_t, h2, x1, mod, modf, nfg, sg, su, sd, S, tm):
    T, D = x1.shape
    row = lambda i: (i, 0)
    const = lambda i: (0, 0)
    bat = lambda i: ((i * tm) // S, 0, 0)
    return pl.pallas_call(
        _final_kernel,
        out_shape=jax.ShapeDtypeStruct((T, D), F32),
        grid=(T // tm,),
        in_specs=[pl.BlockSpec((TOP_K, tm, D // 2), lambda i: (0, i, 0)),
                  pl.BlockSpec((2 * tm, TOP_K), row),
                  pl.BlockSpec((tm, D // 2), row),
                  pl.BlockSpec((tm, D), row),
                  pl.BlockSpec((1, 6, D), bat),
                  pl.BlockSpec((1, 2, D), bat),
                  pl.BlockSpec((1, D), const),
                  pl.BlockSpec(sg.shape, const),
                  pl.BlockSpec(su.shape, const),
                  pl.BlockSpec(sd.shape, const)],
        out_specs=pl.BlockSpec((tm, D), row),
        compiler_params=_cparams(("parallel",)),
        name="combine_shared_final",
    )(yg, wts_t, h2, x1, mod, modf, nfg, sg, su, sd)


def _tile(n, pref):
    t = min(n, pref)
    assert n % t == 0, (n, pref)
    return t


def _plan_dispatch(logits_t, p):
    T = logits_t.shape[1]
    tm_r = _tile(T, 512)
    eidx, wts, rank, counts = _route(logits_t, p["b_router"], tm_r)
    bm = EXPERT_BLOCK
    counts = counts[:, 0]
    padded = (counts + bm - 1) // bm * bm
    pends = jnp.cumsum(padded)
    pstart = (pends - padded).astype(I32)
    nb = (T * TOP_K + N_EXPERTS * bm) // bm
    pos = _positions(eidx, rank, pstart[:, None], tm_r)
    blk_e = jnp.minimum(jnp.searchsorted(pends, jnp.arange(nb) * bm, side="right"),
                        N_EXPERTS - 1).astype(I32)
    nvalid = (pends[-1] // bm).astype(I32)
    blk_e = jnp.where(jnp.arange(nb) < nvalid, blk_e, blk_e[jnp.maximum(nvalid - 1, 0)])
    return jnp.repeat(wts.T, 2, axis=0), pos, blk_e, nvalid[None]


NORM_SLACK = 1.02
REF_GAP_LIMIT = 100.0


def _diff_attention(qh, kaug, vh, nrm, posq, p, B, S):
    qmax = jnp.sqrt(jnp.max(nrm[:, 0, 0, :2 * ATT_HEADS])) * NORM_SLACK
    kmax = jnp.sqrt(jnp.max(nrm[:, 1, 0, :2 * ATT_HEADS])) * NORM_SLACK
    in_range = 2.0 * qmax * kmax <= REF_GAP_LIMIT
    scal = jnp.concatenate([-p["slopes"], kmax[None]]).astype(F32)
    tq = _tile(S, 1024)
    tk = 2 * tq if S >= 16 * tq else tq

    def single_pass(_):
        return _attention_ref(qh, kaug, vh, scal, p["lam_p"], p["subln_g"], posq, B, S, tq, tk)

    def online(_):
        return _attention(qh, kaug, vh, p["slopes"], p["lam_p"], p["subln_g"], B, S,
                          _tile(S, 256), _tile(S, 512))

    return lax.cond(in_range, single_pass, online, None)


def _mixer_and_routing(x, mod, p):
    B, S, D = x.shape
    T = B * S
    xt = x.reshape(T, D)
    tm = _tile(S, 256)
    posq, posk = _alibi_tables(p["slopes"], S)
    proj, qh, kaug, vh, nrm = _proj(xt, mod, p["norm1_g"], p["w_in"], posk, S, tm)
    o_att = _diff_attention(qh, kaug, vh, nrm, posq, p, B, S)
    o_fw, o_bw = _hgrn2(proj, p["lbs"], B, S)
    x1, h2, logits_t = _mixout(xt, o_att, o_fw, o_bw, proj, mod, p["hg_norm_g"], p["w_out"],
                               p["norm2_g"], p["w_router"], S, tm)
    wts, pos, blk_e, nvalid = _plan_dispatch(logits_t, p)
    return dict(x1=x1, h2=h2, wts=wts, pos=pos, blk_e=blk_e, nvalid=nvalid, mod=mod, shape=(B, S, D))


def _expert_rows(st, p, after=None):
    T = st["h2"].shape[0]
    xs = _sc_dispatch(st["h2"], st["pos"], T * TOP_K + N_EXPERTS * EXPERT_BLOCK)
    if after is not None:
        xs, _ = lax.optimization_barrier((xs, after))
    return _experts(xs, st["blk_e"], st["nvalid"], p["wg"], p["wu"], p["wd"])


def _combine(st, ys, modf, p):
    B, S, D = st["shape"]
    yg = _sc_gather(ys, st["pos"])
    out = _final(yg, st["wts"], st["h2"], st["x1"], st["mod"], modf, p["normf_g"],
                 p["sg"], p["su"], p["sd"], S, _tile(S, 256))
    return out.reshape(B, S, D)


def _trunk(x, mod, modf, p):
    st = _mixer_and_routing(x, mod, p)
    return _combine(st, _expert_rows(st, p), modf, p)


def kernel(x_prompt, x_sample, c_prompt, c_sample, w_ada, b_ada, norm1_g, w_in, lam_q1, lam_k1, lam_q2, lam_k2, subln_g, hg_lb_logits, hg_norm_g, w_out, norm2_g, w_router, b_router, w_exp_gate, w_exp_up, w_exp_down, w_sh_gate, w_sh_up, w_sh_down, w_ada_f, b_ada_f, normf_g):
    D = x_prompt.shape[-1]
    Bp, Bs = c_prompt.shape[0], c_sample.shape[0]
    c_all = jnp.concatenate([c_prompt, c_sample], axis=0)
    R = -(-c_all.shape[0] // 8) * 8
    c_all = jnp.pad(c_all, ((0, R - c_all.shape[0]), (0, 0)))
    mod6 = _ada(c_all, w_ada[0], b_ada[0][None]).reshape(R, 6, D)
    mod2 = _ada(c_all, w_ada_f, b_ada_f[None]).reshape(R, 2, D)

    lbs = jax.nn.softmax(hg_lb_logits.astype(F32), axis=0)[0]
    lbs = lbs.reshape(2, HG_HEADS, HEAD_W).transpose(1, 0, 2)
    slopes = (jnp.exp2(-8.0 * (jnp.arange(ATT_HEADS, dtype=F32) + 1.0) / ATT_HEADS) * LOG2E)
    p = dict(
        norm1_g=norm1_g[0][None], w_in=w_in[0].astype(BF16),
        slopes=slopes.astype(F32),
        lam_p=jnp.stack([lam_q1[0], lam_k1[0], lam_q2[0], lam_k2[0]]).astype(F32),
        subln_g=subln_g[0][:, None], lbs=lbs, hg_norm_g=hg_norm_g[0][None],
        w_out=w_out[0].astype(BF16), norm2_g=norm2_g[0][None],
        w_router=jnp.concatenate(_split3(w_router[0].astype(F32))[:2], axis=1),
        b_router=b_router[0][:, None],
        wg=w_exp_gate[0], wu=w_exp_up[0], wd=w_exp_down[0],
        sg=w_sh_gate[0].astype(BF16), su=w_sh_up[0].astype(BF16), sd=w_sh_down[0].astype(BF16),
        normf_g=normf_g[None],
    )
    st_p = _mixer_and_routing(x_prompt, mod6[:Bp], p)
    x_sample, _ = lax.optimization_barrier((x_sample, st_p["pos"]))
    st_s = _mixer_and_routing(x_sample, mod6[Bp:Bp + Bs], p)
    ys_p = _expert_rows(st_p, p)
    ys_s = _expert_rows(st_s, p, after=ys_p)
    y_prompt = _combine(st_p, ys_p, mod2[:Bp], p)
    y_sample = _combine(st_s, ys_s, mod2[Bp:Bp + Bs], p)
    return (y_prompt, y_sample)
```

```python
import functools
import math

import jax
import jax.numpy as jnp
from jax import lax
from jax.experimental import pallas as pl
from jax.experimental.pallas import tpu as pltpu
from jax.experimental.pallas import tpu_sc as plsc

F32 = jnp.float32
BF16 = jnp.bfloat16
I32 = jnp.int32

EPS = 1e-6
LOG2E = 1.4426950408889634

ATT_HEADS = 4
ATT_QK_DIM = 64
HEAD_W = 128
ATT_WIDTH = ATT_HEADS * HEAD_W
HG_HEADS = 4
HG_WIDTH = HG_HEADS * HEAD_W
N_EXPERTS = 256
TOP_K = 8
N_GROUPS = 8
TOPK_GROUPS = 4
GROUP_SIZE = N_EXPERTS // N_GROUPS
EXPERT_DIM = 256
ROUTE_SCALE = 2.5
LAM_INIT = 0.8 - 0.6 * math.exp(-0.3 * 0)

HG_CHUNK = 128
HG_SUB = 16
HG_MILD_LOG_GATE = 3.75
HG_STEP_CHUNKS = 8
EXPERT_BLOCK = 512
MIXOUT_SUB_ROWS = 256
SC_CORES = 2
SC_SUBCORES = 16
SC_WORKERS = SC_CORES * SC_SUBCORES
SC_WINDOW = 32
VMEM_LIMIT = 56 * 1024 * 1024

NEG_INF = float("-inf")


def _cparams(sem):
    return pltpu.CompilerParams(dimension_semantics=sem, vmem_limit_bytes=VMEM_LIMIT)


def _silu(x):
    return x * jax.nn.sigmoid(x)


def _pack_rows(x):
    half = x.shape[1] // 2
    bits = lax.bitcast_convert_type(x.astype(BF16).astype(F32), jnp.uint32)
    return lax.bitcast_convert_type((bits[:, :half] >> 16) | bits[:, half:], I32)


def _unpack_rows(w):
    u = lax.bitcast_convert_type(w, jnp.uint32)
    lo = lax.bitcast_convert_type(u << 16, F32)
    hi = lax.bitcast_convert_type(u & jnp.uint32(0xFFFF0000), F32)
    return lo, hi


def _ada_kernel(c_ref, w_ref, b_ref, o_ref):
    cs = _silu(c_ref[...])
    o_ref[...] = jnp.dot(cs, w_ref[...], preferred_element_type=F32,
                         precision=lax.Precision.HIGHEST) + b_ref[...]


def _ada(c, w, b):
    R, D = c.shape
    N = w.shape[1]
    tn = 1024
    return pl.pallas_call(
        _ada_kernel,
        out_shape=jax.ShapeDtypeStruct((R, N), F32),
        grid=(N // tn,),
        in_specs=[pl.BlockSpec((R, D), lambda j: (0, 0)),
                  pl.BlockSpec((D, tn), lambda j: (0, j)),
                  pl.BlockSpec((1, tn), lambda j: (0, j))],
        out_specs=pl.BlockSpec((R, tn), lambda j: (0, j)),
        compiler_params=_cparams(("arbitrary",)),
        name="ada_mod",
    )(c, w, b)


def _rms(x):
    return x * lax.rsqrt(jnp.mean(x * x, axis=-1, keepdims=True) + EPS)


def _proj_kernel(x_ref, mod_ref, g_ref, w_ref, posk_ref, grp_ref,
                 o_ref, q_ref, ka_ref, v_ref, nrm_ref, *, qscale):
    x = x_ref[...]
    mod = mod_ref[0]
    h = _rms(x) * g_ref[...] * (1.0 + mod[1:2]) + mod[0:1]
    acc = jnp.dot(h.astype(BF16), w_ref[...], preferred_element_type=F32)
    qw = ATT_WIDTH
    qb = (acc[:, :qw] * qscale).astype(BF16)
    kb = acc[:, qw:2 * qw].astype(BF16)
    vb = acc[:, 2 * qw:3 * qw].astype(BF16)
    o_ref[...] = acc[:, 3 * qw:].astype(BF16)
    for hh in range(ATT_HEADS):
        cols = slice(hh * HEAD_W, (hh + 1) * HEAD_W)
        q_ref[hh] = qb[:, cols]
        v_ref[hh] = vb[:, cols]
        ka_ref[hh, :, :HEAD_W] = kb[:, cols]
        ka_ref[hh, :, HEAD_W:] = posk_ref[:, cols]
    for idx, t in enumerate((qb, kb)):
        tf = t.astype(F32)
        n2 = jnp.dot((tf * tf).astype(BF16), grp_ref[...], preferred_element_type=F32)
        nrm_ref[0, idx] = jnp.broadcast_to(jnp.max(n2, axis=0, keepdims=True), (8, HEAD_W))


def _proj(xt, mod, g, w, posk, S, tm):
    T, D = xt.shape
    N = w.shape[1]
    qscale = ATT_QK_DIM ** -0.5 * LOG2E
    nblk = T // tm
    spb = S // tm
    grp = (jnp.arange(ATT_WIDTH)[:, None] // ATT_QK_DIM == jnp.arange(HEAD_W)[None, :]).astype(BF16)
    return pl.pallas_call(
        functools.partial(_proj_kernel, qscale=qscale),
        out_shape=(jax.ShapeDtypeStruct((T, N - 3 * ATT_WIDTH), BF16),
                   jax.ShapeDtypeStruct((ATT_HEADS, T, HEAD_W), BF16),
                   jax.ShapeDtypeStruct((ATT_HEADS, T, 2 * HEAD_W), BF16),
                   jax.ShapeDtypeStruct((ATT_HEADS, T, HEAD_W), BF16),
                   jax.ShapeDtypeStruct((nblk, 2, 8, HEAD_W), F32)),
        grid=(nblk,),
        in_specs=[pl.BlockSpec((tm, D), lambda i: (i, 0)),
                  pl.BlockSpec((1, 6, D), lambda i: ((i * tm) // S, 0, 0)),
                  pl.BlockSpec((1, D), lambda i: (0, 0)),
                  pl.BlockSpec((D, N), lambda i: (0, 0)),
                  pl.BlockSpec((tm, ATT_WIDTH), lambda i: (i % spb, 0)),
                  pl.BlockSpec((ATT_WIDTH, HEAD_W), lambda i: (0, 0))],
        out_specs=(pl.BlockSpec((tm, N - 3 * ATT_WIDTH), lambda i: (i, 0)),
                   pl.BlockSpec((ATT_HEADS, tm, HEAD_W), lambda i: (0, i, 0)),
                   pl.BlockSpec((ATT_HEADS, tm, 2 * HEAD_W), lambda i: (0, i, 0)),
                   pl.BlockSpec((ATT_HEADS, tm, HEAD_W), lambda i: (0, i, 0)),
                   pl.BlockSpec((1, 2, 8, HEAD_W), lambda i: (i, 0, 0, 0))),
        compiler_params=_cparams(("parallel",)),
        name="norm1_proj",
    )(xt, mod, g, w, posk, grp)


def _split3(x):
    hi = x.astype(BF16)
    r = x - hi.astype(F32)
    mid = r.astype(BF16)
    lo = (r - mid.astype(F32)).astype(BF16)
    return hi, mid, lo


def _alibi_tables(slopes2, S):
    H = slopes2.shape[0]
    a = slopes2[:, None] * jnp.arange(S, dtype=F32)[None, :]
    ah, am, al = (t.astype(F32)[..., None] for t in _split3(a))
    lane = jnp.arange(HEAD_W)[None, None, :]
    terms = jnp.where(lane % 3 == 0, ah, jnp.where(lane % 3 == 1, am, al))
    posk = jnp.where(lane < 6, 1.0, jnp.where(lane < 9, terms, 0.0)).astype(BF16)
    posq = jnp.where(lane < 3, 0.0, jnp.where(lane < 6, -terms, jnp.where(lane < 9, 1.0, 0.0))).astype(BF16)
    return posq, posk.transpose(1, 0, 2).reshape(S, H * HEAD_W)


def _attn_kernel(slope_ref, lam_ref, q_ref, k_ref, v_ref, g_ref, o_ref,
                 qm_ref, m_ref, l_ref, acc_ref, *, tq, tk, nk):
    h = pl.program_id(1)
    qi = pl.program_id(2)
    ki = pl.program_id(3)

    @pl.when(ki == 0)
    def _init():
        q = q_ref[...]
        lane = lax.broadcasted_iota(I32, q.shape, 1)
        zero = jnp.zeros_like(q)
        qm_ref[0] = jnp.where(lane < ATT_QK_DIM, q, zero)
        qm_ref[1] = jnp.where(lane >= ATT_QK_DIM, q, zero)
        m_ref[...] = jnp.full(m_ref.shape, NEG_INF, F32)
        l_ref[...] = jnp.zeros(l_ref.shape, F32)
        acc_ref[...] = jnp.zeros(acc_ref.shape, F32)

    slope = slope_ref[h]
    k = k_ref[...]
    v = v_ref[...]
    row = lax.broadcasted_iota(I32, (tk, tq), 0)
    col = lax.broadcasted_iota(I32, (tk, tq), 1)
    dist = (col - row + (qi * tq - ki * tk)).astype(F32)
    bias = jnp.abs(dist) * (-slope)
    for m in range(2):
        s = lax.dot_general(k, qm_ref[m], (((1,), (1,)), ((), ())),
                            preferred_element_type=F32) + bias
        m_old = m_ref[m]
        m_new = jnp.maximum(m_old, jnp.max(s, axis=0, keepdims=True))
        alpha = jnp.exp2(m_old - m_new)
        p = jnp.exp2(s - m_new)
        l_ref[m] = alpha * l_ref[m] + jnp.sum(p, axis=0, keepdims=True)
        pv = lax.dot_general(v, p.astype(BF16), (((0,), (0,)), ((), ())),
                             preferred_element_type=F32)
        acc_ref[m] = alpha * acc_ref[m] + pv
        m_ref[m] = m_new

    @pl.when(ki == nk - 1)
    def _fin():
        lp = lam_ref[...]
        lam = (jnp.exp(jnp.sum(lp[0:1] * lp[1:2], axis=-1, keepdims=True))
               - jnp.exp(jnp.sum(lp[2:3] * lp[3:4], axis=-1, keepdims=True))
               + LAM_INIT)
        o = acc_ref[0] / l_ref[0] - lam * (acc_ref[1] / l_ref[1])
        ms = jnp.mean(o * o, axis=0, keepdims=True)
        y = o * lax.rsqrt(ms + EPS) * g_ref[...] * (1.0 - LAM_INIT)
        o_ref[...] = y.T.astype(o_ref.dtype)


def _attention(qh, kaug, vh, slopes, lam_p, subln_g, B, S, tq, tk):
    T = B * S
    nq, nk = S // tq, S // tk
    grid_spec = pltpu.PrefetchScalarGridSpec(
        num_scalar_prefetch=1,
        grid=(B, ATT_HEADS, nq, nk),
        in_specs=[
            pl.BlockSpec((4, ATT_QK_DIM), lambda b, h, qi, ki, s: (0, 0)),
            pl.BlockSpec((None, tq, HEAD_W), lambda b, h, qi, ki, s: (h, b * nq + qi, 0)),
            pl.BlockSpec((None, tk, HEAD_W), lambda b, h, qi, ki, s: (h, b * nk + ki, 0)),
            pl.BlockSpec((None, tk, HEAD_W), lambda b, h, qi, ki, s: (h, b * nk + ki, 0)),
            pl.BlockSpec((HEAD_W, 1), lambda b, h, qi, ki, s: (0, 0)),
        ],
        out_specs=pl.BlockSpec((tq, HEAD_W), lambda b, h, qi, ki, s: (b * nq + qi, h)),
        scratch_shapes=[pltpu.VMEM((2, tq, HEAD_W), BF16),
                        pltpu.VMEM((2, 1, tq), F32),
                        pltpu.VMEM((2, 1, tq), F32),
                        pltpu.VMEM((2, HEAD_W, tq), F32)],
    )
    return pl.pallas_call(
        functools.partial(_attn_kernel, tq=tq, tk=tk, nk=nk),
        out_shape=jax.ShapeDtypeStruct((T, ATT_WIDTH), BF16),
        grid_spec=grid_spec,
        compiler_params=_cparams(("parallel", "parallel", "parallel", "arbitrary")),
        name="diff_attention",
    )(slopes, lam_p, qh, kaug, vh, subln_g)


def _attn_finish(lam_ref, g_ref, o_ref, l_ref, acc_ref):
    lp = lam_ref[...]
    lam = (jnp.exp(jnp.sum(lp[0:1] * lp[1:2], axis=-1, keepdims=True))
           - jnp.exp(jnp.sum(lp[2:3] * lp[3:4], axis=-1, keepdims=True))
           + LAM_INIT)
    o = acc_ref[0] / l_ref[0] - lam * (acc_ref[1] / l_ref[1])
    ms = jnp.mean(o * o, axis=0, keepdims=True)
    y = o * lax.rsqrt(ms + EPS) * g_ref[...] * (1.0 - LAM_INIT)
    o_ref[...] = y.T.astype(o_ref.dtype)


def _attn_ref_kernel(sc_ref, lam_ref, q_ref, ka_ref, v_ref, g_ref, posq_ref, o_ref,
                     qa_ref, l_ref, acc_ref, *, tq, tk, nk):
    h = pl.program_id(1)
    qi = pl.program_id(2)
    ki = pl.program_id(3)

    @pl.when(ki == 0)
    def _init():
        q = q_ref[...]
        lane = lax.broadcasted_iota(I32, q.shape, 1)
        zero = jnp.zeros_like(q)
        kmax = sc_ref[ATT_HEADS]
        pa = posq_ref[...]
        for m in range(2):
            sel = (lane < ATT_QK_DIM) if m == 0 else (lane >= ATT_QK_DIM)
            qm = jnp.where(sel, q, zero)
            qf = qm.astype(F32)
            ub = jnp.sqrt(jnp.sum(qf * qf, axis=1, keepdims=True)) * kmax
            uh, um, ul = (t.astype(F32) for t in _split3(jnp.broadcast_to(ub, q.shape)))
            ubp = jnp.where(lane == 0, -uh, jnp.where(lane == 1, -um, jnp.where(lane == 2, -ul, 0.0)))
            ubp = ubp.astype(BF16)
            for var, pp in enumerate((pa + ubp, ubp - pa, ubp)):
                qa_ref[2 * var + m] = jnp.concatenate([qm, pp], axis=1)
        l_ref[...] = jnp.zeros(l_ref.shape, F32)
        acc_ref[...] = jnp.zeros(acc_ref.shape, F32)

    rel = qi * tq - ki * tk
    near = jnp.logical_and(rel > -tq, rel < tk)

    def body(with_bias):
        ka = ka_ref[...]
        if with_bias:
            row = lax.broadcasted_iota(I32, (tk, tq), 0)
            col = lax.broadcasted_iota(I32, (tk, tq), 1)
            bias = jnp.abs((col - row + rel).astype(F32)) * sc_ref[h]
            var = 2
        else:
            var = jnp.where(rel > 0, 0, 1)
        ss = [lax.dot_general(ka, qa_ref[2 * var + m], (((1,), (1,)), ((), ())),
                              preferred_element_type=F32) for m in range(2)]
        v = v_ref[...]
        for m in range(2):
            p = jnp.exp2(ss[m] + bias if with_bias else ss[m])
            l_ref[m] += jnp.sum(p, axis=0, keepdims=True)
            acc_ref[m] += lax.dot_general(v, p.astype(BF16), (((0,), (0,)), ((), ())),
                                          preferred_element_type=F32)

    @pl.when(near)
    def _():
        body(True)

    @pl.when(jnp.logical_not(near))
    def _():
        body(False)

    @pl.when(ki == nk - 1)
    def _fin():
        _attn_finish(lam_ref, g_ref, o_ref, l_ref, acc_ref)


def _attention_ref(qh, kaug, vh, scal, lam_p, subln_g, posq, B, S, tq, tk):
    T = B * S
    nq, nk = S // tq, S // tk
    grid_spec = pltpu.PrefetchScalarGridSpec(
        num_scalar_prefetch=1,
        grid=(B, ATT_HEADS, nq, nk),
        in_specs=[
            pl.BlockSpec((4, ATT_QK_DIM), lambda b, h, qi, ki, s: (0, 0)),
            pl.BlockSpec((None, tq, HEAD_W), lambda b, h, qi, ki, s: (h, b * nq + qi, 0)),
            pl.BlockSpec((None, tk, 2 * HEAD_W), lambda b, h, qi, ki, s: (h, b * nk + ki, 0)),
            pl.BlockSpec((None, tk, HEAD_W), lambda b, h, qi, ki, s: (h, b * nk + ki, 0)),
            pl.BlockSpec((HEAD_W, 1), lambda b, h, qi, ki, s: (0, 0)),
            pl.BlockSpec((None, tq, HEAD_W), lambda b, h, qi, ki, s: (h, qi, 0)),
        ],
        out_specs=pl.BlockSpec((tq, HEAD_W), lambda b, h, qi, ki, s: (b * nq + qi, h)),
        scratch_shapes=[pltpu.VMEM((6, tq, 2 * HEAD_W), BF16),
                        pltpu.VMEM((2, 1, tq), F32),
                        pltpu.VMEM((2, HEAD_W, tq), F32)],
    )
    return pl.pallas_call(
        functools.partial(_attn_ref_kernel, tq=tq, tk=tk, nk=nk),
        out_shape=jax.ShapeDtypeStruct((T, ATT_WIDTH), BF16),
        grid_spec=grid_spec,
        compiler_params=_cparams(("parallel", "parallel", "parallel", "arbitrary")),
        name="diff_attention_ref",
    )(scal, lam_p, qh, kaug, vh, subln_g, posq)


def _hg_chunk(q, kk, v, g, st, reverse):
    C = q.shape[0]
    nsub = C // HG_SUB
    row = lax.broadcasted_iota(I32, (C, HEAD_W), 0)
    b = _hg_cumsum(g, reverse)
    b_end = b[C - 1:C] if not reverse else b[0:1]

    qd = (q * jnp.exp(b)).astype(BF16)
    o_inter = lax.dot_general(qd, st.astype(BF16), (((1,), (1,)), ((), ())),
                              preferred_element_type=F32)

    ones = jnp.ones((HEAD_W, HEAD_W), BF16)
    srow = lax.broadcasted_iota(I32, (HG_SUB, HEAD_W), 0)
    vb = v.astype(BF16)
    outs = []
    for blk in range(nsub):
        lo, hi = blk * HG_SUB, (blk + 1) * HG_SUB
        qI, kI, vI, bI = q[lo:hi], kk[lo:hi], v[lo:hi], b[lo:hi]
        pieces = []
        for t in range(HG_SUB):
            dl = bI[t:t + 1] - bI
            keep = (srow <= t) if not reverse else (srow >= t)
            e = jnp.exp(jnp.where(keep, dl, NEG_INF))
            pieces.append(qI[t:t + 1] * kI * e)
        wst = jnp.concatenate(pieces, axis=0).astype(BF16)
        rsum = jnp.dot(wst, ones, preferred_element_type=F32)
        o_blk = jnp.sum(rsum.reshape(HG_SUB, HG_SUB, HEAD_W) * vI[None], axis=1)
        has_off = (blk > 0) if not reverse else (blk < nsub - 1)
        if has_off:
            if not reverse:
                r = b[lo - 1:lo]
                kmask = row < lo
            else:
                r = b[hi:hi + 1]
                kmask = row >= hi
            qs = (qI * jnp.exp(bI - r)).astype(BF16)
            ks = (kk * jnp.exp(jnp.where(kmask, r - b, NEG_INF))).astype(BF16)
            a = lax.dot_general(qs, ks, (((1,), (1,)), ((), ())),
                                preferred_element_type=F32)
            o_blk = o_blk + jnp.dot(a.astype(BF16), vb, preferred_element_type=F32)
        outs.append(o_blk)
    o = o_inter + jnp.concatenate(outs, axis=0)

    kd = (kk * jnp.exp(b_end - b)).astype(BF16)
    upd = lax.dot_general(vb, kd, (((0,), (0,)), ((), ())),
                          preferred_element_type=F32)
    st_new = st * jnp.exp(b_end) + upd
    return o, st_new


def _hg_cumsum(g, reverse):
    C = g.shape[0]
    row = lax.broadcasted_iota(I32, g.shape, 0)
    b = g
    d = 1
    while d < C:
        if not reverse:
            b = b + jnp.where(row >= d, pltpu.roll(b, d, axis=0), 0.0)
        else:
            b = b + jnp.where(row < C - d, pltpu.roll(b, C - d, axis=0), 0.0)
        d *= 2
    return b


def _hg_chunk_mild(q, kk, v, g, st, reverse):
    C = q.shape[0]
    nsub = C // HG_SUB
    row = lax.broadcasted_iota(I32, (C, HEAD_W), 0)
    b = _hg_cumsum(g, reverse)
    b_end = b[C - 1:C] if not reverse else b[0:1]
    qd = (q * jnp.exp(b)).astype(BF16)
    o_inter = lax.dot_general(qd, st.astype(BF16), (((1,), (1,)), ((), ())),
                              preferred_element_type=F32)
    vb = v.astype(BF16)
    qrow = lax.broadcasted_iota(I32, (HG_SUB, C), 0)
    kcol = lax.broadcasted_iota(I32, (HG_SUB, C), 1)
    zero_ref = jnp.zeros((1, HEAD_W), F32)
    a_rows = []
    for blk in range(nsub):
        lo, hi = blk * HG_SUB, (blk + 1) * HG_SUB
        if not reverse:
            r = b[lo - 1:lo] if blk > 0 else zero_ref
            kmask = row < hi
            amask = kcol <= qrow + lo
        else:
            r = b[hi:hi + 1] if blk < nsub - 1 else zero_ref
            kmask = row >= lo
            amask = kcol >= qrow + lo
        qs = (q[lo:hi] * jnp.exp(b[lo:hi] - r)).astype(BF16)
        ks = (kk * jnp.exp(jnp.where(kmask, r - b, NEG_INF))).astype(BF16)
        a = lax.dot_general(qs, ks, (((1,), (1,)), ((), ())), preferred_element_type=F32)
        a_rows.append(jnp.where(amask, a, 0.0))
    a_full = jnp.concatenate(a_rows, axis=0).astype(BF16)
    o = o_inter + jnp.dot(a_full, vb, preferred_element_type=F32)
    kd = (kk * jnp.exp(b_end - b)).astype(BF16)
    upd = lax.dot_general(vb, kd, (((0,), (0,)), ((), ())), preferred_element_type=F32)
    return o, st * jnp.exp(b_end) + upd


def _hg_kernel(qf_ref, ff_ref, if_ref, qb_ref, fb_ref, ib_ref, lb_ref, thr_ref,
               of_ref, ob_ref, sf_ref, sb_ref, *, nchunk):
    j = pl.program_id(2)

    @pl.when(j == 0)
    def _init():
        sf_ref[...] = jnp.zeros(sf_ref.shape, F32)
        sb_ref[...] = jnp.zeros(sb_ref.shape, F32)

    lb = lb_ref[0]
    C = HG_CHUNK

    def prep(q_ref, f_ref, i_ref, lbd, rows):
        q = _silu(q_ref[rows, :].astype(F32))
        f = lbd + (1.0 - lbd) * jax.nn.sigmoid(f_ref[rows, :].astype(F32))
        return q, 1.0 - f, i_ref[rows, :].astype(F32), jnp.log(f)

    def step(chunk_fn, c, carry):
        rows_f = pl.ds(pl.multiple_of(c * C, C), C)
        rows_b = pl.ds(pl.multiple_of((nchunk - 1 - c) * C, C), C)
        o, st = chunk_fn(*prep(qf_ref, ff_ref, if_ref, lb[0:1], rows_f), sf_ref[...], reverse=False)
        of_ref[rows_f, :] = o
        sf_ref[...] = st
        o, st = chunk_fn(*prep(qb_ref, fb_ref, ib_ref, lb[1:2], rows_b), sb_ref[...], reverse=True)
        ob_ref[rows_b, :] = o
        sb_ref[...] = st
        return carry

    thr = thr_ref[0]
    above = jnp.minimum(
        jnp.min(jnp.where(ff_ref[...].astype(F32) >= thr[0:1], 1.0, 0.0)),
        jnp.min(jnp.where(fb_ref[...].astype(F32) >= thr[1:2], 1.0, 0.0)))
    mild = above > 0.5

    @pl.when(mild)
    def _():
        lax.fori_loop(0, nchunk, functools.partial(step, _hg_chunk_mild), 0)

    @pl.when(jnp.logical_not(mild))
    def _():
        lax.fori_loop(0, nchunk, functools.partial(step, _hg_chunk), 0)


def _hgrn2(proj, lbs, B, S):
    T = B * S
    nchunk = min(HG_STEP_CHUNKS, S // HG_CHUNK)
    C = HG_CHUNK * nchunk
    n = S // C
    nh = HG_HEADS
    qc, ffc, fbc, ic = 0, nh, 2 * nh, 3 * nh

    def fw(col):
        return pl.BlockSpec((C, HEAD_W), lambda b, h, j: (b * n + j, col + h))

    def bw(col):
        return pl.BlockSpec((C, HEAD_W), lambda b, h, j: (b * n + n - 1 - j, col + h))

    out_f = pl.BlockSpec((C, HEAD_W), lambda b, h, j: (b * n + j, h))
    out_b = pl.BlockSpec((C, HEAD_W), lambda b, h, j: (b * n + n - 1 - j, h))
    need = (math.exp(-HG_MILD_LOG_GATE) - lbs) / (1.0 - lbs)
    thr = jnp.where(need > 0.0, jnp.log(jnp.maximum(need, 1e-30) / (1.0 - need)), NEG_INF).astype(F32)
    per_head = pl.BlockSpec((1, 2, HEAD_W), lambda b, h, j: (h, 0, 0))
    return pl.pallas_call(
        functools.partial(_hg_kernel, nchunk=nchunk),
        out_shape=(jax.ShapeDtypeStruct((T, HG_WIDTH), F32),
                   jax.ShapeDtypeStruct((T, HG_WIDTH), F32)),
        grid=(B, nh, n),
        in_specs=[fw(qc), fw(ffc), fw(ic), bw(qc), bw(fbc), bw(ic), per_head, per_head],
        out_specs=(out_f, out_b),
        scratch_shapes=[pltpu.VMEM((HEAD_W, HEAD_W), F32),
                        pltpu.VMEM((HEAD_W, HEAD_W), F32)],
        compiler_params=_cparams(("parallel", "parallel", "arbitrary")),
        name="hgrn2_scan",
    )(proj, proj, proj, proj, proj, proj, lbs, thr)


def _mixout_kernel(x_ref, oa_ref, of_ref, ob_ref, gate_ref, mod_ref, hgg_ref, wo_ref,
                   n2_ref, wr_ref, x1_ref, h2_ref, lg_ref):
    mod = mod_ref[0]
    tm = x_ref.shape[0]
    sub = min(tm, MIXOUT_SUB_ROWS)
    for r0 in range(0, tm, sub):
        rows = slice(r0, r0 + sub)
        o = of_ref[rows, :] + ob_ref[rows, :]
        gate = _silu(gate_ref[rows, :].astype(F32))
        hg = jnp.concatenate(
            [_rms(o[:, h * HEAD_W:(h + 1) * HEAD_W]) * hgg_ref[...] for h in range(HG_HEADS)],
            axis=-1) * gate
        mix = (jnp.dot(oa_ref[rows, :], wo_ref[:ATT_WIDTH, :], preferred_element_type=F32)
               + jnp.dot(hg.astype(BF16), wo_ref[ATT_WIDTH:, :], preferred_element_type=F32))
        x1 = x_ref[rows, :] + mod[2:3] * mix
        x1_ref[rows, :] = x1
        h2 = _rms(x1) * n2_ref[...] * (1.0 + mod[4:5]) + mod[3:4]
        h2_ref[rows, :] = _pack_rows(h2)
        h_hi = h2.astype(BF16)
        h_mid = (h2 - h_hi.astype(F32)).astype(BF16)
        two = jnp.dot(h_hi, wr_ref[...], preferred_element_type=F32)
        logits = (two[:, :N_EXPERTS] + two[:, N_EXPERTS:]
                  + jnp.dot(h_mid, wr_ref[:, :N_EXPERTS], preferred_element_type=F32))
        lg_ref[:, rows] = logits.T


def _mixout(xt, o_att, o_fw, o_bw, proj, mod, hg_g, w_out, n2g, w_router, S, tm):
    T, D = xt.shape
    gate_col = proj.shape[1] // HG_WIDTH - 1
    row = lambda i: (i, 0)
    const = lambda i: (0, 0)
    return pl.pallas_call(
        _mixout_kernel,
        out_shape=(jax.ShapeDtypeStruct((T, D), F32),
                   jax.ShapeDtypeStruct((T, D // 2), I32),
                   jax.ShapeDtypeStruct((N_EXPERTS, T), F32)),
        grid=(T // tm,),
        in_specs=[pl.BlockSpec((tm, D), row),
                  pl.BlockSpec((tm, ATT_WIDTH), row),
                  pl.BlockSpec((tm, HG_WIDTH), row),
                  pl.BlockSpec((tm, HG_WIDTH), row),
                  pl.BlockSpec((tm, HG_WIDTH), lambda i: (i, gate_col)),
                  pl.BlockSpec((1, 6, D), lambda i: ((i * tm) // S, 0, 0)),
                  pl.BlockSpec((1, HEAD_W), const),
                  pl.BlockSpec(w_out.shape, const),
                  pl.BlockSpec((1, D), const),
                  pl.BlockSpec(w_router.shape, const)],
        out_specs=(pl.BlockSpec((tm, D), row),
                   pl.BlockSpec((tm, D // 2), row),
                   pl.BlockSpec((N_EXPERTS, tm), lambda i: (0, i))),
        compiler_params=_cparams(("parallel",)),
        name="mixout_norm2_router",
    )(xt, o_att, o_fw, o_bw, proj, mod, hg_g, w_out, n2g, w_router)


def _first_argmax(x, iota, size):
    mx = jnp.max(x, axis=0, keepdims=True)
    idx = jnp.min(jnp.where(x == mx, iota, size), axis=0, keepdims=True)
    return mx, idx


def _route_kernel(lg_ref, br_ref, tri_ref, e_ref, w_ref, r_ref, cnt_ref, carry_ref):
    i = pl.program_id(0)

    @pl.when(i == 0)
    def _init():
        carry_ref[...] = jnp.zeros(carry_ref.shape, F32)

    scores = jax.nn.sigmoid(lg_ref[...])
    biased = scores + br_ref[...]
    tm = scores.shape[1]
    giota = lax.broadcasted_iota(I32, (GROUP_SIZE, tm), 0)
    gs = []
    for g in range(N_GROUPS):
        blk = biased[g * GROUP_SIZE:(g + 1) * GROUP_SIZE]
        m1, i1 = _first_argmax(blk, giota, GROUP_SIZE)
        m2 = jnp.max(jnp.where(giota == i1, NEG_INF, blk), axis=0, keepdims=True)
        gs.append(m1 + m2)
    gsc = jnp.concatenate(gs, axis=0)
    gi = lax.broadcasted_iota(I32, (N_GROUPS, tm), 0)
    gsel = jnp.zeros((N_GROUPS, tm), jnp.bool_)
    for _ in range(TOPK_GROUPS):
        _, idx = _first_argmax(gsc, gi, N_GROUPS)
        hit = gi == idx
        gsel = jnp.logical_or(gsel, hit)
        gsc = jnp.where(hit, NEG_INF, gsc)
    masked = jnp.concatenate(
        [jnp.where(gsel[g:g + 1], biased[g * GROUP_SIZE:(g + 1) * GROUP_SIZE], NEG_INF)
         for g in range(N_GROUPS)], axis=0)
    ei = lax.broadcasted_iota(I32, (N_EXPERTS, tm), 0)
    eidx, wts = [], []
    onehot = jnp.zeros((N_EXPERTS, tm), F32)
    for _ in range(TOP_K):
        _, idx = _first_argmax(masked, ei, N_EXPERTS)
        hit = ei == idx
        eidx.append(idx)
        wts.append(jnp.sum(jnp.where(hit, scores, 0.0), axis=0, keepdims=True))
        onehot = jnp.where(hit, 1.0, onehot)
        masked = jnp.where(hit, NEG_INF, masked)
    w = jnp.concatenate(wts, axis=0)
    w = w / (jnp.sum(w, axis=0, keepdims=True) + 1e-20) * ROUTE_SCALE
    e_ref[...] = jnp.concatenate(eidx, axis=0)
    w_ref[...] = w
    before = jnp.dot(onehot.astype(BF16), tri_ref[...], preferred_element_type=F32)
    before = before + carry_ref[...]
    ranks = [jnp.sum(jnp.where(ei == idx, before, 0.0), axis=0, keepdims=True) for idx in eidx]
    r_ref[...] = jnp.concatenate(ranks, axis=0).astype(I32)
    carry = carry_ref[...] + jnp.sum(onehot, axis=1, keepdims=True)
    carry_ref[...] = carry
    cnt_ref[...] = carry.astype(I32)


def _route(logits_t, b_router, tm):
    E, T = logits_t.shape
    tri = (jnp.arange(tm)[:, None] < jnp.arange(tm)[None, :]).astype(BF16)
    tok = lambda i: (0, i)
    const = lambda i: (0, 0)
    return pl.pallas_call(
        _route_kernel,
        out_shape=(jax.ShapeDtypeStruct((TOP_K, T), I32),
                   jax.ShapeDtypeStruct((TOP_K, T), F32),
                   jax.ShapeDtypeStruct((TOP_K, T), I32),
                   jax.ShapeDtypeStruct((E, 1), I32)),
        grid=(T // tm,),
        in_specs=[pl.BlockSpec((E, tm), tok),
                  pl.BlockSpec((E, 1), const),
                  pl.BlockSpec((tm, tm), const)],
        out_specs=(pl.BlockSpec((TOP_K, tm), tok),
                   pl.BlockSpec((TOP_K, tm), tok),
                   pl.BlockSpec((TOP_K, tm), tok),
                   pl.BlockSpec((E, 1), const)),
        scratch_shapes=[pltpu.VMEM((E, 1), F32)],
        compiler_params=_cparams(("arbitrary",)),
        name="route_topk",
    )(logits_t, b_router, tri)


def _pos_kernel(e_ref, r_ref, ps_ref, o_ref):
    e = e_ref[...]
    tm = e.shape[1]
    ei = lax.broadcasted_iota(I32, (N_EXPERTS, tm), 0)
    ps = ps_ref[...]
    rows = [jnp.sum(jnp.where(ei == e[k:k + 1], ps, 0), axis=0, keepdims=True)
            for k in range(TOP_K)]
    o_ref[...] = jnp.concatenate(rows, axis=0) + r_ref[...]


def _positions(eidx, rank, pstart, tm):
    K, T = eidx.shape
    tok = lambda i: (0, i)
    return pl.pallas_call(
        _pos_kernel,
        out_shape=jax.ShapeDtypeStruct((K, T), I32),
        grid=(T // tm,),
        in_specs=[pl.BlockSpec((K, tm), tok), pl.BlockSpec((K, tm), tok),
                  pl.BlockSpec((N_EXPERTS, 1), lambda i: (0, 0))],
        out_specs=pl.BlockSpec((K, tm), tok),
        compiler_params=_cparams(("parallel",)),
        name="dispatch_positions",
    )(eidx, rank, pstart)


def _sc_mesh():
    return plsc.VectorSubcoreMesh(core_axis_name="c", subcore_axis_name="s")


def _worker_id():
    return lax.axis_index("s") * SC_CORES + lax.axis_index("c")


def _window_positions(pos):
    K, T = pos.shape
    nwin = T // (SC_WORKERS * SC_WINDOW)
    assert nwin * SC_WORKERS * SC_WINDOW == T, T
    return pos.reshape(K, SC_WORKERS, nwin, SC_WINDOW).transpose(1, 2, 0, 3)


def _sc_dispatch(h, pos, P):
    T, D = h.shape
    pos4 = _window_positions(pos)
    NW, nwin, K, W = pos4.shape

    @functools.partial(
        pl.kernel, mesh=_sc_mesh(),
        out_type=jax.ShapeDtypeStruct((P, D), h.dtype),
        scratch_types=[pltpu.VMEM((K, W), I32), pltpu.VMEM((W, D), h.dtype),
                       pltpu.SemaphoreType.DMA],
        name="sc_dispatch")
    def k(h_hbm, pos_hbm, xs_hbm, idx_v, rows_v, sem):
        wid = _worker_id()

        @pl.loop(0, nwin)
        def _(j):
            base = (wid * nwin + j) * W
            pltpu.sync_copy(pos_hbm.at[wid, j], idx_v)
            pltpu.sync_copy(h_hbm.at[pl.ds(base, W)], rows_v)
            copies = [pltpu.async_copy(rows_v, xs_hbm.at[idx_v.at[kk]], sem)
                      for kk in range(K)]
            for c in copies:
                c.wait()

    return k(h, pos4)


def _sc_gather(ys, pos):
    P, D = ys.shape
    T = pos.shape[1]
    pos4 = _window_positions(pos)
    NW, nwin, K, W = pos4.shape

    @functools.partial(
        pl.kernel, mesh=_sc_mesh(),
        out_type=jax.ShapeDtypeStruct((K, T, D), ys.dtype),
        scratch_types=[pltpu.VMEM((K, W), I32), pltpu.VMEM((W, D), ys.dtype),
                       pltpu.SemaphoreType.DMA],
        name="sc_gather")
    def k(ys_hbm, pos_hbm, yg_hbm, idx_v, rows_v, sem):
        wid = _worker_id()

        @pl.loop(0, nwin)
        def _(j):
            base = (wid * nwin + j) * W
            pltpu.sync_copy(pos_hbm.at[wid, j], idx_v)
            for kk in range(K):
                pltpu.async_copy(ys_hbm.at[idx_v.at[kk]], rows_v, sem).wait()
                pltpu.sync_copy(rows_v, yg_hbm.at[kk, pl.ds(base, W)])

    return k(ys, pos4)


def _expert_kernel(be_ref, nv_ref, x_ref, wg_ref, wu_ref, wd_ref, o_ref):
    i = pl.program_id(0)

    @pl.when(i < nv_ref[0])
    def _():
        lo, hi = _unpack_rows(x_ref[...])
        x = jnp.concatenate([lo, hi], axis=1).astype(BF16)
        a = (_silu(jnp.dot(x, wg_ref[0].astype(BF16), preferred_element_type=F32))
             * jnp.dot(x, wu_ref[0].astype(BF16), preferred_element_type=F32))
        o_ref[...] = _pack_rows(jnp.dot(a.astype(BF16), wd_ref[0].astype(BF16),
                                        preferred_element_type=F32))


def _expert_block(T):
    return 2 * EXPERT_BLOCK if T * TOP_K >= 4 * EXPERT_BLOCK * N_EXPERTS else EXPERT_BLOCK


def _experts(xs, blk_e, nvalid, wg, wu, wd, bm):
    P, Dh = xs.shape
    D = 2 * Dh
    nb = P // bm
    rowmap = lambda i, be, nv: (jnp.minimum(i, nv[0] - 1), 0)
    wmap = lambda i, be, nv: (be[i], 0, 0)
    grid_spec = pltpu.PrefetchScalarGridSpec(
        num_scalar_prefetch=2,
        grid=(nb,),
        in_specs=[pl.BlockSpec((bm, Dh), rowmap),
                  pl.BlockSpec((1, D, EXPERT_DIM), wmap),
                  pl.BlockSpec((1, D, EXPERT_DIM), wmap),
                  pl.BlockSpec((1, EXPERT_DIM, D), wmap)],
        out_specs=pl.BlockSpec((bm, Dh), rowmap),
    )
    return pl.pallas_call(
        _expert_kernel,
        out_shape=jax.ShapeDtypeStruct((P, Dh), I32),
        grid_spec=grid_spec,
        compiler_params=_cparams(("arbitrary",)),
        name="expert_ffn",
    )(blk_e, nvalid, xs, wg, wu, wd)


def _final_kernel(yg_ref, w_ref, h2_ref, x1_ref, mod_ref, modf_ref, nf_ref,
                  sg_ref, su_ref, sd_ref, o_ref):
    w2 = w_ref[...].astype(BF16)
    acc = None
    for k in range(TOP_K):
        term = pltpu.bitcast(yg_ref[k], BF16) * w2[:, k:k + 1]
        acc = term if acc is None else acc + term
    y = jnp.concatenate(_unpack_rows(pltpu.bitcast(acc, I32)), axis=1)
    hb = jnp.concatenate(_unpack_rows(h2_ref[...]), axis=1).astype(BF16)
    a = (_silu(jnp.dot(hb, sg_ref[...], preferred_element_type=F32))
         * jnp.dot(hb, su_ref[...], preferred_element_type=F32))
    shared = jnp.dot(a.astype(BF16), sd_ref[...], preferred_element_type=F32)
    mod = mod_ref[0]
    modf = modf_ref[0]
    x2 = x1_ref[...] + mod[5:6] * (y + shared)
    o_ref[...] = _rms(x2) * nf_ref[...] * (1.0 + modf[1:2]) + modf[0:1]


def _final(yg, wts_t, h2, x1, mod, modf, nfg, sg, su, sd, S, tm):
    T, D = x1.shape
    row = lambda i: (i, 0)
    const = lambda i: (0, 0)
    bat = lambda i: ((i * tm) // S, 0, 0)
    return pl.pallas_call(
        _final_kernel,
        out_shape=jax.ShapeDtypeStruct((T, D), F32),
        grid=(T // tm,),
        in_specs=[pl.BlockSpec((TOP_K, tm, D // 2), lambda i: (0, i, 0)),
                  pl.BlockSpec((2 * tm, TOP_K), row),
                  pl.BlockSpec((tm, D // 2), row),
                  pl.BlockSpec((tm, D), row),
                  pl.BlockSpec((1, 6, D), bat),
                  pl.BlockSpec((1, 2, D), bat),
                  pl.BlockSpec((1, D), const),
                  pl.BlockSpec(sg.shape, const),
                  pl.BlockSpec(su.shape, const),
                  pl.BlockSpec(sd.shape, const)],
        out_specs=pl.BlockSpec((tm, D), row),
        compiler_params=_cparams(("parallel",)),
        name="combine_shared_final",
    )(yg, wts_t, h2, x1, mod, modf, nfg, sg, su, sd)


def _tile(n, pref):
    t = min(n, pref)
    assert n % t == 0, (n, pref)
    return t


def _plan_dispatch(logits_t, p):
    T = logits_t.shape[1]
    tm_r = _tile(T, 512)
    eidx, wts, rank, counts = _route(logits_t, p["b_router"], tm_r)
    bm = _expert_block(T)
    counts = counts[:, 0]
    padded = (counts + bm - 1) // bm * bm
    pends = jnp.cumsum(padded)
    pstart = (pends - padded).astype(I32)
    nb = (T * TOP_K + N_EXPERTS * bm) // bm
    pos = _positions(eidx, rank, pstart[:, None], tm_r)
    blk_e = jnp.minimum(jnp.searchsorted(pends, jnp.arange(nb) * bm, side="right"),
                        N_EXPERTS - 1).astype(I32)
    nvalid = (pends[-1] // bm).astype(I32)
    blk_e = jnp.where(jnp.arange(nb) < nvalid, blk_e, blk_e[jnp.maximum(nvalid - 1, 0)])
    return jnp.repeat(wts.T, 2, axis=0), pos, blk_e, nvalid[None]


NORM_SLACK = 1.02
REF_GAP_LIMIT = 100.0


def _diff_attention(qh, kaug, vh, nrm, posq, p, B, S):
    qmax = jnp.sqrt(jnp.max(nrm[:, 0, 0, :2 * ATT_HEADS])) * NORM_SLACK
    kmax = jnp.sqrt(jnp.max(nrm[:, 1, 0, :2 * ATT_HEADS])) * NORM_SLACK
    in_range = 2.0 * qmax * kmax <= REF_GAP_LIMIT
    scal = jnp.concatenate([-p["slopes"], kmax[None]]).astype(F32)
    tq = _tile(S, 1024)
    tk = 2 * tq if S >= 16 * tq else tq

    def single_pass(_):
        return _attention_ref(qh, kaug, vh, scal, p["lam_p"], p["subln_g"], posq, B, S, tq, tk)

    def online(_):
        return _attention(qh, kaug, vh, p["slopes"], p["lam_p"], p["subln_g"], B, S,
                          _tile(S, 256), _tile(S, 512))

    return lax.cond(in_range, single_pass, online, None)


def _mixer_and_routing(x, mod, p):
    B, S, D = x.shape
    T = B * S
    xt = x.reshape(T, D)
    tm = _tile(S, 256)
    posq, posk = _alibi_tables(p["slopes"], S)
    proj, qh, kaug, vh, nrm = _proj(xt, mod, p["norm1_g"], p["w_in"], posk, S, tm)
    o_att = _diff_attention(qh, kaug, vh, nrm, posq, p, B, S)
    o_fw, o_bw = _hgrn2(proj, p["lbs"], B, S)
    x1, h2, logits_t = _mixout(xt, o_att, o_fw, o_bw, proj, mod, p["hg_norm_g"], p["w_out"],
                               p["norm2_g"], p["w_router"], S, _tile(S, 2 * MIXOUT_SUB_ROWS))
    wts, pos, blk_e, nvalid = _plan_dispatch(logits_t, p)
    return dict(x1=x1, h2=h2, wts=wts, pos=pos, blk_e=blk_e, nvalid=nvalid, mod=mod, shape=(B, S, D))


def _expert_rows(st, p, after=None):
    T = st["h2"].shape[0]
    bm = _expert_block(T)
    xs = _sc_dispatch(st["h2"], st["pos"], T * TOP_K + N_EXPERTS * bm)
    if after is not None:
        xs, _ = lax.optimization_barrier((xs, after))
    return _experts(xs, st["blk_e"], st["nvalid"], p["wg"], p["wu"], p["wd"], bm)


def _combine(st, ys, modf, p):
    B, S, D = st["shape"]
    yg = _sc_gather(ys, st["pos"])
    out = _final(yg, st["wts"], st["h2"], st["x1"], st["mod"], modf, p["normf_g"],
                 p["sg"], p["su"], p["sd"], S, _tile(S, 256))
    return out.reshape(B, S, D)


def _trunk(x, mod, modf, p):
    st = _mixer_and_routing(x, mod, p)
    return _combine(st, _expert_rows(st, p), modf, p)


def kernel(x_prompt, x_sample, c_prompt, c_sample, w_ada, b_ada, norm1_g, w_in, lam_q1, lam_k1, lam_q2, lam_k2, subln_g, hg_lb_logits, hg_norm_g, w_out, norm2_g, w_router, b_router, w_exp_gate, w_exp_up, w_exp_down, w_sh_gate, w_sh_up, w_sh_down, w_ada_f, b_ada_f, normf_g):
    D = x_prompt.shape[-1]
    Bp, Bs = c_prompt.shape[0], c_sample.shape[0]
    c_all = jnp.concatenate([c_prompt, c_sample], axis=0)
    R = -(-c_all.shape[0] // 8) * 8
    c_all = jnp.pad(c_all, ((0, R - c_all.shape[0]), (0, 0)))
    mod6 = _ada(c_all, w_ada[0], b_ada[0][None]).reshape(R, 6, D)
    mod2 = _ada(c_all, w_ada_f, b_ada_f[None]).reshape(R, 2, D)

    lbs = jax.nn.softmax(hg_lb_logits.astype(F32), axis=0)[0]
    lbs = lbs.reshape(2, HG_HEADS, HEAD_W).transpose(1, 0, 2)
    slopes = (jnp.exp2(-8.0 * (jnp.arange(ATT_HEADS, dtype=F32) + 1.0) / ATT_HEADS) * LOG2E)
    p = dict(
        norm1_g=norm1_g[0][None], w_in=w_in[0].astype(BF16),
        slopes=slopes.astype(F32),
        lam_p=jnp.stack([lam_q1[0], lam_k1[0], lam_q2[0], lam_k2[0]]).astype(F32),
        subln_g=subln_g[0][:, None], lbs=lbs, hg_norm_g=hg_norm_g[0][None],
        w_out=w_out[0].astype(BF16), norm2_g=norm2_g[0][None],
        w_router=jnp.concatenate(_split3(w_router[0].astype(F32))[:2], axis=1),
        b_router=b_router[0][:, None],
        wg=w_exp_gate[0], wu=w_exp_up[0], wd=w_exp_down[0],
        sg=w_sh_gate[0].astype(BF16), su=w_sh_up[0].astype(BF16), sd=w_sh_down[0].astype(BF16),
        normf_g=normf_g[None],
    )
    st_p = _mixer_and_routing(x_prompt, mod6[:Bp], p)
    x_sample, _ = lax.optimization_barrier((x_sample, st_p["pos"]))
    st_s = _mixer_and_routing(x_sample, mod6[Bp:Bp + Bs], p)
    ys_p = _expert_rows(st_p, p)
    ys_s = _expert_rows(st_s, p, after=ys_p)
    y_prompt = _combine(st_p, ys_p, mod2[:Bp], p)
    y_sample = _combine(st_s, ys_s, mod2[Bp:Bp + Bs], p)
    return (y_prompt, y_sample)
```

```python
import functools
import math

import jax
import jax.numpy as jnp
from jax import lax
from jax.experimental import pallas as pl
from jax.experimental.pallas import tpu as pltpu
from jax.experimental.pallas import tpu_sc as plsc

F32 = jnp.float32
BF16 = jnp.bfloat16
I32 = jnp.int32

EPS = 1e-6
LOG2E = 1.4426950408889634

ATT_HEADS = 4
ATT_QK_DIM = 64
HEAD_W = 128
ATT_WIDTH = ATT_HEADS * HEAD_W
HG_HEADS = 4
HG_WIDTH = HG_HEADS * HEAD_W
N_EXPERTS = 256
TOP_K = 8
N_GROUPS = 8
TOPK_GROUPS = 4
GROUP_SIZE = N_EXPERTS // N_GROUPS
EXPERT_DIM = 256
ROUTE_SCALE = 2.5
LAM_INIT = 0.8 - 0.6 * math.exp(-0.3 * 0)

HG_CHUNK = 128
HG_SUB = 16
HG_MILD_LOG_GATE = 3.75
HG_STEP_CHUNKS = 8
EXPERT_BLOCK = 512
MIXOUT_SUB_ROWS = 256
SC_CORES = 2
SC_SUBCORES = 16
SC_WORKERS = SC_CORES * SC_SUBCORES
SC_WINDOW = 32
VMEM_LIMIT = 56 * 1024 * 1024

NEG_INF = float("-inf")


def _cparams(sem):
    return pltpu.CompilerParams(dimension_semantics=sem, vmem_limit_bytes=VMEM_LIMIT)


def _silu(x):
    return x * jax.nn.sigmoid(x)


def _pack_rows(x):
    half = x.shape[1] // 2
    bits = lax.bitcast_convert_type(x.astype(BF16).astype(F32), jnp.uint32)
    return lax.bitcast_convert_type((bits[:, :half] >> 16) | bits[:, half:], I32)


def _unpack_rows(w):
    u = lax.bitcast_convert_type(w, jnp.uint32)
    lo = lax.bitcast_convert_type(u << 16, F32)
    hi = lax.bitcast_convert_type(u & jnp.uint32(0xFFFF0000), F32)
    return lo, hi


def _ada_kernel(c_ref, w_ref, b_ref, o_ref):
    cs = _silu(c_ref[...])
    o_ref[...] = jnp.dot(cs, w_ref[...], preferred_element_type=F32,
                         precision=lax.Precision.HIGHEST) + b_ref[...]


def _ada(c, w, b):
    R, D = c.shape
    N = w.shape[1]
    tn = 1024
    return pl.pallas_call(
        _ada_kernel,
        out_shape=jax.ShapeDtypeStruct((R, N), F32),
        grid=(N // tn,),
        in_specs=[pl.BlockSpec((R, D), lambda j: (0, 0)),
                  pl.BlockSpec((D, tn), lambda j: (0, j)),
                  pl.BlockSpec((1, tn), lambda j: (0, j))],
        out_specs=pl.BlockSpec((R, tn), lambda j: (0, j)),
        compiler_params=_cparams(("arbitrary",)),
        name="ada_mod",
    )(c, w, b)


def _rms(x):
    return x * lax.rsqrt(jnp.mean(x * x, axis=-1, keepdims=True) + EPS)


def _proj_kernel(x_ref, mod_ref, g_ref, w_ref, posk_ref, grp_ref,
                 o_ref, q_ref, ka_ref, v_ref, nrm_ref, *, qscale):
    x = x_ref[...]
    mod = mod_ref[0]
    h = _rms(x) * g_ref[...] * (1.0 + mod[1:2]) + mod[0:1]
    acc = jnp.dot(h.astype(BF16), w_ref[...], preferred_element_type=F32)
    qw = ATT_WIDTH
    qb = (acc[:, :qw] * qscale).astype(BF16)
    kb = acc[:, qw:2 * qw].astype(BF16)
    vb = acc[:, 2 * qw:3 * qw].astype(BF16)
    o_ref[...] = acc[:, 3 * qw:].astype(BF16)
    for hh in range(ATT_HEADS):
        cols = slice(hh * HEAD_W, (hh + 1) * HEAD_W)
        q_ref[hh] = qb[:, cols]
        v_ref[hh] = vb[:, cols]
        ka_ref[hh, :, :HEAD_W] = kb[:, cols]
        ka_ref[hh, :, HEAD_W:] = posk_ref[:, cols]
    for idx, t in enumerate((qb, kb)):
        tf = t.astype(F32)
        n2 = jnp.dot((tf * tf).astype(BF16), grp_ref[...], preferred_element_type=F32)
        nrm_ref[0, idx] = jnp.broadcast_to(jnp.max(n2, axis=0, keepdims=True), (8, HEAD_W))


def _proj(xt, mod, g, w, posk, S, tm):
    T, D = xt.shape
    N = w.shape[1]
    qscale = ATT_QK_DIM ** -0.5 * LOG2E
    nblk = T // tm
    spb = S // tm
    grp = (jnp.arange(ATT_WIDTH)[:, None] // ATT_QK_DIM == jnp.arange(HEAD_W)[None, :]).astype(BF16)
    return pl.pallas_call(
        functools.partial(_proj_kernel, qscale=qscale),
        out_shape=(jax.ShapeDtypeStruct((T, N - 3 * ATT_WIDTH), BF16),
                   jax.ShapeDtypeStruct((ATT_HEADS, T, HEAD_W), BF16),
                   jax.ShapeDtypeStruct((ATT_HEADS, T, 2 * HEAD_W), BF16),
                   jax.ShapeDtypeStruct((ATT_HEADS, T, HEAD_W), BF16),
                   jax.ShapeDtypeStruct((nblk, 2, 8, HEAD_W), F32)),
        grid=(nblk,),
        in_specs=[pl.BlockSpec((tm, D), lambda i: (i, 0)),
                  pl.BlockSpec((1, 6, D), lambda i: ((i * tm) // S, 0, 0)),
                  pl.BlockSpec((1, D), lambda i: (0, 0)),
                  pl.BlockSpec((D, N), lambda i: (0, 0)),
                  pl.BlockSpec((tm, ATT_WIDTH), lambda i: (i % spb, 0)),
                  pl.BlockSpec((ATT_WIDTH, HEAD_W), lambda i: (0, 0))],
        out_specs=(pl.BlockSpec((tm, N - 3 * ATT_WIDTH), lambda i: (i, 0)),
                   pl.BlockSpec((ATT_HEADS, tm, HEAD_W), lambda i: (0, i, 0)),
                   pl.BlockSpec((ATT_HEADS, tm, 2 * HEAD_W), lambda i: (0, i, 0)),
                   pl.BlockSpec((ATT_HEADS, tm, HEAD_W), lambda i: (0, i, 0)),
                   pl.BlockSpec((1, 2, 8, HEAD_W), lambda i: (i, 0, 0, 0))),
        compiler_params=_cparams(("parallel",)),
        name="norm1_proj",
    )(xt, mod, g, w, posk, grp)


def _split3(x):
    hi = x.astype(BF16)
    r = x - hi.astype(F32)
    mid = r.astype(BF16)
    lo = (r - mid.astype(F32)).astype(BF16)
    return hi, mid, lo


def _alibi_tables(slopes2, S):
    H = slopes2.shape[0]
    a = slopes2[:, None] * jnp.arange(S, dtype=F32)[None, :]
    ah, am, al = (t.astype(F32)[..., None] for t in _split3(a))
    lane = jnp.arange(HEAD_W)[None, None, :]
    terms = jnp.where(lane % 3 == 0, ah, jnp.where(lane % 3 == 1, am, al))
    posk = jnp.where(lane < 6, 1.0, jnp.where(lane < 9, terms, 0.0)).astype(BF16)
    posq = jnp.where(lane < 3, 0.0, jnp.where(lane < 6, -terms, jnp.where(lane < 9, 1.0, 0.0))).astype(BF16)
    return posq, posk.transpose(1, 0, 2).reshape(S, H * HEAD_W)


def _attn_kernel(slope_ref, lam_ref, q_ref, k_ref, v_ref, g_ref, o_ref,
                 qm_ref, m_ref, l_ref, acc_ref, *, tq, tk, nk):
    h = pl.program_id(1)
    qi = pl.program_id(2)
    ki = pl.program_id(3)

    @pl.when(ki == 0)
    def _init():
        q = q_ref[...]
        lane = lax.broadcasted_iota(I32, q.shape, 1)
        zero = jnp.zeros_like(q)
        qm_ref[0] = jnp.where(lane < ATT_QK_DIM, q, zero)
        qm_ref[1] = jnp.where(lane >= ATT_QK_DIM, q, zero)
        m_ref[...] = jnp.full(m_ref.shape, NEG_INF, F32)
        l_ref[...] = jnp.zeros(l_ref.shape, F32)
        acc_ref[...] = jnp.zeros(acc_ref.shape, F32)

    slope = slope_ref[h]
    k = k_ref[...]
    v = v_ref[...]
    row = lax.broadcasted_iota(I32, (tk, tq), 0)
    col = lax.broadcasted_iota(I32, (tk, tq), 1)
    dist = (col - row + (qi * tq - ki * tk)).astype(F32)
    bias = jnp.abs(dist) * (-slope)
    for m in range(2):
        s = lax.dot_general(k, qm_ref[m], (((1,), (1,)), ((), ())),
                            preferred_element_type=F32) + bias
        m_old = m_ref[m]
        m_new = jnp.maximum(m_old, jnp.max(s, axis=0, keepdims=True))
        alpha = jnp.exp2(m_old - m_new)
        p = jnp.exp2(s - m_new)
        l_ref[m] = alpha * l_ref[m] + jnp.sum(p, axis=0, keepdims=True)
        pv = lax.dot_general(v, p.astype(BF16), (((0,), (0,)), ((), ())),
                             preferred_element_type=F32)
        acc_ref[m] = alpha * acc_ref[m] + pv
        m_ref[m] = m_new

    @pl.when(ki == nk - 1)
    def _fin():
        lp = lam_ref[...]
        lam = (jnp.exp(jnp.sum(lp[0:1] * lp[1:2], axis=-1, keepdims=True))
               - jnp.exp(jnp.sum(lp[2:3] * lp[3:4], axis=-1, keepdims=True))
               + LAM_INIT)
        o = acc_ref[0] / l_ref[0] - lam * (acc_ref[1] / l_ref[1])
        ms = jnp.mean(o * o, axis=0, keepdims=True)
        y = o * lax.rsqrt(ms + EPS) * g_ref[...] * (1.0 - LAM_INIT)
        o_ref[...] = y.T.astype(o_ref.dtype)


def _attention(qh, kaug, vh, slopes, lam_p, subln_g, B, S, tq, tk):
    T = B * S
    nq, nk = S // tq, S // tk
    grid_spec = pltpu.PrefetchScalarGridSpec(
        num_scalar_prefetch=1,
        grid=(B, ATT_HEADS, nq, nk),
        in_specs=[
            pl.BlockSpec((4, ATT_QK_DIM), lambda b, h, qi, ki, s: (0, 0)),
            pl.BlockSpec((None, tq, HEAD_W), lambda b, h, qi, ki, s: (h, b * nq + qi, 0)),
            pl.BlockSpec((None, tk, HEAD_W), lambda b, h, qi, ki, s: (h, b * nk + ki, 0)),
            pl.BlockSpec((None, tk, HEAD_W), lambda b, h, qi, ki, s: (h, b * nk + ki, 0)),
            pl.BlockSpec((HEAD_W, 1), lambda b, h, qi, ki, s: (0, 0)),
        ],
        out_specs=pl.BlockSpec((tq, HEAD_W), lambda b, h, qi, ki, s: (b * nq + qi, h)),
        scratch_shapes=[pltpu.VMEM((2, tq, HEAD_W), BF16),
                        pltpu.VMEM((2, 1, tq), F32),
                        pltpu.VMEM((2, 1, tq), F32),
                        pltpu.VMEM((2, HEAD_W, tq), F32)],
    )
    return pl.pallas_call(
        functools.partial(_attn_kernel, tq=tq, tk=tk, nk=nk),
        out_shape=jax.ShapeDtypeStruct((T, ATT_WIDTH), BF16),
        grid_spec=grid_spec,
        compiler_params=_cparams(("parallel", "parallel", "parallel", "arbitrary")),
        name="diff_attention",
    )(slopes, lam_p, qh, kaug, vh, subln_g)


def _attn_finish(lam_ref, g_ref, o_ref, l_ref, acc_ref):
    lp = lam_ref[...]
    lam = (jnp.exp(jnp.sum(lp[0:1] * lp[1:2], axis=-1, keepdims=True))
           - jnp.exp(jnp.sum(lp[2:3] * lp[3:4], axis=-1, keepdims=True))
           + LAM_INIT)
    o = acc_ref[0] / l_ref[0] - lam * (acc_ref[1] / l_ref[1])
    ms = jnp.mean(o * o, axis=0, keepdims=True)
    y = o * lax.rsqrt(ms + EPS) * g_ref[...] * (1.0 - LAM_INIT)
    o_ref[...] = y.T.astype(o_ref.dtype)


def _attn_ref_kernel(sc_ref, lam_ref, q_ref, ka_ref, v_ref, g_ref, posq_ref, o_ref,
                     qa_ref, l_ref, acc_ref, *, tq, tk, nk):
    h = pl.program_id(1)
    qi = pl.program_id(2)
    ki = pl.program_id(3)

    @pl.when(ki == 0)
    def _init():
        q = q_ref[...]
        lane = lax.broadcasted_iota(I32, q.shape, 1)
        zero = jnp.zeros_like(q)
        kmax = sc_ref[ATT_HEADS]
        pa = posq_ref[...]
        for m in range(2):
            sel = (lane < ATT_QK_DIM) if m == 0 else (lane >= ATT_QK_DIM)
            qm = jnp.where(sel, q, zero)
            qf = qm.astype(F32)
            ub = jnp.sqrt(jnp.sum(qf * qf, axis=1, keepdims=True)) * kmax
            uh, um, ul = (t.astype(F32) for t in _split3(jnp.broadcast_to(ub, q.shape)))
            ubp = jnp.where(lane == 0, -uh, jnp.where(lane == 1, -um, jnp.where(lane == 2, -ul, 0.0)))
            ubp = ubp.astype(BF16)
            for var, pp in enumerate((pa + ubp, ubp - pa, ubp)):
                qa_ref[2 * var + m] = jnp.concatenate([qm, pp], axis=1)
        l_ref[...] = jnp.zeros(l_ref.shape, F32)
        acc_ref[...] = jnp.zeros(acc_ref.shape, F32)

    rel = qi * tq - ki * tk
    near = jnp.logical_and(rel > -tq, rel < tk)

    def body(with_bias):
        ka = ka_ref[...]
        if with_bias:
            row = lax.broadcasted_iota(I32, (tk, tq), 0)
            col = lax.broadcasted_iota(I32, (tk, tq), 1)
            bias = jnp.abs((col - row + rel).astype(F32)) * sc_ref[h]
            var = 2
        else:
            var = jnp.where(rel > 0, 0, 1)
        ss = [lax.dot_general(ka, qa_ref[2 * var + m], (((1,), (1,)), ((), ())),
                              preferred_element_type=F32) for m in range(2)]
        v = v_ref[...]
        for m in range(2):
            p = jnp.exp2(ss[m] + bias if with_bias else ss[m])
            l_ref[m] += jnp.sum(p, axis=0, keepdims=True)
            acc_ref[m] += lax.dot_general(v, p.astype(BF16), (((0,), (0,)), ((), ())),
                                          preferred_element_type=F32)

    @pl.when(near)
    def _():
        body(True)

    @pl.when(jnp.logical_not(near))
    def _():
        body(False)

    @pl.when(ki == nk - 1)
    def _fin():
        _attn_finish(lam_ref, g_ref, o_ref, l_ref, acc_ref)


def _attention_ref(qh, kaug, vh, scal, lam_p, subln_g, posq, B, S, tq, tk):
    T = B * S
    nq, nk = S // tq, S // tk
    grid_spec = pltpu.PrefetchScalarGridSpec(
        num_scalar_prefetch=1,
        grid=(B, ATT_HEADS, nq, nk),
        in_specs=[
            pl.BlockSpec((4, ATT_QK_DIM), lambda b, h, qi, ki, s: (0, 0)),
            pl.BlockSpec((None, tq, HEAD_W), lambda b, h, qi, ki, s: (h, b * nq + qi, 0)),
            pl.BlockSpec((None, tk, 2 * HEAD_W), lambda b, h, qi, ki, s: (h, b * nk + ki, 0)),
            pl.BlockSpec((None, tk, HEAD_W), lambda b, h, qi, ki, s: (h, b * nk + ki, 0)),
            pl.BlockSpec((HEAD_W, 1), lambda b, h, qi, ki, s: (0, 0)),
            pl.BlockSpec((None, tq, HEAD_W), lambda b, h, qi, ki, s: (h, qi, 0)),
        ],
        out_specs=pl.BlockSpec((tq, HEAD_W), lambda b, h, qi, ki, s: (b * nq + qi, h)),
        scratch_shapes=[pltpu.VMEM((6, tq, 2 * HEAD_W), BF16),
                        pltpu.VMEM((2, 1, tq), F32),
                        pltpu.VMEM((2, HEAD_W, tq), F32)],
    )
    return pl.pallas_call(
        functools.partial(_attn_ref_kernel, tq=tq, tk=tk, nk=nk),
        out_shape=jax.ShapeDtypeStruct((T, ATT_WIDTH), BF16),
        grid_spec=grid_spec,
        compiler_params=_cparams(("parallel", "parallel", "parallel", "arbitrary")),
        name="diff_attention_ref",
    )(scal, lam_p, qh, kaug, vh, subln_g, posq)


def _hg_chunk(q, kk, v, g, st, reverse):
    C = q.shape[0]
    nsub = C // HG_SUB
    row = lax.broadcasted_iota(I32, (C, HEAD_W), 0)
    b = _hg_cumsum(g, reverse)
    b_end = b[C - 1:C] if not reverse else b[0:1]

    qd = (q * jnp.exp(b)).astype(BF16)
    o_inter = lax.dot_general(qd, st.astype(BF16), (((1,), (1,)), ((), ())),
                              preferred_element_type=F32)

    ones = jnp.ones((HEAD_W, HEAD_W), BF16)
    srow = lax.broadcasted_iota(I32, (HG_SUB, HEAD_W), 0)
    vb = v.astype(BF16)
    outs = []
    for blk in range(nsub):
        lo, hi = blk * HG_SUB, (blk + 1) * HG_SUB
        qI, kI, vI, bI = q[lo:hi], kk[lo:hi], v[lo:hi], b[lo:hi]
        pieces = []
        for t in range(HG_SUB):
            dl = bI[t:t + 1] - bI
            keep = (srow <= t) if not reverse else (srow >= t)
            e = jnp.exp(jnp.where(keep, dl, NEG_INF))
            pieces.append(qI[t:t + 1] * kI * e)
        wst = jnp.concatenate(pieces, axis=0).astype(BF16)
        rsum = jnp.dot(wst, ones, preferred_element_type=F32)
        o_blk = jnp.sum(rsum.reshape(HG_SUB, HG_SUB, HEAD_W) * vI[None], axis=1)
        has_off = (blk > 0) if not reverse else (blk < nsub - 1)
        if has_off:
            if not reverse:
                r = b[lo - 1:lo]
                kmask = row < lo
            else:
                r = b[hi:hi + 1]
                kmask = row >= hi
            qs = (qI * jnp.exp(bI - r)).astype(BF16)
            ks = (kk * jnp.exp(jnp.where(kmask, r - b, NEG_INF))).astype(BF16)
            a = lax.dot_general(qs, ks, (((1,), (1,)), ((), ())),
                                preferred_element_type=F32)
            o_blk = o_blk + jnp.dot(a.astype(BF16), vb, preferred_element_type=F32)
        outs.append(o_blk)
    o = o_inter + jnp.concatenate(outs, axis=0)

    kd = (kk * jnp.exp(b_end - b)).astype(BF16)
    upd = lax.dot_general(vb, kd, (((0,), (0,)), ((), ())),
                          preferred_element_type=F32)
    st_new = st * jnp.exp(b_end) + upd
    return o, st_new


def _hg_cumsum(g, reverse):
    C = g.shape[0]
    row = lax.broadcasted_iota(I32, g.shape, 0)
    b = g
    d = 1
    while d < C:
        if not reverse:
            b = b + jnp.where(row >= d, pltpu.roll(b, d, axis=0), 0.0)
        else:
            b = b + jnp.where(row < C - d, pltpu.roll(b, C - d, axis=0), 0.0)
        d *= 2
    return b


def _hg_chunk_mild(q, kk, v, g, st, reverse):
    C = q.shape[0]
    nsub = C // HG_SUB
    row = lax.broadcasted_iota(I32, (C, HEAD_W), 0)
    b = _hg_cumsum(g, reverse)
    b_end = b[C - 1:C] if not reverse else b[0:1]
    qd = (q * jnp.exp(b)).astype(BF16)
    o_inter = lax.dot_general(qd, st.astype(BF16), (((1,), (1,)), ((), ())),
                              preferred_element_type=F32)
    vb = v.astype(BF16)
    qrow = lax.broadcasted_iota(I32, (HG_SUB, C), 0)
    kcol = lax.broadcasted_iota(I32, (HG_SUB, C), 1)
    zero_ref = jnp.zeros((1, HEAD_W), F32)
    a_rows = []
    for blk in range(nsub):
        lo, hi = blk * HG_SUB, (blk + 1) * HG_SUB
        if not reverse:
            r = b[lo - 1:lo] if blk > 0 else zero_ref
            seen = (kk[:hi] * jnp.exp(r - b[:hi])).astype(BF16)
            ks = seen if hi == C else jnp.concatenate([seen, jnp.zeros((C - hi, HEAD_W), BF16)], axis=0)
            amask = kcol <= qrow + lo
        else:
            r = b[hi:hi + 1] if blk < nsub - 1 else zero_ref
            seen = (kk[lo:] * jnp.exp(r - b[lo:])).astype(BF16)
            ks = seen if lo == 0 else jnp.concatenate([jnp.zeros((lo, HEAD_W), BF16), seen], axis=0)
            amask = kcol >= qrow + lo
        qs = (q[lo:hi] * jnp.exp(b[lo:hi] - r)).astype(BF16)
        a = lax.dot_general(qs, ks, (((1,), (1,)), ((), ())), preferred_element_type=F32)
        a_rows.append(jnp.where(amask, a, 0.0))
    a_full = jnp.concatenate(a_rows, axis=0).astype(BF16)
    o = o_inter + jnp.dot(a_full, vb, preferred_element_type=F32)
    kd = (kk * jnp.exp(b_end - b)).astype(BF16)
    upd = lax.dot_general(vb, kd, (((0,), (0,)), ((), ())), preferred_element_type=F32)
    return o, st * jnp.exp(b_end) + upd


def _hg_kernel(qf_ref, ff_ref, if_ref, qb_ref, fb_ref, ib_ref, lb_ref, thr_ref,
               of_ref, ob_ref, sf_ref, sb_ref, *, nchunk):
    j = pl.program_id(2)

    @pl.when(j == 0)
    def _init():
        sf_ref[...] = jnp.zeros(sf_ref.shape, F32)
        sb_ref[...] = jnp.zeros(sb_ref.shape, F32)

    lb = lb_ref[0]
    C = HG_CHUNK

    def prep(q_ref, f_ref, i_ref, lbd, rows):
        q = _silu(q_ref[rows, :].astype(F32))
        f = lbd + (1.0 - lbd) * jax.nn.sigmoid(f_ref[rows, :].astype(F32))
        return q, 1.0 - f, i_ref[rows, :].astype(F32), jnp.log(f)

    def step(chunk_fn, c, carry):
        rows_f = pl.ds(pl.multiple_of(c * C, C), C)
        rows_b = pl.ds(pl.multiple_of((nchunk - 1 - c) * C, C), C)
        o, st = chunk_fn(*prep(qf_ref, ff_ref, if_ref, lb[0:1], rows_f), sf_ref[...], reverse=False)
        of_ref[rows_f, :] = o
        sf_ref[...] = st
        o, st = chunk_fn(*prep(qb_ref, fb_ref, ib_ref, lb[1:2], rows_b), sb_ref[...], reverse=True)
        ob_ref[rows_b, :] = o
        sb_ref[...] = st
        return carry

    thr = thr_ref[0]
    above = jnp.minimum(
        jnp.min(jnp.where(ff_ref[...].astype(F32) >= thr[0:1], 1.0, 0.0)),
        jnp.min(jnp.where(fb_ref[...].astype(F32) >= thr[1:2], 1.0, 0.0)))
    mild = above > 0.5

    @pl.when(mild)
    def _():
        lax.fori_loop(0, nchunk, functools.partial(step, _hg_chunk_mild), 0)

    @pl.when(jnp.logical_not(mild))
    def _():
        lax.fori_loop(0, nchunk, functools.partial(step, _hg_chunk), 0)


def _hgrn2(proj, lbs, B, S):
    T = B * S
    nchunk = min(HG_STEP_CHUNKS, S // HG_CHUNK)
    C = HG_CHUNK * nchunk
    n = S // C
    nh = HG_HEADS
    qc, ffc, fbc, ic = 0, nh, 2 * nh, 3 * nh

    def fw(col):
        return pl.BlockSpec((C, HEAD_W), lambda b, h, j: (b * n + j, col + h))

    def bw(col):
        return pl.BlockSpec((C, HEAD_W), lambda b, h, j: (b * n + n - 1 - j, col + h))

    out_f = pl.BlockSpec((C, HEAD_W), lambda b, h, j: (b * n + j, h))
    out_b = pl.BlockSpec((C, HEAD_W), lambda b, h, j: (b * n + n - 1 - j, h))
    need = (math.exp(-HG_MILD_LOG_GATE) - lbs) / (1.0 - lbs)
    thr = jnp.where(need > 0.0, jnp.log(jnp.maximum(need, 1e-30) / (1.0 - need)), NEG_INF).astype(F32)
    per_head = pl.BlockSpec((1, 2, HEAD_W), lambda b, h, j: (h, 0, 0))
    return pl.pallas_call(
        functools.partial(_hg_kernel, nchunk=nchunk),
        out_shape=(jax.ShapeDtypeStruct((T, HG_WIDTH), F32),
                   jax.ShapeDtypeStruct((T, HG_WIDTH), F32)),
        grid=(B, nh, n),
        in_specs=[fw(qc), fw(ffc), fw(ic), bw(qc), bw(fbc), bw(ic), per_head, per_head],
        out_specs=(out_f, out_b),
        scratch_shapes=[pltpu.VMEM((HEAD_W, HEAD_W), F32),
                        pltpu.VMEM((HEAD_W, HEAD_W), F32)],
        compiler_params=_cparams(("parallel", "parallel", "arbitrary")),
        name="hgrn2_scan",
    )(proj, proj, proj, proj, proj, proj, lbs, thr)


def _mixout_kernel(x_ref, oa_ref, of_ref, ob_ref, gate_ref, mod_ref, hgg_ref, wo_ref,
                   n2_ref, wr_ref, x1_ref, h2_ref, lg_ref):
    mod = mod_ref[0]
    tm = x_ref.shape[0]
    sub = min(tm, MIXOUT_SUB_ROWS)
    for r0 in range(0, tm, sub):
        rows = slice(r0, r0 + sub)
        o = of_ref[rows, :] + ob_ref[rows, :]
        gate = _silu(gate_ref[rows, :].astype(F32))
        hg = jnp.concatenate(
            [_rms(o[:, h * HEAD_W:(h + 1) * HEAD_W]) * hgg_ref[...] for h in range(HG_HEADS)],
            axis=-1) * gate
        mix = (jnp.dot(oa_ref[rows, :], wo_ref[:ATT_WIDTH, :], preferred_element_type=F32)
               + jnp.dot(hg.astype(BF16), wo_ref[ATT_WIDTH:, :], preferred_element_type=F32))
        x1 = x_ref[rows, :] + mod[2:3] * mix
        x1_ref[rows, :] = x1
        h2 = _rms(x1) * n2_ref[...] * (1.0 + mod[4:5]) + mod[3:4]
        h2_ref[rows, :] = _pack_rows(h2)
        h_hi = h2.astype(BF16)
        h_mid = (h2 - h_hi.astype(F32)).astype(BF16)
        two = jnp.dot(h_hi, wr_ref[...], preferred_element_type=F32)
        logits = (two[:, :N_EXPERTS] + two[:, N_EXPERTS:]
                  + jnp.dot(h_mid, wr_ref[:, :N_EXPERTS], preferred_element_type=F32))
        lg_ref[:, rows] = logits.T


def _mixout(xt, o_att, o_fw, o_bw, proj, mod, hg_g, w_out, n2g, w_router, S, tm):
    T, D = xt.shape
    gate_col = proj.shape[1] // HG_WIDTH - 1
    row = lambda i: (i, 0)
    const = lambda i: (0, 0)
    return pl.pallas_call(
        _mixout_kernel,
        out_shape=(jax.ShapeDtypeStruct((T, D), F32),
                   jax.ShapeDtypeStruct((T, D // 2), I32),
                   jax.ShapeDtypeStruct((N_EXPERTS, T), F32)),
        grid=(T // tm,),
        in_specs=[pl.BlockSpec((tm, D), row),
                  pl.BlockSpec((tm, ATT_WIDTH), row),
                  pl.BlockSpec((tm, HG_WIDTH), row),
                  pl.BlockSpec((tm, HG_WIDTH), row),
                  pl.BlockSpec((tm, HG_WIDTH), lambda i: (i, gate_col)),
                  pl.BlockSpec((1, 6, D), lambda i: ((i * tm) // S, 0, 0)),
                  pl.BlockSpec((1, HEAD_W), const),
                  pl.BlockSpec(w_out.shape, const),
                  pl.BlockSpec((1, D), const),
                  pl.BlockSpec(w_router.shape, const)],
        out_specs=(pl.BlockSpec((tm, D), row),
                   pl.BlockSpec((tm, D // 2), row),
                   pl.BlockSpec((N_EXPERTS, tm), lambda i: (0, i))),
        compiler_params=_cparams(("parallel",)),
        name="mixout_norm2_router",
    )(xt, o_att, o_fw, o_bw, proj, mod, hg_g, w_out, n2g, w_router)


def _first_argmax(x, iota, size):
    mx = jnp.max(x, axis=0, keepdims=True)
    idx = jnp.min(jnp.where(x == mx, iota, size), axis=0, keepdims=True)
    return mx, idx


def _route_kernel(lg_ref, br_ref, tri_ref, e_ref, w_ref, r_ref, cnt_ref, carry_ref):
    i = pl.program_id(0)

    @pl.when(i == 0)
    def _init():
        carry_ref[...] = jnp.zeros(carry_ref.shape, F32)

    scores = jax.nn.sigmoid(lg_ref[...])
    biased = scores + br_ref[...]
    tm = scores.shape[1]
    giota = lax.broadcasted_iota(I32, (GROUP_SIZE, tm), 0)
    gs = []
    for g in range(N_GROUPS):
        blk = biased[g * GROUP_SIZE:(g + 1) * GROUP_SIZE]
        m1, i1 = _first_argmax(blk, giota, GROUP_SIZE)
        m2 = jnp.max(jnp.where(giota == i1, NEG_INF, blk), axis=0, keepdims=True)
        gs.append(m1 + m2)
    gsc = jnp.concatenate(gs, axis=0)
    gi = lax.broadcasted_iota(I32, (N_GROUPS, tm), 0)
    gsel = jnp.zeros((N_GROUPS, tm), jnp.bool_)
    for _ in range(TOPK_GROUPS):
        _, idx = _first_argmax(gsc, gi, N_GROUPS)
        hit = gi == idx
        gsel = jnp.logical_or(gsel, hit)
        gsc = jnp.where(hit, NEG_INF, gsc)
    masked = jnp.concatenate(
        [jnp.where(gsel[g:g + 1], biased[g * GROUP_SIZE:(g + 1) * GROUP_SIZE], NEG_INF)
         for g in range(N_GROUPS)], axis=0)
    ei = lax.broadcasted_iota(I32, (N_EXPERTS, tm), 0)
    eidx, wts = [], []
    onehot = jnp.zeros((N_EXPERTS, tm), F32)
    for _ in range(TOP_K):
        _, idx = _first_argmax(masked, ei, N_EXPERTS)
        hit = ei == idx
        eidx.append(idx)
        wts.append(jnp.sum(jnp.where(hit, scores, 0.0), axis=0, keepdims=True))
        onehot = jnp.where(hit, 1.0, onehot)
        masked = jnp.where(hit, NEG_INF, masked)
    w = jnp.concatenate(wts, axis=0)
    w = w / (jnp.sum(w, axis=0, keepdims=True) + 1e-20) * ROUTE_SCALE
    e_ref[...] = jnp.concatenate(eidx, axis=0)
    w_ref[...] = w
    before = jnp.dot(onehot.astype(BF16), tri_ref[...], preferred_element_type=F32)
    before = before + carry_ref[...]
    ranks = [jnp.sum(jnp.where(ei == idx, before, 0.0), axis=0, keepdims=True) for idx in eidx]
    r_ref[...] = jnp.concatenate(ranks, axis=0).astype(I32)
    carry = carry_ref[...] + jnp.sum(onehot, axis=1, keepdims=True)
    carry_ref[...] = carry
    cnt_ref[...] = carry.astype(I32)


def _route(logits_t, b_router, tm):
    E, T = logits_t.shape
    tri = (jnp.arange(tm)[:, None] < jnp.arange(tm)[None, :]).astype(BF16)
    tok = lambda i: (0, i)
    const = lambda i: (0, 0)
    return pl.pallas_call(
        _route_kernel,
        out_shape=(jax.ShapeDtypeStruct((TOP_K, T), I32),
                   jax.ShapeDtypeStruct((TOP_K, T), F32),
                   jax.ShapeDtypeStruct((TOP_K, T), I32),
                   jax.ShapeDtypeStruct((E, 1), I32)),
        grid=(T // tm,),
        in_specs=[pl.BlockSpec((E, tm), tok),
                  pl.BlockSpec((E, 1), const),
                  pl.BlockSpec((tm, tm), const)],
        out_specs=(pl.BlockSpec((TOP_K, tm), tok),
                   pl.BlockSpec((TOP_K, tm), tok),
                   pl.BlockSpec((TOP_K, tm), tok),
                   pl.BlockSpec((E, 1), const)),
        scratch_shapes=[pltpu.VMEM((E, 1), F32)],
        compiler_params=_cparams(("arbitrary",)),
        name="route_topk",
    )(logits_t, b_router, tri)


def _pos_kernel(e_ref, r_ref, ps_ref, o_ref):
    e = e_ref[...]
    tm = e.shape[1]
    ei = lax.broadcasted_iota(I32, (N_EXPERTS, tm), 0)
    ps = ps_ref[...]
    rows = [jnp.sum(jnp.where(ei == e[k:k + 1], ps, 0), axis=0, keepdims=True)
            for k in range(TOP_K)]
    o_ref[...] = jnp.concatenate(rows, axis=0) + r_ref[...]


def _positions(eidx, rank, pstart, tm):
    K, T = eidx.shape
    tok = lambda i: (0, i)
    return pl.pallas_call(
        _pos_kernel,
        out_shape=jax.ShapeDtypeStruct((K, T), I32),
        grid=(T // tm,),
        in_specs=[pl.BlockSpec((K, tm), tok), pl.BlockSpec((K, tm), tok),
                  pl.BlockSpec((N_EXPERTS, 1), lambda i: (0, 0))],
        out_specs=pl.BlockSpec((K, tm), tok),
        compiler_params=_cparams(("parallel",)),
        name="dispatch_positions",
    )(eidx, rank, pstart)


def _sc_mesh():
    return plsc.VectorSubcoreMesh(core_axis_name="c", subcore_axis_name="s")


def _worker_id():
    return lax.axis_index("s") * SC_CORES + lax.axis_index("c")


def _window_positions(pos):
    K, T = pos.shape
    nwin = T // (SC_WORKERS * SC_WINDOW)
    assert nwin * SC_WORKERS * SC_WINDOW == T, T
    return pos.reshape(K, SC_WORKERS, nwin, SC_WINDOW).transpose(1, 2, 0, 3)


def _sc_dispatch(h, pos, P):
    T, D = h.shape
    pos4 = _window_positions(pos)
    NW, nwin, K, W = pos4.shape

    @functools.partial(
        pl.kernel, mesh=_sc_mesh(),
        out_type=jax.ShapeDtypeStruct((P, D), h.dtype),
        scratch_types=[pltpu.VMEM((K, W), I32), pltpu.VMEM((W, D), h.dtype),
                       pltpu.SemaphoreType.DMA],
        name="sc_dispatch")
    def k(h_hbm, pos_hbm, xs_hbm, idx_v, rows_v, sem):
        wid = _worker_id()

        @pl.loop(0, nwin)
        def _(j):
            base = (wid * nwin + j) * W
            pltpu.sync_copy(pos_hbm.at[wid, j], idx_v)
            pltpu.sync_copy(h_hbm.at[pl.ds(base, W)], rows_v)
            copies = [pltpu.async_copy(rows_v, xs_hbm.at[idx_v.at[kk]], sem)
                      for kk in range(K)]
            for c in copies:
                c.wait()

    return k(h, pos4)


def _sc_gather(ys, pos):
    P, D = ys.shape
    T = pos.shape[1]
    pos4 = _window_positions(pos)
    NW, nwin, K, W = pos4.shape

    @functools.partial(
        pl.kernel, mesh=_sc_mesh(),
        out_type=jax.ShapeDtypeStruct((K, T, D), ys.dtype),
        scratch_types=[pltpu.VMEM((K, W), I32), pltpu.VMEM((W, D), ys.dtype),
                       pltpu.SemaphoreType.DMA],
        name="sc_gather")
    def k(ys_hbm, pos_hbm, yg_hbm, idx_v, rows_v, sem):
        wid = _worker_id()

        @pl.loop(0, nwin)
        def _(j):
            base = (wid * nwin + j) * W
            pltpu.sync_copy(pos_hbm.at[wid, j], idx_v)
            for kk in range(K):
                pltpu.async_copy(ys_hbm.at[idx_v.at[kk]], rows_v, sem).wait()
                pltpu.sync_copy(rows_v, yg_hbm.at[kk, pl.ds(base, W)])

    return k(ys, pos4)


def _expert_kernel(be_ref, nv_ref, x_ref, wg_ref, wu_ref, wd_ref, o_ref):
    i = pl.program_id(0)

    @pl.when(i < nv_ref[0])
    def _():
        lo, hi = _unpack_rows(x_ref[...])
        x = jnp.concatenate([lo, hi], axis=1).astype(BF16)
        a = (_silu(jnp.dot(x, wg_ref[0].astype(BF16), preferred_element_type=F32))
             * jnp.dot(x, wu_ref[0].astype(BF16), preferred_element_type=F32))
        o_ref[...] = _pack_rows(jnp.dot(a.astype(BF16), wd_ref[0].astype(BF16),
                                        preferred_element_type=F32))


def _expert_block(T):
    return 2 * EXPERT_BLOCK if T * TOP_K >= 4 * EXPERT_BLOCK * N_EXPERTS else EXPERT_BLOCK


def _experts(xs, blk_e, nvalid, wg, wu, wd, bm):
    P, Dh = xs.shape
    D = 2 * Dh
    nb = P // bm
    rowmap = lambda i, be, nv: (jnp.minimum(i, nv[0] - 1), 0)
    wmap = lambda i, be, nv: (be[i], 0, 0)
    grid_spec = pltpu.PrefetchScalarGridSpec(
        num_scalar_prefetch=2,
        grid=(nb,),
        in_specs=[pl.BlockSpec((bm, Dh), rowmap),
                  pl.BlockSpec((1, D, EXPERT_DIM), wmap),
                  pl.BlockSpec((1, D, EXPERT_DIM), wmap),
                  pl.BlockSpec((1, EXPERT_DIM, D), wmap)],
        out_specs=pl.BlockSpec((bm, Dh), rowmap),
    )
    return pl.pallas_call(
        _expert_kernel,
        out_shape=jax.ShapeDtypeStruct((P, Dh), I32),
        grid_spec=grid_spec,
        compiler_params=_cparams(("arbitrary",)),
        name="expert_ffn",
    )(blk_e, nvalid, xs, wg, wu, wd)


def _final_kernel(yg_ref, w_ref, h2_ref, x1_ref, mod_ref, modf_ref, nf_ref,
                  sg_ref, su_ref, sd_ref, o_ref):
    w2 = w_ref[...].astype(BF16)
    acc = None
    for k in range(TOP_K):
        term = pltpu.bitcast(yg_ref[k], BF16) * w2[:, k:k + 1]
        acc = term if acc is None else acc + term
    y = jnp.concatenate(_unpack_rows(pltpu.bitcast(acc, I32)), axis=1)
    hb = jnp.concatenate(_unpack_rows(h2_ref[...]), axis=1).astype(BF16)
    a = (_silu(jnp.dot(hb, sg_ref[...], preferred_element_type=F32))
         * jnp.dot(hb, su_ref[...], preferred_element_type=F32))
    shared = jnp.dot(a.astype(BF16), sd_ref[...], preferred_element_type=F32)
    mod = mod_ref[0]
    modf = modf_ref[0]
    x2 = x1_ref[...] + mod[5:6] * (y + shared)
    o_ref[...] = _rms(x2) * nf_ref[...] * (1.0 + modf[1:2]) + modf[0:1]


def _final(yg, wts_t, h2, x1, mod, modf, nfg, sg, su, sd, S, tm):
    T, D = x1.shape
    row = lambda i: (i, 0)
    const = lambda i: (0, 0)
    bat = lambda i: ((i * tm) // S, 0, 0)
    return pl.pallas_call(
        _final_kernel,
        out_shape=jax.ShapeDtypeStruct((T, D), F32),
        grid=(T // tm,),
        in_specs=[pl.BlockSpec((TOP_K, tm, D // 2), lambda i: (0, i, 0)),
                  pl.BlockSpec((2 * tm, TOP_K), row),
                  pl.BlockSpec((tm, D // 2), row),
                  pl.BlockSpec((tm, D), row),
                  pl.BlockSpec((1, 6, D), bat),
                  pl.BlockSpec((1, 2, D), bat),
                  pl.BlockSpec((1, D), const),
                  pl.BlockSpec(sg.shape, const),
                  pl.BlockSpec(su.shape, const),
                  pl.BlockSpec(sd.shape, const)],
        out_specs=pl.BlockSpec((tm, D), row),
        compiler_params=_cparams(("parallel",)),
        name="combine_shared_final",
    )(yg, wts_t, h2, x1, mod, modf, nfg, sg, su, sd)


def _tile(n, pref):
    t = min(n, pref)
    assert n % t == 0, (n, pref)
    return t


def _plan_dispatch(logits_t, p):
    T = logits_t.shape[1]
    tm_r = _tile(T, 512)
    eidx, wts, rank, counts = _route(logits_t, p["b_router"], tm_r)
    bm = _expert_block(T)
    counts = counts[:, 0]
    padded = (counts + bm - 1) // bm * bm
    pends = jnp.cumsum(padded)
    pstart = (pends - padded).astype(I32)
    nb = (T * TOP_K + N_EXPERTS * bm) // bm
    pos = _positions(eidx, rank, pstart[:, None], tm_r)
    blk_e = jnp.minimum(jnp.searchsorted(pends, jnp.arange(nb) * bm, side="right"),
                        N_EXPERTS - 1).astype(I32)
    nvalid = (pends[-1] // bm).astype(I32)
    blk_e = jnp.where(jnp.arange(nb) < nvalid, blk_e, blk_e[jnp.maximum(nvalid - 1, 0)])
    return jnp.repeat(wts.T, 2, axis=0), pos, blk_e, nvalid[None]


NORM_SLACK = 1.02
REF_GAP_LIMIT = 100.0


def _diff_attention(qh, kaug, vh, nrm, posq, p, B, S):
    qmax = jnp.sqrt(jnp.max(nrm[:, 0, 0, :2 * ATT_HEADS])) * NORM_SLACK
    kmax = jnp.sqrt(jnp.max(nrm[:, 1, 0, :2 * ATT_HEADS])) * NORM_SLACK
    in_range = 2.0 * qmax * kmax <= REF_GAP_LIMIT
    scal = jnp.concatenate([-p["slopes"], kmax[None]]).astype(F32)
    tq = _tile(S, 1024)
    tk = 2 * tq if S >= 16 * tq else tq

    def single_pass(_):
        return _attention_ref(qh, kaug, vh, scal, p["lam_p"], p["subln_g"], posq, B, S, tq, tk)

    def online(_):
        return _attention(qh, kaug, vh, p["slopes"], p["lam_p"], p["subln_g"], B, S,
                          _tile(S, 256), _tile(S, 512))

    return lax.cond(in_range, single_pass, online, None)


def _mixer_and_routing(x, mod, p):
    B, S, D = x.shape
    T = B * S
    xt = x.reshape(T, D)
    tm = _tile(S, 256)
    posq, posk = _alibi_tables(p["slopes"], S)
    proj, qh, kaug, vh, nrm = _proj(xt, mod, p["norm1_g"], p["w_in"], posk, S, tm)
    o_att = _diff_attention(qh, kaug, vh, nrm, posq, p, B, S)
    o_fw, o_bw = _hgrn2(proj, p["lbs"], B, S)
    x1, h2, logits_t = _mixout(xt, o_att, o_fw, o_bw, proj, mod, p["hg_norm_g"], p["w_out"],
                               p["norm2_g"], p["w_router"], S, _tile(S, 2 * MIXOUT_SUB_ROWS))
    wts, pos, blk_e, nvalid = _plan_dispatch(logits_t, p)
    return dict(x1=x1, h2=h2, wts=wts, pos=pos, blk_e=blk_e, nvalid=nvalid, mod=mod, shape=(B, S, D))


def _expert_rows(st, p, after=None):
    T = st["h2"].shape[0]
    bm = _expert_block(T)
    xs = _sc_dispatch(st["h2"], st["pos"], T * TOP_K + N_EXPERTS * bm)
    if after is not None:
        xs, _ = lax.optimization_barrier((xs, after))
    return _experts(xs, st["blk_e"], st["nvalid"], p["wg"], p["wu"], p["wd"], bm)


def _combine(st, ys, modf, p):
    B, S, D = st["shape"]
    yg = _sc_gather(ys, st["pos"])
    out = _final(yg, st["wts"], st["h2"], st["x1"], st["mod"], modf, p["normf_g"],
                 p["sg"], p["su"], p["sd"], S, _tile(S, 256))
    return out.reshape(B, S, D)


def _trunk(x, mod, modf, p):
    st = _mixer_and_routing(x, mod, p)
    return _combine(st, _expert_rows(st, p), modf, p)


def kernel(x_prompt, x_sample, c_prompt, c_sample, w_ada, b_ada, norm1_g, w_in, lam_q1, lam_k1, lam_q2, lam_k2, subln_g, hg_lb_logits, hg_norm_g, w_out, norm2_g, w_router, b_router, w_exp_gate, w_exp_up, w_exp_down, w_sh_gate, w_sh_up, w_sh_down, w_ada_f, b_ada_f, normf_g):
    D = x_prompt.shape[-1]
    Bp, Bs = c_prompt.shape[0], c_sample.shape[0]
    c_all = jnp.concatenate([c_prompt, c_sample], axis=0)
    R = -(-c_all.shape[0] // 8) * 8
    c_all = jnp.pad(c_all, ((0, R - c_all.shape[0]), (0, 0)))
    mod6 = _ada(c_all, w_ada[0], b_ada[0][None]).reshape(R, 6, D)
    mod2 = _ada(c_all, w_ada_f, b_ada_f[None]).reshape(R, 2, D)

    lbs = jax.nn.softmax(hg_lb_logits.astype(F32), axis=0)[0]
    lbs = lbs.reshape(2, HG_HEADS, HEAD_W).transpose(1, 0, 2)
    slopes = (jnp.exp2(-8.0 * (jnp.arange(ATT_HEADS, dtype=F32) + 1.0) / ATT_HEADS) * LOG2E)
    p = dict(
        norm1_g=norm1_g[0][None], w_in=w_in[0].astype(BF16),
        slopes=slopes.astype(F32),
        lam_p=jnp.stack([lam_q1[0], lam_k1[0], lam_q2[0], lam_k2[0]]).astype(F32),
        subln_g=subln_g[0][:, None], lbs=lbs, hg_norm_g=hg_norm_g[0][None],
        w_out=w_out[0].astype(BF16), norm2_g=norm2_g[0][None],
        w_router=jnp.concatenate(_split3(w_router[0].astype(F32))[:2], axis=1),
        b_router=b_router[0][:, None],
        wg=w_exp_gate[0], wu=w_exp_up[0], wd=w_exp_down[0],
        sg=w_sh_gate[0].astype(BF16), su=w_sh_up[0].astype(BF16), sd=w_sh_down[0].astype(BF16),
        normf_g=normf_g[None],
    )
    st_p = _mixer_and_routing(x_prompt, mod6[:Bp], p)
    x_sample, _ = lax.optimization_barrier((x_sample, st_p["pos"]))
    st_s = _mixer_and_routing(x_sample, mod6[Bp:Bp + Bs], p)
    ys_p = _expert_rows(st_p, p)
    ys_s = _expert_rows(st_s, p, after=ys_p)
    y_prompt = _combine(st_p, ys_p, mod2[:Bp], p)
    y_sample = _combine(st_s, ys_s, mod2[Bp:Bp + Bs], p)
    return (y_prompt, y_sample)
```

```python
import functools
import math

import jax
import jax.numpy as jnp
from jax import lax
from jax.experimental import pallas as pl
from jax.experimental.pallas import tpu as pltpu
from jax.experimental.pallas import tpu_sc as plsc

F32 = jnp.float32
BF16 = jnp.bfloat16
I32 = jnp.int32

EPS = 1e-6
LOG2E = 1.4426950408889634

ATT_HEADS = 4
ATT_QK_DIM = 64
HEAD_W = 128
ATT_WIDTH = ATT_HEADS * HEAD_W
HG_HEADS = 4
HG_WIDTH = HG_HEADS * HEAD_W
N_EXPERTS = 256
TOP_K = 8
N_GROUPS = 8
TOPK_GROUPS = 4
GROUP_SIZE = N_EXPERTS // N_GROUPS
EXPERT_DIM = 256
ROUTE_SCALE = 2.5
LAM_INIT = 0.8 - 0.6 * math.exp(-0.3 * 0)

HG_CHUNK = 128
HG_SUB = 16
HG_MILD_LOG_GATE = 3.75
HG_STEP_CHUNKS = 8
EXPERT_BLOCK = 512
MIXOUT_SUB_ROWS = 256
SC_CORES = 2
SC_SUBCORES = 16
SC_WORKERS = SC_CORES * SC_SUBCORES
SC_WINDOW = 32
VMEM_LIMIT = 56 * 1024 * 1024

NEG_INF = float("-inf")


def _cparams(sem):
    return pltpu.CompilerParams(dimension_semantics=sem, vmem_limit_bytes=VMEM_LIMIT)


def _silu(x):
    return x * jax.nn.sigmoid(x)


def _pack_rows(x):
    half = x.shape[1] // 2
    bits = lax.bitcast_convert_type(x.astype(BF16).astype(F32), jnp.uint32)
    return lax.bitcast_convert_type((bits[:, :half] >> 16) | bits[:, half:], I32)


def _unpack_rows(w):
    u = lax.bitcast_convert_type(w, jnp.uint32)
    lo = lax.bitcast_convert_type(u << 16, F32)
    hi = lax.bitcast_convert_type(u & jnp.uint32(0xFFFF0000), F32)
    return lo, hi


def _ada_kernel(c_ref, w_ref, b_ref, o_ref):
    cs = _silu(c_ref[...])
    o_ref[...] = jnp.dot(cs, w_ref[...], preferred_element_type=F32,
                         precision=lax.Precision.HIGHEST) + b_ref[...]


def _ada(c, w, b):
    R, D = c.shape
    N = w.shape[1]
    tn = 1024
    return pl.pallas_call(
        _ada_kernel,
        out_shape=jax.ShapeDtypeStruct((R, N), F32),
        grid=(N // tn,),
        in_specs=[pl.BlockSpec((R, D), lambda j: (0, 0)),
                  pl.BlockSpec((D, tn), lambda j: (0, j)),
                  pl.BlockSpec((1, tn), lambda j: (0, j))],
        out_specs=pl.BlockSpec((R, tn), lambda j: (0, j)),
        compiler_params=_cparams(("arbitrary",)),
        name="ada_mod",
    )(c, w, b)


def _rms(x):
    return x * lax.rsqrt(jnp.mean(x * x, axis=-1, keepdims=True) + EPS)


def _proj_kernel(x_ref, mod_ref, g_ref, w_ref, posk_ref, grp_ref,
                 o_ref, q_ref, ka_ref, v_ref, nrm_ref, *, qscale):
    x = x_ref[...]
    mod = mod_ref[0]
    h = _rms(x) * g_ref[...] * (1.0 + mod[1:2]) + mod[0:1]
    acc = jnp.dot(h.astype(BF16), w_ref[...], preferred_element_type=F32)
    qw = ATT_WIDTH
    qb = (acc[:, :qw] * qscale).astype(BF16)
    kb = acc[:, qw:2 * qw].astype(BF16)
    vb = acc[:, 2 * qw:3 * qw].astype(BF16)
    o_ref[...] = acc[:, 3 * qw:].astype(BF16)
    for hh in range(ATT_HEADS):
        cols = slice(hh * HEAD_W, (hh + 1) * HEAD_W)
        q_ref[hh] = qb[:, cols]
        v_ref[hh] = vb[:, cols]
        ka_ref[hh, :, :HEAD_W] = kb[:, cols]
        ka_ref[hh, :, HEAD_W:] = posk_ref[:, cols]
    for idx, t in enumerate((qb, kb)):
        tf = t.astype(F32)
        n2 = jnp.dot((tf * tf).astype(BF16), grp_ref[...], preferred_element_type=F32)
        nrm_ref[0, idx] = jnp.broadcast_to(jnp.max(n2, axis=0, keepdims=True), (8, HEAD_W))


def _proj(xt, mod, g, w, posk, S, tm):
    T, D = xt.shape
    N = w.shape[1]
    qscale = ATT_QK_DIM ** -0.5 * LOG2E
    nblk = T // tm
    spb = S // tm
    grp = (jnp.arange(ATT_WIDTH)[:, None] // ATT_QK_DIM == jnp.arange(HEAD_W)[None, :]).astype(BF16)
    return pl.pallas_call(
        functools.partial(_proj_kernel, qscale=qscale),
        out_shape=(jax.ShapeDtypeStruct((T, N - 3 * ATT_WIDTH), BF16),
                   jax.ShapeDtypeStruct((ATT_HEADS, T, HEAD_W), BF16),
                   jax.ShapeDtypeStruct((ATT_HEADS, T, 2 * HEAD_W), BF16),
                   jax.ShapeDtypeStruct((ATT_HEADS, T, HEAD_W), BF16),
                   jax.ShapeDtypeStruct((nblk, 2, 8, HEAD_W), F32)),
        grid=(nblk,),
        in_specs=[pl.BlockSpec((tm, D), lambda i: (i, 0)),
                  pl.BlockSpec((1, 6, D), lambda i: ((i * tm) // S, 0, 0)),
                  pl.BlockSpec((1, D), lambda i: (0, 0)),
                  pl.BlockSpec((D, N), lambda i: (0, 0)),
                  pl.BlockSpec((tm, ATT_WIDTH), lambda i: (i % spb, 0)),
                  pl.BlockSpec((ATT_WIDTH, HEAD_W), lambda i: (0, 0))],
        out_specs=(pl.BlockSpec((tm, N - 3 * ATT_WIDTH), lambda i: (i, 0)),
                   pl.BlockSpec((ATT_HEADS, tm, HEAD_W), lambda i: (0, i, 0)),
                   pl.BlockSpec((ATT_HEADS, tm, 2 * HEAD_W), lambda i: (0, i, 0)),
                   pl.BlockSpec((ATT_HEADS, tm, HEAD_W), lambda i: (0, i, 0)),
                   pl.BlockSpec((1, 2, 8, HEAD_W), lambda i: (i, 0, 0, 0))),
        compiler_params=_cparams(("parallel",)),
        name="norm1_proj",
    )(xt, mod, g, w, posk, grp)


def _split3(x):
    hi = x.astype(BF16)
    r = x - hi.astype(F32)
    mid = r.astype(BF16)
    lo = (r - mid.astype(F32)).astype(BF16)
    return hi, mid, lo


def _alibi_tables(slopes2, S):
    H = slopes2.shape[0]
    a = slopes2[:, None] * jnp.arange(S, dtype=F32)[None, :]
    ah, am, al = (t.astype(F32)[..., None] for t in _split3(a))
    lane = jnp.arange(HEAD_W)[None, None, :]
    terms = jnp.where(lane % 3 == 0, ah, jnp.where(lane % 3 == 1, am, al))
    posk = jnp.where(lane < 6, 1.0, jnp.where(lane < 9, terms, 0.0)).astype(BF16)
    posq = jnp.where(lane < 3, 0.0, jnp.where(lane < 6, -terms, jnp.where(lane < 9, 1.0, 0.0))).astype(BF16)
    return posq, posk.transpose(1, 0, 2).reshape(S, H * HEAD_W)


def _attn_kernel(slope_ref, lam_ref, q_ref, k_ref, v_ref, g_ref, o_ref,
                 qm_ref, m_ref, l_ref, acc_ref, *, tq, tk, nk):
    h = pl.program_id(1)
    qi = pl.program_id(2)
    ki = pl.program_id(3)

    @pl.when(ki == 0)
    def _init():
        q = q_ref[...]
        lane = lax.broadcasted_iota(I32, q.shape, 1)
        zero = jnp.zeros_like(q)
        qm_ref[0] = jnp.where(lane < ATT_QK_DIM, q, zero)
        qm_ref[1] = jnp.where(lane >= ATT_QK_DIM, q, zero)
        m_ref[...] = jnp.full(m_ref.shape, NEG_INF, F32)
        l_ref[...] = jnp.zeros(l_ref.shape, F32)
        acc_ref[...] = jnp.zeros(acc_ref.shape, F32)

    slope = slope_ref[h]
    k = k_ref[...]
    v = v_ref[...]
    row = lax.broadcasted_iota(I32, (tk, tq), 0)
    col = lax.broadcasted_iota(I32, (tk, tq), 1)
    dist = (col - row + (qi * tq - ki * tk)).astype(F32)
    bias = jnp.abs(dist) * (-slope)
    for m in range(2):
        s = lax.dot_general(k, qm_ref[m], (((1,), (1,)), ((), ())),
                            preferred_element_type=F32) + bias
        m_old = m_ref[m]
        m_new = jnp.maximum(m_old, jnp.max(s, axis=0, keepdims=True))
        alpha = jnp.exp2(m_old - m_new)
        p = jnp.exp2(s - m_new)
        l_ref[m] = alpha * l_ref[m] + jnp.sum(p, axis=0, keepdims=True)
        pv = lax.dot_general(v, p.astype(BF16), (((0,), (0,)), ((), ())),
                             preferred_element_type=F32)
        acc_ref[m] = alpha * acc_ref[m] + pv
        m_ref[m] = m_new

    @pl.when(ki == nk - 1)
    def _fin():
        lp = lam_ref[...]
        lam = (jnp.exp(jnp.sum(lp[0:1] * lp[1:2], axis=-1, keepdims=True))
               - jnp.exp(jnp.sum(lp[2:3] * lp[3:4], axis=-1, keepdims=True))
               + LAM_INIT)
        o = acc_ref[0] / l_ref[0] - lam * (acc_ref[1] / l_ref[1])
        ms = jnp.mean(o * o, axis=0, keepdims=True)
        y = o * lax.rsqrt(ms + EPS) * g_ref[...] * (1.0 - LAM_INIT)
        o_ref[...] = y.T.astype(o_ref.dtype)


def _attention(qh, kaug, vh, slopes, lam_p, subln_g, B, S, tq, tk):
    T = B * S
    nq, nk = S // tq, S // tk
    grid_spec = pltpu.PrefetchScalarGridSpec(
        num_scalar_prefetch=1,
        grid=(B, ATT_HEADS, nq, nk),
        in_specs=[
            pl.BlockSpec((4, ATT_QK_DIM), lambda b, h, qi, ki, s: (0, 0)),
            pl.BlockSpec((None, tq, HEAD_W), lambda b, h, qi, ki, s: (h, b * nq + qi, 0)),
            pl.BlockSpec((None, tk, HEAD_W), lambda b, h, qi, ki, s: (h, b * nk + ki, 0)),
            pl.BlockSpec((None, tk, HEAD_W), lambda b, h, qi, ki, s: (h, b * nk + ki, 0)),
            pl.BlockSpec((HEAD_W, 1), lambda b, h, qi, ki, s: (0, 0)),
        ],
        out_specs=pl.BlockSpec((tq, HEAD_W), lambda b, h, qi, ki, s: (b * nq + qi, h)),
        scratch_shapes=[pltpu.VMEM((2, tq, HEAD_W), BF16),
                        pltpu.VMEM((2, 1, tq), F32),
                        pltpu.VMEM((2, 1, tq), F32),
                        pltpu.VMEM((2, HEAD_W, tq), F32)],
    )
    return pl.pallas_call(
        functools.partial(_attn_kernel, tq=tq, tk=tk, nk=nk),
        out_shape=jax.ShapeDtypeStruct((T, ATT_WIDTH), BF16),
        grid_spec=grid_spec,
        compiler_params=_cparams(("parallel", "parallel", "parallel", "arbitrary")),
        name="diff_attention",
    )(slopes, lam_p, qh, kaug, vh, subln_g)


def _attn_finish(lam_ref, g_ref, o_ref, l_ref, acc_ref):
    lp = lam_ref[...]
    lam = (jnp.exp(jnp.sum(lp[0:1] * lp[1:2], axis=-1, keepdims=True))
           - jnp.exp(jnp.sum(lp[2:3] * lp[3:4], axis=-1, keepdims=True))
           + LAM_INIT)
    o = acc_ref[0] / l_ref[0] - lam * (acc_ref[1] / l_ref[1])
    ms = jnp.mean(o * o, axis=0, keepdims=True)
    y = o * lax.rsqrt(ms + EPS) * g_ref[...] * (1.0 - LAM_INIT)
    o_ref[...] = y.T.astype(o_ref.dtype)


def _attn_ref_kernel(sc_ref, lam_ref, q_ref, ka_ref, v_ref, g_ref, posq_ref, o_ref,
                     qa_ref, l_ref, acc_ref, *, tq, tk, nk):
    h = pl.program_id(1)
    qi = pl.program_id(2)
    ki = pl.program_id(3)

    @pl.when(ki == 0)
    def _init():
        q = q_ref[...]
        lane = lax.broadcasted_iota(I32, q.shape, 1)
        zero = jnp.zeros_like(q)
        kmax = sc_ref[ATT_HEADS]
        pa = posq_ref[...]
        for m in range(2):
            sel = (lane < ATT_QK_DIM) if m == 0 else (lane >= ATT_QK_DIM)
            qm = jnp.where(sel, q, zero)
            qf = qm.astype(F32)
            ub = jnp.sqrt(jnp.sum(qf * qf, axis=1, keepdims=True)) * kmax
            uh, um, ul = (t.astype(F32) for t in _split3(jnp.broadcast_to(ub, q.shape)))
            ubp = jnp.where(lane == 0, -uh, jnp.where(lane == 1, -um, jnp.where(lane == 2, -ul, 0.0)))
            ubp = ubp.astype(BF16)
            for var, pp in enumerate((pa + ubp, ubp - pa, ubp)):
                qa_ref[2 * var + m] = jnp.concatenate([qm, pp], axis=1)
        l_ref[...] = jnp.zeros(l_ref.shape, F32)
        acc_ref[...] = jnp.zeros(acc_ref.shape, F32)

    rel = qi * tq - ki * tk
    near = jnp.logical_and(rel > -tq, rel < tk)

    def body(with_bias):
        ka = ka_ref[...]
        if with_bias:
            row = lax.broadcasted_iota(I32, (tk, tq), 0)
            col = lax.broadcasted_iota(I32, (tk, tq), 1)
            bias = jnp.abs((col - row + rel).astype(F32)) * sc_ref[h]
            var = 2
        else:
            var = jnp.where(rel > 0, 0, 1)
        ss = [lax.dot_general(ka, qa_ref[2 * var + m], (((1,), (1,)), ((), ())),
                              preferred_element_type=F32) for m in range(2)]
        v = v_ref[...]
        for m in range(2):
            p = jnp.exp2(ss[m] + bias if with_bias else ss[m])
            l_ref[m] += jnp.sum(p, axis=0, keepdims=True)
            acc_ref[m] += lax.dot_general(v, p.astype(BF16), (((0,), (0,)), ((), ())),
                                          preferred_element_type=F32)

    @pl.when(near)
    def _():
        body(True)

    @pl.when(jnp.logical_not(near))
    def _():
        body(False)

    @pl.when(ki == nk - 1)
    def _fin():
        _attn_finish(lam_ref, g_ref, o_ref, l_ref, acc_ref)


def _attention_ref(qh, kaug, vh, scal, lam_p, subln_g, posq, B, S, tq, tk):
    T = B * S
    nq, nk = S // tq, S // tk
    grid_spec = pltpu.PrefetchScalarGridSpec(
        num_scalar_prefetch=1,
        grid=(B, ATT_HEADS, nq, nk),
        in_specs=[
            pl.BlockSpec((4, ATT_QK_DIM), lambda b, h, qi, ki, s: (0, 0)),
            pl.BlockSpec((None, tq, HEAD_W), lambda b, h, qi, ki, s: (h, b * nq + qi, 0)),
            pl.BlockSpec((None, tk, 2 * HEAD_W), lambda b, h, qi, ki, s: (h, b * nk + ki, 0)),
            pl.BlockSpec((None, tk, HEAD_W), lambda b, h, qi, ki, s: (h, b * nk + ki, 0)),
            pl.BlockSpec((HEAD_W, 1), lambda b, h, qi, ki, s: (0, 0)),
            pl.BlockSpec((None, tq, HEAD_W), lambda b, h, qi, ki, s: (h, qi, 0)),
        ],
        out_specs=pl.BlockSpec((tq, HEAD_W), lambda b, h, qi, ki, s: (b * nq + qi, h)),
        scratch_shapes=[pltpu.VMEM((6, tq, 2 * HEAD_W), BF16),
                        pltpu.VMEM((2, 1, tq), F32),
                        pltpu.VMEM((2, HEAD_W, tq), F32)],
    )
    return pl.pallas_call(
        functools.partial(_attn_ref_kernel, tq=tq, tk=tk, nk=nk),
        out_shape=jax.ShapeDtypeStruct((T, ATT_WIDTH), BF16),
        grid_spec=grid_spec,
        compiler_params=_cparams(("parallel", "parallel", "parallel", "arbitrary")),
        name="diff_attention_ref",
    )(scal, lam_p, qh, kaug, vh, subln_g, posq)


def _hg_chunk(q, kk, v, g, st, reverse):
    C = q.shape[0]
    nsub = C // HG_SUB
    row = lax.broadcasted_iota(I32, (C, HEAD_W), 0)
    b = _hg_cumsum(g, reverse)
    b_end = b[C - 1:C] if not reverse else b[0:1]

    qd = (q * jnp.exp(b)).astype(BF16)
    o_inter = lax.dot_general(qd, st.astype(BF16), (((1,), (1,)), ((), ())),
                              preferred_element_type=F32)

    ones = jnp.ones((HEAD_W, HEAD_W), BF16)
    srow = lax.broadcasted_iota(I32, (HG_SUB, HEAD_W), 0)
    vb = v.astype(BF16)
    outs = []
    for blk in range(nsub):
        lo, hi = blk * HG_SUB, (blk + 1) * HG_SUB
        qI, kI, vI, bI = q[lo:hi], kk[lo:hi], v[lo:hi], b[lo:hi]
        pieces = []
        for t in range(HG_SUB):
            dl = bI[t:t + 1] - bI
            keep = (srow <= t) if not reverse else (srow >= t)
            e = jnp.exp(jnp.where(keep, dl, NEG_INF))
            pieces.append(qI[t:t + 1] * kI * e)
        wst = jnp.concatenate(pieces, axis=0).astype(BF16)
        rsum = jnp.dot(wst, ones, preferred_element_type=F32)
        o_blk = jnp.sum(rsum.reshape(HG_SUB, HG_SUB, HEAD_W) * vI[None], axis=1)
        has_off = (blk > 0) if not reverse else (blk < nsub - 1)
        if has_off:
            if not reverse:
                r = b[lo - 1:lo]
                kmask = row < lo
            else:
                r = b[hi:hi + 1]
                kmask = row >= hi
            qs = (qI * jnp.exp(bI - r)).astype(BF16)
            ks = (kk * jnp.exp(jnp.where(kmask, r - b, NEG_INF))).astype(BF16)
            a = lax.dot_general(qs, ks, (((1,), (1,)), ((), ())),
                                preferred_element_type=F32)
            o_blk = o_blk + jnp.dot(a.astype(BF16), vb, preferred_element_type=F32)
        outs.append(o_blk)
    o = o_inter + jnp.concatenate(outs, axis=0)

    kd = (kk * jnp.exp(b_end - b)).astype(BF16)
    upd = lax.dot_general(vb, kd, (((0,), (0,)), ((), ())),
                          preferred_element_type=F32)
    st_new = st * jnp.exp(b_end) + upd
    return o, st_new


def _hg_cumsum(g, reverse):
    C = g.shape[0]
    row = lax.broadcasted_iota(I32, g.shape, 0)
    b = g
    d = 1
    while d < C:
        if not reverse:
            b = b + jnp.where(row >= d, pltpu.roll(b, d, axis=0), 0.0)
        else:
            b = b + jnp.where(row < C - d, pltpu.roll(b, C - d, axis=0), 0.0)
        d *= 2
    return b


def _hg_chunk_mild(q, kk, v, g, st, reverse):
    C = q.shape[0]
    nsub = C // HG_SUB
    row = lax.broadcasted_iota(I32, (C, HEAD_W), 0)
    b = _hg_cumsum(g, reverse)
    b_end = b[C - 1:C] if not reverse else b[0:1]
    qd = (q * jnp.exp(b)).astype(BF16)
    o_inter = lax.dot_general(qd, st.astype(BF16), (((1,), (1,)), ((), ())),
                              preferred_element_type=F32)
    vb = v.astype(BF16)
    qrow = lax.broadcasted_iota(I32, (HG_SUB, C), 0)
    kcol = lax.broadcasted_iota(I32, (HG_SUB, C), 1)
    zero_ref = jnp.zeros((1, HEAD_W), F32)
    a_rows = []
    for blk in range(nsub):
        lo, hi = blk * HG_SUB, (blk + 1) * HG_SUB
        if not reverse:
            r = b[lo - 1:lo] if blk > 0 else zero_ref
            seen = (kk[:hi] * jnp.exp(r - b[:hi])).astype(BF16)
            ks = seen if hi == C else jnp.concatenate([seen, jnp.zeros((C - hi, HEAD_W), BF16)], axis=0)
            amask = kcol <= qrow + lo
        else:
            r = b[hi:hi + 1] if blk < nsub - 1 else zero_ref
            seen = (kk[lo:] * jnp.exp(r - b[lo:])).astype(BF16)
            ks = seen if lo == 0 else jnp.concatenate([jnp.zeros((lo, HEAD_W), BF16), seen], axis=0)
            amask = kcol >= qrow + lo
        qs = (q[lo:hi] * jnp.exp(b[lo:hi] - r)).astype(BF16)
        a = lax.dot_general(qs, ks, (((1,), (1,)), ((), ())), preferred_element_type=F32)
        a_rows.append(jnp.where(amask, a, 0.0))
    a_full = jnp.concatenate(a_rows, axis=0).astype(BF16)
    o = o_inter + jnp.dot(a_full, vb, preferred_element_type=F32)
    kd = (kk * jnp.exp(b_end - b)).astype(BF16)
    upd = lax.dot_general(vb, kd, (((0,), (0,)), ((), ())), preferred_element_type=F32)
    return o, st * jnp.exp(b_end) + upd


def _hg_kernel(qf_ref, ff_ref, if_ref, qb_ref, fb_ref, ib_ref, lb_ref, thr_ref,
               of_ref, ob_ref, sf_ref, sb_ref, *, nchunk):
    j = pl.program_id(2)

    @pl.when(j == 0)
    def _init():
        sf_ref[...] = jnp.zeros(sf_ref.shape, F32)
        sb_ref[...] = jnp.zeros(sb_ref.shape, F32)

    lb = lb_ref[0]
    C = HG_CHUNK

    def prep(q_ref, f_ref, i_ref, lbd, rows):
        q = _silu(q_ref[rows, :].astype(F32))
        f = lbd + (1.0 - lbd) * jax.nn.sigmoid(f_ref[rows, :].astype(F32))
        return q, 1.0 - f, i_ref[rows, :].astype(F32), jnp.log(f)

    def step(chunk_fn, c, carry):
        rows_f = pl.ds(pl.multiple_of(c * C, C), C)
        rows_b = pl.ds(pl.multiple_of((nchunk - 1 - c) * C, C), C)
        o, st = chunk_fn(*prep(qf_ref, ff_ref, if_ref, lb[0:1], rows_f), sf_ref[...], reverse=False)
        of_ref[rows_f, :] = o
        sf_ref[...] = st
        o, st = chunk_fn(*prep(qb_ref, fb_ref, ib_ref, lb[1:2], rows_b), sb_ref[...], reverse=True)
        ob_ref[rows_b, :] = o
        sb_ref[...] = st
        return carry

    thr = thr_ref[0]
    above = jnp.minimum(
        jnp.min(jnp.where(ff_ref[...].astype(F32) >= thr[0:1], 1.0, 0.0)),
        jnp.min(jnp.where(fb_ref[...].astype(F32) >= thr[1:2], 1.0, 0.0)))
    mild = above > 0.5

    @pl.when(mild)
    def _():
        lax.fori_loop(0, nchunk, functools.partial(step, _hg_chunk_mild), 0)

    @pl.when(jnp.logical_not(mild))
    def _():
        lax.fori_loop(0, nchunk, functools.partial(step, _hg_chunk), 0)


def _hgrn2(proj, lbs, B, S):
    T = B * S
    nchunk = min(HG_STEP_CHUNKS, S // HG_CHUNK)
    C = HG_CHUNK * nchunk
    n = S // C
    nh = HG_HEADS
    qc, ffc, fbc, ic = 0, nh, 2 * nh, 3 * nh

    def fw(col):
        return pl.BlockSpec((C, HEAD_W), lambda b, h, j: (b * n + j, col + h))

    def bw(col):
        return pl.BlockSpec((C, HEAD_W), lambda b, h, j: (b * n + n - 1 - j, col + h))

    out_f = pl.BlockSpec((C, HEAD_W), lambda b, h, j: (b * n + j, h))
    out_b = pl.BlockSpec((C, HEAD_W), lambda b, h, j: (b * n + n - 1 - j, h))
    need = (math.exp(-HG_MILD_LOG_GATE) - lbs) / (1.0 - lbs)
    thr = jnp.where(need > 0.0, jnp.log(jnp.maximum(need, 1e-30) / (1.0 - need)), NEG_INF).astype(F32)
    per_head = pl.BlockSpec((1, 2, HEAD_W), lambda b, h, j: (h, 0, 0))
    return pl.pallas_call(
        functools.partial(_hg_kernel, nchunk=nchunk),
        out_shape=(jax.ShapeDtypeStruct((T, HG_WIDTH), F32),
                   jax.ShapeDtypeStruct((T, HG_WIDTH), F32)),
        grid=(B, nh, n),
        in_specs=[fw(qc), fw(ffc), fw(ic), bw(qc), bw(fbc), bw(ic), per_head, per_head],
        out_specs=(out_f, out_b),
        scratch_shapes=[pltpu.VMEM((HEAD_W, HEAD_W), F32),
                        pltpu.VMEM((HEAD_W, HEAD_W), F32)],
        compiler_params=_cparams(("parallel", "parallel", "arbitrary")),
        name="hgrn2_scan",
    )(proj, proj, proj, proj, proj, proj, lbs, thr)


def _mixout_kernel(x_ref, oa_ref, of_ref, ob_ref, gate_ref, mod_ref, hgg_ref, wo_ref,
                   n2_ref, wr_ref, x1_ref, h2_ref, lg_ref):
    mod = mod_ref[0]
    tm = x_ref.shape[0]
    sub = min(tm, MIXOUT_SUB_ROWS)
    for r0 in range(0, tm, sub):
        rows = slice(r0, r0 + sub)
        o = of_ref[rows, :] + ob_ref[rows, :]
        gate = _silu(gate_ref[rows, :].astype(F32))
        hg = jnp.concatenate(
            [_rms(o[:, h * HEAD_W:(h + 1) * HEAD_W]) * hgg_ref[...] for h in range(HG_HEADS)],
            axis=-1) * gate
        mix = (jnp.dot(oa_ref[rows, :], wo_ref[:ATT_WIDTH, :], preferred_element_type=F32)
               + jnp.dot(hg.astype(BF16), wo_ref[ATT_WIDTH:, :], preferred_element_type=F32))
        x1 = x_ref[rows, :] + mod[2:3] * mix
        x1_ref[rows, :] = x1
        h2 = _rms(x1) * n2_ref[...] * (1.0 + mod[4:5]) + mod[3:4]
        h2_ref[rows, :] = _pack_rows(h2)
        h_hi = h2.astype(BF16)
        h_mid = (h2 - h_hi.astype(F32)).astype(BF16)
        two = jnp.dot(h_hi, wr_ref[...], preferred_element_type=F32)
        logits = (two[:, :N_EXPERTS] + two[:, N_EXPERTS:]
                  + jnp.dot(h_mid, wr_ref[:, :N_EXPERTS], preferred_element_type=F32))
        lg_ref[:, rows] = logits.T


def _mixout(xt, o_att, o_fw, o_bw, proj, mod, hg_g, w_out, n2g, w_router, S, tm):
    T, D = xt.shape
    gate_col = proj.shape[1] // HG_WIDTH - 1
    row = lambda i: (i, 0)
    const = lambda i: (0, 0)
    return pl.pallas_call(
        _mixout_kernel,
        out_shape=(jax.ShapeDtypeStruct((T, D), F32),
                   jax.ShapeDtypeStruct((T, D // 2), I32),
                   jax.ShapeDtypeStruct((N_EXPERTS, T), F32)),
        grid=(T // tm,),
        in_specs=[pl.BlockSpec((tm, D), row),
                  pl.BlockSpec((tm, ATT_WIDTH), row),
                  pl.BlockSpec((tm, HG_WIDTH), row),
                  pl.BlockSpec((tm, HG_WIDTH), row),
                  pl.BlockSpec((tm, HG_WIDTH), lambda i: (i, gate_col)),
                  pl.BlockSpec((1, 6, D), lambda i: ((i * tm) // S, 0, 0)),
                  pl.BlockSpec((1, HEAD_W), const),
                  pl.BlockSpec(w_out.shape, const),
                  pl.BlockSpec((1, D), const),
                  pl.BlockSpec(w_router.shape, const)],
        out_specs=(pl.BlockSpec((tm, D), row),
                   pl.BlockSpec((tm, D // 2), row),
                   pl.BlockSpec((N_EXPERTS, tm), lambda i: (0, i))),
        compiler_params=_cparams(("parallel",)),
        name="mixout_norm2_router",
    )(xt, o_att, o_fw, o_bw, proj, mod, hg_g, w_out, n2g, w_router)


def _first_argmax(x, iota, size):
    mx = jnp.max(x, axis=0, keepdims=True)
    idx = jnp.min(jnp.where(x == mx, iota, size), axis=0, keepdims=True)
    return mx, idx


def _route_kernel(lg_ref, br_ref, tri_ref, e_ref, w_ref, r_ref, cnt_ref, carry_ref):
    i = pl.program_id(0)

    @pl.when(i == 0)
    def _init():
        carry_ref[...] = jnp.zeros(carry_ref.shape, F32)

    scores = jax.nn.sigmoid(lg_ref[...])
    biased = scores + br_ref[...]
    tm = scores.shape[1]
    giota = lax.broadcasted_iota(I32, (GROUP_SIZE, tm), 0)
    gs = []
    for g in range(N_GROUPS):
        blk = biased[g * GROUP_SIZE:(g + 1) * GROUP_SIZE]
        m1, i1 = _first_argmax(blk, giota, GROUP_SIZE)
        m2 = jnp.max(jnp.where(giota == i1, NEG_INF, blk), axis=0, keepdims=True)
        gs.append(m1 + m2)
    gsc = jnp.concatenate(gs, axis=0)
    gi = lax.broadcasted_iota(I32, (N_GROUPS, tm), 0)
    gsel = jnp.zeros((N_GROUPS, tm), jnp.bool_)
    for _ in range(TOPK_GROUPS):
        _, idx = _first_argmax(gsc, gi, N_GROUPS)
        hit = gi == idx
        gsel = jnp.logical_or(gsel, hit)
        gsc = jnp.where(hit, NEG_INF, gsc)
    masked = jnp.concatenate(
        [jnp.where(gsel[g:g + 1], biased[g * GROUP_SIZE:(g + 1) * GROUP_SIZE], NEG_INF)
         for g in range(N_GROUPS)], axis=0)
    ei = lax.broadcasted_iota(I32, (N_EXPERTS, tm), 0)
    eidx, wts = [], []
    onehot = jnp.zeros((N_EXPERTS, tm), F32)
    for _ in range(TOP_K):
        _, idx = _first_argmax(masked, ei, N_EXPERTS)
        hit = ei == idx
        eidx.append(idx)
        wts.append(jnp.sum(jnp.where(hit, scores, 0.0), axis=0, keepdims=True))
        onehot = jnp.where(hit, 1.0, onehot)
        masked = jnp.where(hit, NEG_INF, masked)
    w = jnp.concatenate(wts, axis=0)
    w = w / (jnp.sum(w, axis=0, keepdims=True) + 1e-20) * ROUTE_SCALE
    e_ref[...] = jnp.concatenate(eidx, axis=0)
    w_ref[...] = w
    before = jnp.dot(onehot.astype(BF16), tri_ref[...], preferred_element_type=F32)
    before = before + carry_ref[...]
    ranks = [jnp.sum(jnp.where(ei == idx, before, 0.0), axis=0, keepdims=True) for idx in eidx]
    r_ref[...] = jnp.concatenate(ranks, axis=0).astype(I32)
    carry = carry_ref[...] + jnp.sum(onehot, axis=1, keepdims=True)
    carry_ref[...] = carry
    cnt_ref[...] = carry.astype(I32)


def _route(logits_t, b_router, tm):
    E, T = logits_t.shape
    tri = (jnp.arange(tm)[:, None] < jnp.arange(tm)[None, :]).astype(BF16)
    tok = lambda i: (0, i)
    const = lambda i: (0, 0)
    return pl.pallas_call(
        _route_kernel,
        out_shape=(jax.ShapeDtypeStruct((TOP_K, T), I32),
                   jax.ShapeDtypeStruct((TOP_K, T), F32),
                   jax.ShapeDtypeStruct((TOP_K, T), I32),
                   jax.ShapeDtypeStruct((E, 1), I32)),
        grid=(T // tm,),
        in_specs=[pl.BlockSpec((E, tm), tok),
                  pl.BlockSpec((E, 1), const),
                  pl.BlockSpec((tm, tm), const)],
        out_specs=(pl.BlockSpec((TOP_K, tm), tok),
                   pl.BlockSpec((TOP_K, tm), tok),
                   pl.BlockSpec((TOP_K, tm), tok),
                   pl.BlockSpec((E, 1), const)),
        scratch_shapes=[pltpu.VMEM((E, 1), F32)],
        compiler_params=_cparams(("arbitrary",)),
        name="route_topk",
    )(logits_t, b_router, tri)


def _pos_kernel(e_ref, r_ref, ps_ref, o_ref):
    e = e_ref[...]
    tm = e.shape[1]
    ei = lax.broadcasted_iota(I32, (N_EXPERTS, tm), 0)
    ps = ps_ref[...]
    rows = [jnp.sum(jnp.where(ei == e[k:k + 1], ps, 0), axis=0, keepdims=True)
            for k in range(TOP_K)]
    o_ref[...] = jnp.concatenate(rows, axis=0) + r_ref[...]


def _positions(eidx, rank, pstart, tm):
    K, T = eidx.shape
    tok = lambda i: (0, i)
    return pl.pallas_call(
        _pos_kernel,
        out_shape=jax.ShapeDtypeStruct((K, T), I32),
        grid=(T // tm,),
        in_specs=[pl.BlockSpec((K, tm), tok), pl.BlockSpec((K, tm), tok),
                  pl.BlockSpec((N_EXPERTS, 1), lambda i: (0, 0))],
        out_specs=pl.BlockSpec((K, tm), tok),
        compiler_params=_cparams(("parallel",)),
        name="dispatch_positions",
    )(eidx, rank, pstart)


def _sc_mesh():
    return plsc.VectorSubcoreMesh(core_axis_name="c", subcore_axis_name="s")


def _worker_id():
    return lax.axis_index("s") * SC_CORES + lax.axis_index("c")


def _window_positions(pos):
    K, T = pos.shape
    nwin = T // (SC_WORKERS * SC_WINDOW)
    assert nwin * SC_WORKERS * SC_WINDOW == T, T
    return pos.reshape(K, SC_WORKERS, nwin, SC_WINDOW).transpose(1, 2, 0, 3)


def _sc_dispatch(h, pos, P):
    T, D = h.shape
    pos4 = _window_positions(pos)
    NW, nwin, K, W = pos4.shape

    @functools.partial(
        pl.kernel, mesh=_sc_mesh(),
        out_type=jax.ShapeDtypeStruct((P, D), h.dtype),
        scratch_types=[pltpu.VMEM((K, W), I32), pltpu.VMEM((W, D), h.dtype),
                       pltpu.SemaphoreType.DMA],
        name="sc_dispatch")
    def k(h_hbm, pos_hbm, xs_hbm, idx_v, rows_v, sem):
        wid = _worker_id()

        @pl.loop(0, nwin)
        def _(j):
            base = (wid * nwin + j) * W
            pltpu.sync_copy(pos_hbm.at[wid, j], idx_v)
            pltpu.sync_copy(h_hbm.at[pl.ds(base, W)], rows_v)
            copies = [pltpu.async_copy(rows_v, xs_hbm.at[idx_v.at[kk]], sem)
                      for kk in range(K)]
            for c in copies:
                c.wait()

    return k(h, pos4)


def _sc_gather(ys, pos):
    P, D = ys.shape
    T = pos.shape[1]
    pos4 = _window_positions(pos)
    NW, nwin, K, W = pos4.shape

    @functools.partial(
        pl.kernel, mesh=_sc_mesh(),
        out_type=jax.ShapeDtypeStruct((K, T, D), ys.dtype),
        scratch_types=[pltpu.VMEM((K, W), I32), pltpu.VMEM((W, D), ys.dtype),
                       pltpu.SemaphoreType.DMA],
        name="sc_gather")
    def k(ys_hbm, pos_hbm, yg_hbm, idx_v, rows_v, sem):
        wid = _worker_id()

        @pl.loop(0, nwin)
        def _(j):
            base = (wid * nwin + j) * W
            pltpu.sync_copy(pos_hbm.at[wid, j], idx_v)
            for kk in range(K):
                pltpu.async_copy(ys_hbm.at[idx_v.at[kk]], rows_v, sem).wait()
                pltpu.sync_copy(rows_v, yg_hbm.at[kk, pl.ds(base, W)])

    return k(ys, pos4)


def _expert_kernel(be_ref, nv_ref, x_ref, wg_ref, wu_ref, wd_ref, o_ref):
    i = pl.program_id(0)

    @pl.when(i < nv_ref[0])
    def _():
        lo, hi = _unpack_rows(x_ref[...])
        x = jnp.concatenate([lo, hi], axis=1).astype(BF16)
        a = (_silu(jnp.dot(x, wg_ref[0].astype(BF16), preferred_element_type=F32))
             * jnp.dot(x, wu_ref[0].astype(BF16), preferred_element_type=F32))
        o_ref[...] = _pack_rows(jnp.dot(a.astype(BF16), wd_ref[0].astype(BF16),
                                        preferred_element_type=F32))


def _expert_block(T):
    return 2 * EXPERT_BLOCK if T * TOP_K >= 4 * EXPERT_BLOCK * N_EXPERTS else EXPERT_BLOCK


def _experts(xs, blk_e, nvalid, wg, wu, wd, bm):
    P, Dh = xs.shape
    D = 2 * Dh
    nb = P // bm
    steps = jnp.broadcast_to(jnp.arange(nb, dtype=I32)[:, None, None], (nb, 8, HEAD_W))

    def outer(be_ref, nv_ref, st_hbm, x_hbm, wg_hbm, wu_hbm, wd_hbm, o_hbm):
        def body(step_ref, x_ref, wg_ref, wu_ref, wd_ref, o_ref):
            @pl.when(step_ref[0, 0, 0] < nv_ref[0])
            def _():
                lo, hi = _unpack_rows(x_ref[...])
                x = jnp.concatenate([lo, hi], axis=1).astype(BF16)
                a = (_silu(jnp.dot(x, wg_ref[0].astype(BF16), preferred_element_type=F32))
                     * jnp.dot(x, wu_ref[0].astype(BF16), preferred_element_type=F32))
                o_ref[...] = _pack_rows(jnp.dot(a.astype(BF16), wd_ref[0].astype(BF16),
                                                preferred_element_type=F32))

        rowmap = lambda i: (i, 0)
        wmap = lambda i: (be_ref[i], 0, 0)
        ahead = pl.Buffered(3)
        pltpu.emit_pipeline(
            body,
            grid=(nb,),
            in_specs=[pl.BlockSpec((1, 8, HEAD_W), lambda i: (i, 0, 0)),
                      pl.BlockSpec((bm, Dh), rowmap, pipeline_mode=ahead),
                      pl.BlockSpec((1, D, EXPERT_DIM), wmap, pipeline_mode=ahead),
                      pl.BlockSpec((1, D, EXPERT_DIM), wmap, pipeline_mode=ahead),
                      pl.BlockSpec((1, EXPERT_DIM, D), wmap, pipeline_mode=ahead)],
            out_specs=[pl.BlockSpec((bm, Dh), rowmap)],
        )(st_hbm, x_hbm, wg_hbm, wu_hbm, wd_hbm, o_hbm)

    any_spec = pl.BlockSpec(memory_space=pl.ANY)
    return pl.pallas_call(
        outer,
        out_shape=jax.ShapeDtypeStruct((P, Dh), I32),
        grid_spec=pltpu.PrefetchScalarGridSpec(
            num_scalar_prefetch=2, grid=(1,), in_specs=[any_spec] * 5, out_specs=any_spec),
        compiler_params=_cparams(("arbitrary",)),
        name="expert_ffn",
    )(blk_e, nvalid, steps, xs, wg, wu, wd)


def _final_kernel(yg_ref, w_ref, h2_ref, x1_ref, mod_ref, modf_ref, nf_ref,
                  sg_ref, su_ref, sd_ref, o_ref):
    w2 = w_ref[...].astype(BF16)
    acc = None
    for k in range(TOP_K):
        term = pltpu.bitcast(yg_ref[k], BF16) * w2[:, k:k + 1]
        acc = term if acc is None else acc + term
    y = jnp.concatenate(_unpack_rows(pltpu.bitcast(acc, I32)), axis=1)
    hb = jnp.concatenate(_unpack_rows(h2_ref[...]), axis=1).astype(BF16)
    a = (_silu(jnp.dot(hb, sg_ref[...], preferred_element_type=F32))
         * jnp.dot(hb, su_ref[...], preferred_element_type=F32))
    shared = jnp.dot(a.astype(BF16), sd_ref[...], preferred_element_type=F32)
    mod = mod_ref[0]
    modf = modf_ref[0]
    x2 = x1_ref[...] + mod[5:6] * (y + shared)
    o_ref[...] = _rms(x2) * nf_ref[...] * (1.0 + modf[1:2]) + modf[0:1]


def _final(yg, wts_t, h2, x1, mod, modf, nfg, sg, su, sd, S, tm):
    T, D = x1.shape
    row = lambda i: (i, 0)
    const = lambda i: (0, 0)
    bat = lambda i: ((i * tm) // S, 0, 0)
    return pl.pallas_call(
        _final_kernel,
        out_shape=jax.ShapeDtypeStruct((T, D), F32),
        grid=(T // tm,),
        in_specs=[pl.BlockSpec((TOP_K, tm, D // 2), lambda i: (0, i, 0)),
                  pl.BlockSpec((2 * tm, TOP_K), row),
                  pl.BlockSpec((tm, D // 2), row),
                  pl.BlockSpec((tm, D), row),
                  pl.BlockSpec((1, 6, D), bat),
                  pl.BlockSpec((1, 2, D), bat),
                  pl.BlockSpec((1, D), const),
                  pl.BlockSpec(sg.shape, const),
                  pl.BlockSpec(su.shape, const),
                  pl.BlockSpec(sd.shape, const)],
        out_specs=pl.BlockSpec((tm, D), row),
        compiler_params=_cparams(("parallel",)),
        name="combine_shared_final",
    )(yg, wts_t, h2, x1, mod, modf, nfg, sg, su, sd)


def _tile(n, pref):
    t = min(n, pref)
    assert n % t == 0, (n, pref)
    return t


def _plan_dispatch(logits_t, p):
    T = logits_t.shape[1]
    tm_r = _tile(T, 512)
    eidx, wts, rank, counts = _route(logits_t, p["b_router"], tm_r)
    bm = _expert_block(T)
    counts = counts[:, 0]
    padded = (counts + bm - 1) // bm * bm
    pends = jnp.cumsum(padded)
    pstart = (pends - padded).astype(I32)
    nb = (T * TOP_K + N_EXPERTS * bm) // bm
    pos = _positions(eidx, rank, pstart[:, None], tm_r)
    blk_e = jnp.minimum(jnp.searchsorted(pends, jnp.arange(nb) * bm, side="right"),
                        N_EXPERTS - 1).astype(I32)
    nvalid = (pends[-1] // bm).astype(I32)
    blk_e = jnp.where(jnp.arange(nb) < nvalid, blk_e, blk_e[jnp.maximum(nvalid - 1, 0)])
    return jnp.repeat(wts.T, 2, axis=0), pos, blk_e, nvalid[None]


NORM_SLACK = 1.02
REF_GAP_LIMIT = 100.0


def _diff_attention(qh, kaug, vh, nrm, posq, p, B, S):
    qmax = jnp.sqrt(jnp.max(nrm[:, 0, 0, :2 * ATT_HEADS])) * NORM_SLACK
    kmax = jnp.sqrt(jnp.max(nrm[:, 1, 0, :2 * ATT_HEADS])) * NORM_SLACK
    in_range = 2.0 * qmax * kmax <= REF_GAP_LIMIT
    scal = jnp.concatenate([-p["slopes"], kmax[None]]).astype(F32)
    tq = _tile(S, 1024)
    tk = 2 * tq if S >= 16 * tq else tq

    def single_pass(_):
        return _attention_ref(qh, kaug, vh, scal, p["lam_p"], p["subln_g"], posq, B, S, tq, tk)

    def online(_):
        return _attention(qh, kaug, vh, p["slopes"], p["lam_p"], p["subln_g"], B, S,
                          _tile(S, 256), _tile(S, 512))

    return lax.cond(in_range, single_pass, online, None)


def _mixer_and_routing(x, mod, p):
    B, S, D = x.shape
    T = B * S
    xt = x.reshape(T, D)
    tm = _tile(S, 256)
    posq, posk = _alibi_tables(p["slopes"], S)
    proj, qh, kaug, vh, nrm = _proj(xt, mod, p["norm1_g"], p["w_in"], posk, S, tm)
    o_att = _diff_attention(qh, kaug, vh, nrm, posq, p, B, S)
    o_fw, o_bw = _hgrn2(proj, p["lbs"], B, S)
    x1, h2, logits_t = _mixout(xt, o_att, o_fw, o_bw, proj, mod, p["hg_norm_g"], p["w_out"],
                               p["norm2_g"], p["w_router"], S, _tile(S, 2 * MIXOUT_SUB_ROWS))
    wts, pos, blk_e, nvalid = _plan_dispatch(logits_t, p)
    return dict(x1=x1, h2=h2, wts=wts, pos=pos, blk_e=blk_e, nvalid=nvalid, mod=mod, shape=(B, S, D))


def _expert_rows(st, p, after=None):
    T = st["h2"].shape[0]
    bm = _expert_block(T)
    xs = _sc_dispatch(st["h2"], st["pos"], T * TOP_K + N_EXPERTS * bm)
    if after is not None:
        xs, _ = lax.optimization_barrier((xs, after))
    return _experts(xs, st["blk_e"], st["nvalid"], p["wg"], p["wu"], p["wd"], bm)


def _combine(st, ys, modf, p):
    B, S, D = st["shape"]
    yg = _sc_gather(ys, st["pos"])
    out = _final(yg, st["wts"], st["h2"], st["x1"], st["mod"], modf, p["normf_g"],
                 p["sg"], p["su"], p["sd"], S, _tile(S, 256))
    return out.reshape(B, S, D)


def _trunk(x, mod, modf, p):
    st = _mixer_and_routing(x, mod, p)
    return _combine(st, _expert_rows(st, p), modf, p)


def kernel(x_prompt, x_sample, c_prompt, c_sample, w_ada, b_ada, norm1_g, w_in, lam_q1, lam_k1, lam_q2, lam_k2, subln_g, hg_lb_logits, hg_norm_g, w_out, norm2_g, w_router, b_router, w_exp_gate, w_exp_up, w_exp_down, w_sh_gate, w_sh_up, w_sh_down, w_ada_f, b_ada_f, normf_g):
    D = x_prompt.shape[-1]
    Bp, Bs = c_prompt.shape[0], c_sample.shape[0]
    c_all = jnp.concatenate([c_prompt, c_sample], axis=0)
    R = -(-c_all.shape[0] // 8) * 8
    c_all = jnp.pad(c_all, ((0, R - c_all.shape[0]), (0, 0)))
    mod6 = _ada(c_all, w_ada[0], b_ada[0][None]).reshape(R, 6, D)
    mod2 = _ada(c_all, w_ada_f, b_ada_f[None]).reshape(R, 2, D)

    lbs = jax.nn.softmax(hg_lb_logits.astype(F32), axis=0)[0]
    lbs = lbs.reshape(2, HG_HEADS, HEAD_W).transpose(1, 0, 2)
    slopes = (jnp.exp2(-8.0 * (jnp.arange(ATT_HEADS, dtype=F32) + 1.0) / ATT_HEADS) * LOG2E)
    p = dict(
        norm1_g=norm1_g[0][None], w_in=w_in[0].astype(BF16),
        slopes=slopes.astype(F32),
        lam_p=jnp.stack([lam_q1[0], lam_k1[0], lam_q2[0], lam_k2[0]]).astype(F32),
        subln_g=subln_g[0][:, None], lbs=lbs, hg_norm_g=hg_norm_g[0][None],
        w_out=w_out[0].astype(BF16), norm2_g=norm2_g[0][None],
        w_router=jnp.concatenate(_split3(w_router[0].astype(F32))[:2], axis=1),
        b_router=b_router[0][:, None],
        wg=w_exp_gate[0], wu=w_exp_up[0], wd=w_exp_down[0],
        sg=w_sh_gate[0].astype(BF16), su=w_sh_up[0].astype(BF16), sd=w_sh_down[0].astype(BF16),
        normf_g=normf_g[None],
    )
    st_p = _mixer_and_routing(x_prompt, mod6[:Bp], p)
    x_sample, _ = lax.optimization_barrier((x_sample, st_p["pos"]))
    st_s = _mixer_and_routing(x_sample, mod6[Bp:Bp + Bs], p)
    ys_p = _expert_rows(st_p, p)
    ys_s = _expert_rows(st_s, p, after=ys_p)
    y_prompt = _combine(st_p, ys_p, mod2[:Bp], p)
    y_sample = _combine(st_s, ys_s, mod2[Bp:Bp + Bs], p)
    return (y_prompt, y_sample)
```
